```python
import jax, jax.numpy as jnp
from jax import lax
import numpy as np

D_MODEL = 2048
BATCH = 8
SEQ = 2048
DEPTH = 1

P_DIM = 256
CHUNK = 64
MIX_WIDTH = D_MODEL
MLSTM_WIDTH = MIX_WIDTH // 2
MLSTM_HEADS = 8
MLSTM_HEAD_DIM = MLSTM_WIDTH // MLSTM_HEADS
QK_CONV = 4
POOL_WIDTH = MIX_WIDTH - MLSTM_WIDTH
POOL_WINDOWS = (2, 4, 8, 16)
POOL_GROUPS = len(POOL_WINDOWS)
POOL_GROUP_WIDTH = POOL_WIDTH // POOL_GROUPS
D_FF = 4 * D_MODEL
IN_COLS = 4 * MLSTM_WIDTH + 2 * MLSTM_HEADS + POOL_WIDTH
SPLITS = [MLSTM_WIDTH, 2 * MLSTM_WIDTH, 3 * MLSTM_WIDTH, 4 * MLSTM_WIDTH,
          4 * MLSTM_WIDTH + MLSTM_HEADS, 4 * MLSTM_WIDTH + 2 * MLSTM_HEADS]
EPS = 1e-6

kernel_name = "hybrid_mlstm_pool_stream_block"


def rms_norm(x, g):
    xf = x.astype(jnp.float32)
    y = xf * lax.rsqrt(jnp.mean(xf * xf, axis=-1, keepdims=True) + EPS)
    return (y * g.astype(jnp.float32)).astype(x.dtype)


def causal_depthwise_conv(x, w):
    K = w.shape[0]
    S = x.shape[1]
    xp = jnp.pad(x, ((0, 0), (K - 1, 0), (0, 0)))
    y = xp[:, 0:S] * w[0]
    for k in range(1, K):
        y = y + xp[:, k:k + S] * w[k]
    return y


def mlstm_chunkwise(q, k, v, log_i, log_f):
    B, S, H, Dh = q.shape
    L = CHUNK
    NC = S // L

    def to_chunks(t):
        t = t.reshape((B, NC, L, H) + t.shape[3:])
        return jnp.moveaxis(t, 3, 1)

    qc = to_chunks(q)
    kc = to_chunks(k) * (Dh ** -0.5)
    vc = to_chunks(v)
    li = to_chunks(log_i)
    lf = to_chunks(log_f)
    b = jnp.cumsum(lf, axis=-1)
    b_tot = b[..., -1]
    a = b_tot[..., None] - b + li
    a_max = jnp.max(a, axis=-1)

    def step(carry, xs):
        C, n, m = carry
        k_c, v_c, a_c, amax_c, btot_c = xs
        m_new = jnp.maximum(btot_c + m, amax_c)
        decay = jnp.exp(btot_c + m - m_new)
        w = jnp.exp(a_c - m_new[..., None])
        C_new = decay[..., None, None] * C + jnp.einsum('bhl,bhld,bhle->bhde', w, v_c, k_c)
        n_new = decay[..., None] * n + jnp.einsum('bhl,bhle->bhe', w, k_c)
        return (C_new, n_new, m_new), (C, n, m)

    xs = (jnp.moveaxis(kc, 2, 0), jnp.moveaxis(vc, 2, 0), jnp.moveaxis(a, 2, 0),
          jnp.moveaxis(a_max, 2, 0), jnp.moveaxis(b_tot, 2, 0))
    init = (jnp.zeros((B, H, Dh, Dh), jnp.float32),
            jnp.zeros((B, H, Dh), jnp.float32),
            jnp.zeros((B, H), jnp.float32))
    _, (C_prev, n_prev, m_prev) = lax.scan(step, init, xs)
    C_prev = jnp.moveaxis(C_prev, 0, 2)
    n_prev = jnp.moveaxis(n_prev, 0, 2)
    m_prev = jnp.moveaxis(m_prev, 0, 2)

    inter_log = b + m_prev[..., None]
    causal = jnp.tril(jnp.ones((L, L), dtype=bool))
    Dlog = b[..., :, None] - b[..., None, :] + li[..., None, :]
    Dlog = jnp.where(causal, Dlog, -jnp.inf)
    m_t = jnp.maximum(inter_log, jnp.max(Dlog, axis=-1))
    Dw = jnp.exp(Dlog - m_t[..., None])
    inter_w = jnp.exp(inter_log - m_t)
    s = jnp.einsum('bhcld,bhcjd->bhclj', qc, kc) * Dw
    num = (inter_w[..., None] * jnp.einsum('bhcde,bhcle->bhcld', C_prev, qc)
           + jnp.einsum('bhclj,bhcjd->bhcld', s, vc))
    den = inter_w * jnp.einsum('bhce,bhcle->bhcl', n_prev, qc) + jnp.sum(s, axis=-1)
    h = num / jnp.maximum(jnp.abs(den), jnp.exp(-m_t))[..., None]
    return jnp.moveaxis(h, 1, 3).reshape(B, S, H, Dh)


def multiscale_pool(u, w_pool, scale):
    B, S, _ = u.shape
    uf = u.astype(jnp.float32)
    cs = jnp.pad(jnp.cumsum(uf, axis=1), ((0, 0), (1, 0), (0, 0)))
    t1 = jnp.arange(1, S + 1, dtype=jnp.float32)
    outs = []
    for g, w in enumerate(POOL_WINDOWS):
        sl = slice(g * POOL_GROUP_WIDTH, (g + 1) * POOL_GROUP_WIDTH)
        hi = cs[:, 1:, sl]
        lo = jnp.pad(cs[:, :S + 1 - w, sl], ((0, 0), (w - 1, 0), (0, 0)))
        mean = (hi - lo) / jnp.minimum(t1, w)[None, :, None]
        outs.append(mean - uf[..., sl])
    pooled = jnp.stack(outs, axis=2).astype(u.dtype)
    mixed = jnp.einsum('bsgc,gcd->bsgd', pooled, w_pool)
    return mixed.reshape(B, S, POOL_WIDTH) * scale


def setup_inputs(seed: int = 0) -> dict:
    key = jax.random.key(seed)
    ks = jax.random.split(key, 24)
    H = MLSTM_HEADS
    nrm = lambda k, shape, s: jax.random.normal(k, shape, jnp.float32) * s
    gain = lambda k, shape: 1.0 + 0.05 * jax.random.normal(k, shape, jnp.float32)
    f_bias = jnp.linspace(3.0, 6.0, H, dtype=jnp.float32)
    b_gates = jnp.concatenate([
        nrm(ks[3], (DEPTH, H), 0.1),
        f_bias[None, :] + nrm(ks[4], (DEPTH, H), 0.1)], axis=-1)
    return {
        "x": nrm(ks[0], (BATCH, SEQ, D_MODEL), 1.0),
        "p": nrm(ks[1], (DEPTH, BATCH, SEQ, P_DIM), 1.0),
        "w_in": nrm(ks[2], (DEPTH, D_MODEL, IN_COLS), D_MODEL ** -0.5),
        "b_gates": b_gates,
        "w_qk_conv": nrm(ks[5], (DEPTH, QK_CONV, 2 * MLSTM_WIDTH), QK_CONV ** -0.5),
        "g_mlstm": gain(ks[6], (DEPTH, MLSTM_WIDTH)),
        "w_pool": nrm(ks[7], (DEPTH, POOL_GROUPS, POOL_GROUP_WIDTH, POOL_GROUP_WIDTH), POOL_GROUP_WIDTH ** -0.5),
        "pool_scale": gain(ks[8], (DEPTH, POOL_WIDTH)),
        "w_out": nrm(ks[9], (DEPTH, MIX_WIDTH, D_MODEL), MIX_WIDTH ** -0.5),
        "g_mix_pre": gain(ks[10], (DEPTH, D_MODEL)),
        "g_mix_post": gain(ks[11], (DEPTH, D_MODEL)),
        "w_ff1": nrm(ks[12], (DEPTH, D_MODEL, D_FF), D_MODEL ** -0.5),
        "w_ff2": nrm(ks[13], (DEPTH, D_FF, D_MODEL), D_FF ** -0.5),
        "g_ff_pre": gain(ks[14], (DEPTH, D_MODEL)),
        "g_ff_post": gain(ks[15], (DEPTH, D_MODEL)),
        "w_ple_proj": nrm(ks[16], (DEPTH, P_DIM, D_MODEL), P_DIM ** -0.5),
        "w_ple_gate": nrm(ks[17], (DEPTH, D_MODEL, D_MODEL), D_MODEL ** -0.5),
        "g_ple_gate": gain(ks[18], (DEPTH, D_MODEL)),
        "g_ple_post": gain(ks[19], (DEPTH, D_MODEL)),
    }


def reference(x, p, w_in, b_gates, w_qk_conv, g_mlstm, w_pool, pool_scale, w_out,
              g_mix_pre, g_mix_post, w_ff1, w_ff2, g_ff_pre, g_ff_post,
              w_ple_proj, w_ple_gate, g_ple_gate, g_ple_post):
    B, S, _ = x.shape
    H, Dh = MLSTM_HEADS, MLSTM_HEAD_DIM
    h = x
    for i in range(DEPTH):
        xn = rms_norm(h, g_mix_pre[i])
        proj = xn @ w_in[i]
        q, k, v, o, gi, gf, u = jnp.split(proj, SPLITS, axis=-1)
        qk = jax.nn.silu(causal_depthwise_conv(jnp.concatenate([q, k], axis=-1), w_qk_conv[i]))
        q, k = jnp.split(qk, 2, axis=-1)
        gates = jnp.concatenate([gi, gf], axis=-1).astype(jnp.float32) + b_gates[i].astype(jnp.float32)
        log_i = gates[..., :H]
        log_f = jax.nn.log_sigmoid(gates[..., H:])
        cell = mlstm_chunkwise(q.reshape(B, S, H, Dh).astype(jnp.float32),
                               k.reshape(B, S, H, Dh).astype(jnp.float32),
                               v.reshape(B, S, H, Dh).astype(jnp.float32),
                               log_i, log_f).astype(x.dtype)
        cell = rms_norm(cell, g_mlstm[i].reshape(H, Dh))
        h_ml = (jax.nn.sigmoid(o).reshape(B, S, H, Dh) * cell).reshape(B, S, MLSTM_WIDTH)
        h_pool = multiscale_pool(u, w_pool[i], pool_scale[i])
        mix = jnp.concatenate([h_ml, h_pool], axis=-1) @ w_out[i]
        h = h + rms_norm(mix, g_mix_post[i])
        hn = rms_norm(h, g_ff_pre[i])
        ff = jnp.square(jax.nn.relu(hn @ w_ff1[i])) @ w_ff2[i]
        h = h + rms_norm(ff, g_ff_post[i])
        gate = jax.nn.sigmoid(rms_norm(h, g_ple_gate[i]) @ w_ple_gate[i])
        e = p[i] @ w_ple_proj[i]
        h = h + rms_norm(e * gate, g_ple_post[i])
    return h
```

```python
import functools

import jax
import jax.numpy as jnp
from jax import lax
from jax.experimental import pallas as pl
from jax.experimental.pallas import tpu as pltpu

EPS = 1e-6
HEADS = 8
HEAD_DIM = 128
MLSTM_WIDTH = HEADS * HEAD_DIM
QK_CONV = 4
POOL_WINDOWS = (2, 4, 8, 16)
POOL_GROUP_WIDTH = 256
POOL_WIDTH = POOL_GROUP_WIDTH * len(POOL_WINDOWS)

LANES = 128
SUBLANES = 8
POOL_HALO = 16
CHUNK = 128
VMEM_LIMIT = 56 * 1024 * 1024

F32 = jnp.float32
BF16 = jnp.bfloat16


def _rms(x, g):
    return x * lax.rsqrt(jnp.mean(x * x, axis=-1, keepdims=True) + EPS) * g


def _log_sigmoid(x):
    return jnp.minimum(x, 0.0) - jnp.log1p(jnp.exp(-jnp.abs(x)))


def _sigmoid(x):
    return 1.0 / (1.0 + jnp.exp(-x))


def _in_proj_kernel(x_ref, g_ref, w_ref, wgc_ref, wgr_ref, proj_ref, gcol_ref, grow_ref, xn_ref):
    @pl.when(pl.program_id(1) == 0)
    def _():
        xn = _rms(x_ref[...], g_ref[...]).astype(BF16)
        xn_ref[...] = xn
        gcol_ref[...] = jnp.dot(xn, wgc_ref[...], preferred_element_type=F32)
        grow_ref[...] = lax.dot_general(wgr_ref[...], xn, (((1,), (1,)), ((), ())),
                                        preferred_element_type=F32)

    proj_ref[...] = jnp.dot(xn_ref[...], w_ref[...], preferred_element_type=F32)


def _in_proj(x2, g, w_main, w_gcol, w_grow, tm, tn):
    m, d = x2.shape
    n = w_main.shape[1]
    return pl.pallas_call(
        _in_proj_kernel,
        grid=(m // tm, n // tn),
        in_specs=[
            pl.BlockSpec((tm, d), lambda i, j: (i, 0)),
            pl.BlockSpec((1, d), lambda i, j: (0, 0)),
            pl.BlockSpec((d, tn), lambda i, j: (0, j)),
            pl.BlockSpec((d, 2 * LANES), lambda i, j: (0, 0)),
            pl.BlockSpec((2 * SUBLANES, d), lambda i, j: (0, 0)),
        ],
        out_specs=[
            pl.BlockSpec((tm, tn), lambda i, j: (i, j)),
            pl.BlockSpec((tm, 2 * LANES), lambda i, j: (i, 0)),
            pl.BlockSpec((2 * SUBLANES, tm), lambda i, j: (0, i)),
        ],
        out_shape=[
            jax.ShapeDtypeStruct((m, n), F32),
            jax.ShapeDtypeStruct((m, 2 * LANES), F32),
            jax.ShapeDtypeStruct((2 * SUBLANES, m), F32),
        ],
        scratch_shapes=[pltpu.VMEM((tm, d), BF16)],
        compiler_params=pltpu.CompilerParams(
            dimension_semantics=("arbitrary", "arbitrary"), vmem_limit_bytes=VMEM_LIMIT),
        name="in_proj",
    )(x2, g, w_main, w_gcol, w_grow)


def _mlstm_kernel(q_ref, k_ref, v_ref, o_ref, gc_ref, gr_ref, bc_ref, br_ref, wq_ref, wk_ref,
                  gn_ref, out_ref, ct_ref, m_ref, qe_ref, ke_ref):
    L = CHUNK

    @pl.when(pl.program_id(1) == 0)
    def _():
        ct_ref[...] = jnp.zeros_like(ct_ref)
        m_ref[...] = jnp.zeros_like(m_ref)
        qe_ref[0:SUBLANES, :] = jnp.zeros((SUBLANES, MLSTM_WIDTH), F32)
        ke_ref[0:SUBLANES, :] = jnp.zeros((SUBLANES, MLSTM_WIDTH), F32)

    def conv_silu(x_ref, xe_ref, w_ref):
        x = x_ref[...]
        xe_ref[SUBLANES:SUBLANES + L, :] = x
        acc = x * w_ref[QK_CONV - 1:QK_CONV, :]
        for back in range(1, QK_CONV):
            tap = QK_CONV - 1 - back
            acc = acc + xe_ref[SUBLANES - back:SUBLANES - back + L, :] * w_ref[tap:tap + 1, :]
        xe_ref[0:SUBLANES, :] = x[L - SUBLANES:L, :]
        return acc * _sigmoid(acc)

    qc = conv_silu(q_ref, qe_ref, wq_ref).astype(BF16)
    kc = (conv_silu(k_ref, ke_ref, wk_ref) * (HEAD_DIM ** -0.5)).astype(BF16)

    gc = gc_ref[...] + bc_ref[...]
    li_c = gc[:, :LANES]
    lf_c = _log_sigmoid(gc[:, LANES:])
    gr = gr_ref[...] + br_ref[...]
    li_r = gr[0:SUBLANES, :]
    lf_r = _log_sigmoid(gr[SUBLANES:, :])
    row = lax.broadcasted_iota(jnp.int32, (L, L), 0)
    col = lax.broadcasted_iota(jnp.int32, (L, L), 1)
    causal = row >= col
    tril = jnp.where(causal, 1.0, 0.0).astype(F32)
    triu = jnp.where(row <= col, 1.0, 0.0).astype(F32)
    b_c = jnp.dot(tril, lf_c, preferred_element_type=F32, precision=lax.Precision.HIGHEST)
    b_r = jnp.dot(lf_r, triu, preferred_element_type=F32, precision=lax.Precision.HIGHEST)
    r_r = li_r - b_r
    btot = b_c[L - 1:L, :]
    a_c = btot - b_c + li_c
    m_prev = m_ref[0:1, :]
    m_new = jnp.maximum(btot + m_prev, jnp.max(a_c, axis=0, keepdims=True))
    decay = jnp.exp(btot + m_prev - m_new)
    w_c = jnp.exp(a_c - m_new)
    inter_log = b_c + m_prev
    m_ref[0:1, :] = m_new

    ones_col = jnp.where(lax.broadcasted_iota(jnp.int32, (L, LANES), 1) == 0, 1.0, 0.0).astype(F32)

    for h in range(HEADS):
        sl = slice(h * HEAD_DIM, (h + 1) * HEAD_DIM)
        qh = qc[:, sl]
        kh = kc[:, sl]
        vaug = jnp.concatenate([v_ref[:, sl], ones_col], axis=1)
        dlog = jnp.where(causal, b_c[:, h:h + 1] + r_r[h:h + 1, :], -jnp.inf)
        inter = inter_log[:, h:h + 1]
        m_t = jnp.maximum(inter, jnp.max(dlog, axis=1, keepdims=True))
        dw = jnp.exp(dlog - m_t)
        inter_w = jnp.exp(inter - m_t)
        s = lax.dot_general(qh, kh, (((1,), (1,)), ((), ())), preferred_element_type=F32) * dw
        ct = ct_ref[h]
        tot = (inter_w * jnp.dot(qh, ct.astype(BF16), preferred_element_type=F32)
               + jnp.dot(s.astype(BF16), vaug.astype(BF16), preferred_element_type=F32))
        num = tot[:, :HEAD_DIM]
        den = tot[:, HEAD_DIM:HEAD_DIM + 1]
        cell = num / jnp.maximum(jnp.abs(den), jnp.exp(-m_t))
        cell = _rms(cell, gn_ref[:, sl])
        out_ref[:, sl] = _sigmoid(o_ref[:, sl]) * cell
        wv = (w_c[:, h:h + 1] * vaug).astype(BF16)
        upd = lax.dot_general(kh, wv, (((0,), (0,)), ((), ())), preferred_element_type=F32)
        ct_ref[h] = decay[:, h:h + 1] * ct + upd


def _mlstm(proj, gcol, grow, b_col, b_row, wq, wk, gn, batch, seq):
    m = proj.shape[0]
    nc = seq // CHUNK
    row_map = lambda b, c: b * nc + c
    full = lambda shape: pl.BlockSpec(shape, lambda b, c: (0,) * len(shape))
    return pl.pallas_call(
        _mlstm_kernel,
        grid=(batch, nc),
        in_specs=[
            pl.BlockSpec((CHUNK, MLSTM_WIDTH), lambda b, c: (row_map(b, c), 0)),
            pl.BlockSpec((CHUNK, MLSTM_WIDTH), lambda b, c: (row_map(b, c), 1)),
            pl.BlockSpec((CHUNK, MLSTM_WIDTH), lambda b, c: (row_map(b, c), 2)),
            pl.BlockSpec((CHUNK, MLSTM_WIDTH), lambda b, c: (row_map(b, c), 3)),
            pl.BlockSpec((CHUNK, 2 * LANES), lambda b, c: (row_map(b, c), 0)),
            pl.BlockSpec((2 * SUBLANES, CHUNK), lambda b, c: (0, row_map(b, c))),
            full((1, 2 * LANES)),
            full((2 * SUBLANES, 1)),
            full((QK_CONV, MLSTM_WIDTH)),
            full((QK_CONV, MLSTM_WIDTH)),
            full((1, MLSTM_WIDTH)),
        ],
        out_specs=pl.BlockSpec((CHUNK, MLSTM_WIDTH), lambda b, c: (row_map(b, c), 0)),
        out_shape=jax.ShapeDtypeStruct((m, MLSTM_WIDTH), F32),
        scratch_shapes=[
            pltpu.VMEM((HEADS, HEAD_DIM, 2 * HEAD_DIM), F32),
            pltpu.VMEM((SUBLANES, LANES), F32),
            pltpu.VMEM((SUBLANES + CHUNK, MLSTM_WIDTH), F32),
            pltpu.VMEM((SUBLANES + CHUNK, MLSTM_WIDTH), F32),
        ],
        compiler_params=pltpu.CompilerParams(
            dimension_semantics=("arbitrary", "arbitrary"), vmem_limit_bytes=VMEM_LIMIT),
        name="mlstm",
    )(proj, proj, proj, proj, gcol, grow, b_col, b_row, wq, wk, gn)


def _mix_kernel(hml_ref, u_ref, x_ref, wpool_ref, pscale_ref, wo_ml_ref, wo_pool_ref, g_ref,
                out_ref, ue_ref, *, tiles_per_seq):
    tm = u_ref.shape[0]
    t = pl.program_id(0) % tiles_per_seq

    @pl.when(t == 0)
    def _():
        ue_ref[0:POOL_HALO, :] = jnp.zeros((POOL_HALO, POOL_WIDTH), F32)

    u = u_ref[...]
    ue_ref[POOL_HALO:POOL_HALO + tm, :] = u
    pos = (lax.broadcasted_iota(jnp.int32, (tm, 1), 0) + (t * tm + 1)).astype(F32)
    mix = jnp.dot(hml_ref[...].astype(BF16), wo_ml_ref[...], preferred_element_type=F32)
    for g, window in enumerate(POOL_WINDOWS):
        sl = slice(g * POOL_GROUP_WIDTH, (g + 1) * POOL_GROUP_WIDTH)
        win = ue_ref[:, sl]
        span = 1
        while span < window:
            win = win + pltpu.roll(win, span, axis=0)
            span *= 2
        mean = win[POOL_HALO:, :] / jnp.minimum(pos, float(window))
        pooled = (mean - u[:, sl]).astype(BF16)
        mixed = jnp.dot(pooled, wpool_ref[g], preferred_element_type=F32) * pscale_ref[:, sl]
        mix = mix + jnp.dot(mixed.astype(BF16), wo_pool_ref[sl, :], preferred_element_type=F32)
    ue_ref[0:POOL_HALO, :] = u[tm - POOL_HALO:tm, :]
    out_ref[...] = x_ref[...] + _rms(mix, g_ref[...])


def _mix(hml, proj, x2, w_pool, pool_scale, wo_ml, wo_pool, g_post, seq, tm):
    m, d = x2.shape
    u_blk = (4 * MLSTM_WIDTH) // POOL_WIDTH
    const = lambda shape: pl.BlockSpec(shape, lambda i: (0,) * len(shape),
                                       pipeline_mode=pl.Buffered(1))
    return pl.pallas_call(
        functools.partial(_mix_kernel, tiles_per_seq=seq // tm),
        grid=(m // tm,),
        in_specs=[
            pl.BlockSpec((tm, MLSTM_WIDTH), lambda i: (i, 0)),
            pl.BlockSpec((tm, POOL_WIDTH), lambda i: (i, u_blk)),
            pl.BlockSpec((tm, d), lambda i: (i, 0)),
            const(w_pool.shape),
            const((1, POOL_WIDTH)),
            const(wo_ml.shape),
            const(wo_pool.shape),
            const((1, d)),
        ],
        out_specs=pl.BlockSpec((tm, d), lambda i: (i, 0)),
        out_shape=jax.ShapeDtypeStruct((m, d), F32),
        scratch_shapes=[pltpu.VMEM((POOL_HALO + tm, POOL_WIDTH), F32)],
        compiler_params=pltpu.CompilerParams(
            dimension_semantics=("arbitrary",), vmem_limit_bytes=VMEM_LIMIT),
        name="mix",
    )(hml, proj, x2, w_pool, pool_scale, wo_ml, wo_pool, g_post)


def _ffn_kernel(h_ref, gpre_ref, w1_ref, w2_ref, gpost_ref, out_ref, hn_ref):
    j = pl.program_id(1)

    @pl.when(j == 0)
    def _():
        hn_ref[...] = _rms(h_ref[...], gpre_ref[...]).astype(BF16)

    a = jnp.dot(hn_ref[...], w1_ref[...], preferred_element_type=F32)
    a = jnp.square(jnp.maximum(a, 0.0)).astype(BF16)
    part = jnp.dot(a, w2_ref[...], preferred_element_type=F32)

    @pl.when(j == 0)
    def _():
        out_ref[...] = part

    @pl.when(j > 0)
    def _():
        out_ref[...] += part

    @pl.when(j == pl.num_programs(1) - 1)
    def _():
        out_ref[...] = h_ref[...] + _rms(out_ref[...], gpost_ref[...])


def _ffn(h, g_pre, w1, w2, g_post, tm, tf):
    m, d = h.shape
    f = w1.shape[1]
    return pl.pallas_call(
        _ffn_kernel,
        grid=(m // tm, f // tf),
        in_specs=[
            pl.BlockSpec((tm, d), lambda i, j: (i, 0)),
            pl.BlockSpec((1, d), lambda i, j: (0, 0)),
            pl.BlockSpec((d, tf), lambda i, j: (0, j)),
            pl.BlockSpec((tf, d), lambda i, j: (j, 0)),
            pl.BlockSpec((1, d), lambda i, j: (0, 0)),
        ],
        out_specs=pl.BlockSpec((tm, d), lambda i, j: (i, 0)),
        out_shape=jax.ShapeDtypeStruct((m, d), F32),
        scratch_shapes=[pltpu.VMEM((tm, d), BF16)],
        compiler_params=pltpu.CompilerParams(
            dimension_semantics=("arbitrary", "arbitrary"), vmem_limit_bytes=VMEM_LIMIT),
        name="ffn",
    )(h, g_pre, w1, w2, g_post)


def _ple_kernel(h_ref, p_ref, ggate_ref, wgate_ref, wproj_ref, gpost_ref, out_ref):
    h = h_ref[...]
    hn = _rms(h, ggate_ref[...]).astype(BF16)
    gate = _sigmoid(jnp.dot(hn, wgate_ref[...], preferred_element_type=F32))
    e = jnp.dot(p_ref[...].astype(BF16), wproj_ref[...], preferred_element_type=F32)
    out_ref[...] = h + _rms(e * gate, gpost_ref[...])


def _ple(h, p2, g_gate, w_gate, w_proj, g_post, tm):
    m, d = h.shape
    pd = p2.shape[1]
    const = lambda shape: pl.BlockSpec(shape, lambda i: (0,) * len(shape),
                                       pipeline_mode=pl.Buffered(1))
    return pl.pallas_call(
        _ple_kernel,
        grid=(m // tm,),
        in_specs=[
            pl.BlockSpec((tm, d), lambda i: (i, 0)),
            pl.BlockSpec((tm, pd), lambda i: (i, 0)),
            const((1, d)),
            const(w_gate.shape),
            const(w_proj.shape),
            const((1, d)),
        ],
        out_specs=pl.BlockSpec((tm, d), lambda i: (i, 0)),
        out_shape=jax.ShapeDtypeStruct((m, d), F32),
        compiler_params=pltpu.CompilerParams(
            dimension_semantics=("arbitrary",), vmem_limit_bytes=VMEM_LIMIT),
        name="ple",
    )(h, p2, g_gate, w_gate, w_proj, g_post)


def _layer(h, p2, batch, seq, w_in, b_gates, w_qk_conv, g_mlstm, w_pool, pool_scale, w_out,
           g_mix_pre, g_mix_post, w_ff1, w_ff2, g_ff_pre, g_ff_post, w_ple_proj, w_ple_gate,
           g_ple_gate, g_ple_post):
    d = h.shape[1]
    qkvo = 4 * MLSTM_WIDTH
    row = lambda v: v.reshape(1, -1).astype(F32)
    w_main = jnp.concatenate([w_in[:, :qkvo], w_in[:, qkvo + 2 * HEADS:]], axis=1).astype(BF16)
    w_gi = w_in[:, qkvo:qkvo + HEADS]
    w_gf = w_in[:, qkvo + HEADS:qkvo + 2 * HEADS]
    lane_pad = lambda w: jnp.pad(w, ((0, 0), (0, LANES - HEADS)))
    w_gcol = jnp.concatenate([lane_pad(w_gi), lane_pad(w_gf)], axis=1).astype(BF16)
    w_grow = jnp.concatenate([w_gi, w_gf], axis=1).T.astype(BF16)
    b_col = jnp.concatenate([lane_pad(b_gates[None, :HEADS]), lane_pad(b_gates[None, HEADS:])],
                            axis=1).astype(F32)
    b_row = b_gates.reshape(2 * HEADS, 1).astype(F32)

    proj, gcol, grow = _in_proj(h, row(g_mix_pre), w_main, w_gcol, w_grow, tm=1024, tn=1024)
    hml = _mlstm(proj, gcol, grow, b_col, b_row,
                 w_qk_conv[:, :MLSTM_WIDTH].astype(F32), w_qk_conv[:, MLSTM_WIDTH:].astype(F32),
                 row(g_mlstm), batch, seq)
    w_out16 = w_out.astype(BF16)
    h = _mix(hml, proj, h, w_pool.astype(BF16), row(pool_scale), w_out16[:MLSTM_WIDTH],
             w_out16[MLSTM_WIDTH:], row(g_mix_post), seq, tm=256)
    h = _ffn(h, row(g_ff_pre), w_ff1.astype(BF16), w_ff2.astype(BF16), row(g_ff_post),
             tm=512, tf=512)
    h = _ple(h, p2, row(g_ple_gate), w_ple_gate.astype(BF16), w_ple_proj.astype(BF16),
             row(g_ple_post), tm=256)
    return h


def kernel(x, p, w_in, b_gates, w_qk_conv, g_mlstm, w_pool, pool_scale, w_out, g_mix_pre,
           g_mix_post, w_ff1, w_ff2, g_ff_pre, g_ff_post, w_ple_proj, w_ple_gate, g_ple_gate,
           g_ple_post):
    batch, seq, d = x.shape
    h = x.reshape(batch * seq, d)
    for i in range(p.shape[0]):
        h = _layer(h, p[i].reshape(batch * seq, -1), batch, seq, w_in[i], b_gates[i],
                   w_qk_conv[i], g_mlstm[i], w_pool[i], pool_scale[i], w_out[i], g_mix_pre[i],
                   g_mix_post[i], w_ff1[i], w_ff2[i], g_ff_pre[i], g_ff_post[i], w_ple_proj[i],
                   w_ple_gate[i], g_ple_gate[i], g_ple_post[i])
    return h.reshape(batch, seq, d)
```

```python
import functools

import jax
import jax.numpy as jnp
from jax import lax
from jax.experimental import pallas as pl
from jax.experimental.pallas import tpu as pltpu

EPS = 1e-6
HEADS = 8
HEAD_DIM = 128
MLSTM_WIDTH = HEADS * HEAD_DIM
QK_CONV = 4
POOL_WINDOWS = (2, 4, 8, 16)
POOL_GROUP_WIDTH = 256
POOL_WIDTH = POOL_GROUP_WIDTH * len(POOL_WINDOWS)

LANES = 128
SUBLANES = 8
POOL_HALO = 16
CHUNK = 128
VMEM_LIMIT = 56 * 1024 * 1024

F32 = jnp.float32
BF16 = jnp.bfloat16


def _rms(x, g):
    return x * lax.rsqrt(jnp.mean(x * x, axis=-1, keepdims=True) + EPS) * g


def _log_sigmoid(x):
    return jnp.minimum(x, 0.0) - jnp.log1p(jnp.exp(-jnp.abs(x)))


def _sigmoid(x):
    return 1.0 / (1.0 + jnp.exp(-x))


def _in_proj_kernel(x_ref, g_ref, w_ref, wgc_ref, wgr_ref, proj_ref, gcol_ref, grow_ref, xn_ref):
    @pl.when(pl.program_id(1) == 0)
    def _():
        xn = _rms(x_ref[...], g_ref[...]).astype(BF16)
        xn_ref[...] = xn
        gcol_ref[...] = jnp.dot(xn, wgc_ref[...], preferred_element_type=F32)
        grow_ref[...] = lax.dot_general(wgr_ref[...], xn, (((1,), (1,)), ((), ())),
                                        preferred_element_type=F32)

    proj_ref[...] = jnp.dot(xn_ref[...], w_ref[...], preferred_element_type=F32)


def _in_proj(x2, g, w_main, w_gcol, w_grow, tm, tn):
    m, d = x2.shape
    n = w_main.shape[1]
    return pl.pallas_call(
        _in_proj_kernel,
        grid=(m // tm, n // tn),
        in_specs=[
            pl.BlockSpec((tm, d), lambda i, j: (i, 0)),
            pl.BlockSpec((1, d), lambda i, j: (0, 0)),
            pl.BlockSpec((d, tn), lambda i, j: (0, j)),
            pl.BlockSpec((d, 2 * LANES), lambda i, j: (0, 0)),
            pl.BlockSpec((2 * SUBLANES, d), lambda i, j: (0, 0)),
        ],
        out_specs=[
            pl.BlockSpec((tm, tn), lambda i, j: (i, j)),
            pl.BlockSpec((tm, 2 * LANES), lambda i, j: (i, 0)),
            pl.BlockSpec((2 * SUBLANES, tm), lambda i, j: (0, i)),
        ],
        out_shape=[
            jax.ShapeDtypeStruct((m, n), F32),
            jax.ShapeDtypeStruct((m, 2 * LANES), F32),
            jax.ShapeDtypeStruct((2 * SUBLANES, m), F32),
        ],
        scratch_shapes=[pltpu.VMEM((tm, d), BF16)],
        compiler_params=pltpu.CompilerParams(
            dimension_semantics=("arbitrary", "arbitrary"), vmem_limit_bytes=VMEM_LIMIT),
        name="in_proj",
    )(x2, g, w_main, w_gcol, w_grow)


def _mlstm_kernel(q_ref, k_ref, v_ref, o_ref, gc_ref, gr_ref, bc_ref, br_ref, wq_ref, wk_ref,
                  gn_ref, out_ref, ct_ref, m_ref, qe_ref, ke_ref):
    L = CHUNK

    @pl.when(pl.program_id(1) == 0)
    def _():
        ct_ref[...] = jnp.zeros_like(ct_ref)
        m_ref[...] = jnp.zeros_like(m_ref)
        qe_ref[0:SUBLANES, :] = jnp.zeros((SUBLANES, MLSTM_WIDTH), F32)
        ke_ref[0:SUBLANES, :] = jnp.zeros((SUBLANES, MLSTM_WIDTH), F32)

    def conv_silu(x_ref, xe_ref, w_ref):
        x = x_ref[...]
        xe_ref[SUBLANES:SUBLANES + L, :] = x
        acc = x * w_ref[QK_CONV - 1:QK_CONV, :]
        for back in range(1, QK_CONV):
            tap = QK_CONV - 1 - back
            acc = acc + xe_ref[SUBLANES - back:SUBLANES - back + L, :] * w_ref[tap:tap + 1, :]
        xe_ref[0:SUBLANES, :] = x[L - SUBLANES:L, :]
        return acc * _sigmoid(acc)

    qc = conv_silu(q_ref, qe_ref, wq_ref).astype(BF16)
    kc = (conv_silu(k_ref, ke_ref, wk_ref) * (HEAD_DIM ** -0.5)).astype(BF16)

    gc = gc_ref[...] + bc_ref[...]
    li_c = gc[:, :LANES]
    lf_c = _log_sigmoid(gc[:, LANES:])
    gr = gr_ref[...] + br_ref[...]
    li_r = gr[0:SUBLANES, :]
    lf_r = _log_sigmoid(gr[SUBLANES:, :])
    row = lax.broadcasted_iota(jnp.int32, (L, L), 0)
    col = lax.broadcasted_iota(jnp.int32, (L, L), 1)
    causal = row >= col
    tril = jnp.where(causal, 1.0, 0.0).astype(F32)
    triu = jnp.where(row <= col, 1.0, 0.0).astype(F32)
    b_c = jnp.dot(tril, lf_c, preferred_element_type=F32, precision=lax.Precision.HIGHEST)
    b_r = jnp.dot(lf_r, triu, preferred_element_type=F32, precision=lax.Precision.HIGHEST)
    r_r = li_r - b_r
    btot = b_c[L - 1:L, :]
    a_c = btot - b_c + li_c
    m_prev = m_ref[0:1, :]
    m_new = jnp.maximum(btot + m_prev, jnp.max(a_c, axis=0, keepdims=True))
    decay = jnp.exp(btot + m_prev - m_new)
    w_c = jnp.exp(a_c - m_new)
    inter_log = b_c + m_prev
    m_ref[0:1, :] = m_new

    ones_col = jnp.where(lax.broadcasted_iota(jnp.int32, (L, LANES), 1) == 0, 1.0, 0.0).astype(F32)

    for h in range(HEADS):
        sl = slice(h * HEAD_DIM, (h + 1) * HEAD_DIM)
        qh = qc[:, sl]
        kh = kc[:, sl]
        vaug = jnp.concatenate([v_ref[:, sl], ones_col], axis=1)
        dlog = jnp.where(causal, b_c[:, h:h + 1] + r_r[h:h + 1, :], -jnp.inf)
        inter = inter_log[:, h:h + 1]
        m_t = jnp.maximum(inter, jnp.max(dlog, axis=1, keepdims=True))
        dw = jnp.exp(dlog - m_t)
        inter_w = jnp.exp(inter - m_t)
        s = lax.dot_general(qh, kh, (((1,), (1,)), ((), ())), preferred_element_type=F32) * dw
        ct = ct_ref[h]
        tot = (inter_w * jnp.dot(qh, ct.astype(BF16), preferred_element_type=F32)
               + jnp.dot(s.astype(BF16), vaug.astype(BF16), preferred_element_type=F32))
        num = tot[:, :HEAD_DIM]
        den = tot[:, HEAD_DIM:HEAD_DIM + 1]
        cell = num / jnp.maximum(jnp.abs(den), jnp.exp(-m_t))
        cell = _rms(cell, gn_ref[:, sl])
        out_ref[:, sl] = _sigmoid(o_ref[:, sl]) * cell
        wv = (w_c[:, h:h + 1] * vaug).astype(BF16)
        upd = lax.dot_general(kh, wv, (((0,), (0,)), ((), ())), preferred_element_type=F32)
        ct_ref[h] = decay[:, h:h + 1] * ct + upd


def _mlstm(proj, gcol, grow, b_col, b_row, wq, wk, gn, batch, seq):
    m = proj.shape[0]
    nc = seq // CHUNK
    row_map = lambda b, c: b * nc + c
    full = lambda shape: pl.BlockSpec(shape, lambda b, c: (0,) * len(shape))
    return pl.pallas_call(
        _mlstm_kernel,
        grid=(batch, nc),
        in_specs=[
            pl.BlockSpec((CHUNK, MLSTM_WIDTH), lambda b, c: (row_map(b, c), 0)),
            pl.BlockSpec((CHUNK, MLSTM_WIDTH), lambda b, c: (row_map(b, c), 1)),
            pl.BlockSpec((CHUNK, MLSTM_WIDTH), lambda b, c: (row_map(b, c), 2)),
            pl.BlockSpec((CHUNK, MLSTM_WIDTH), lambda b, c: (row_map(b, c), 3)),
            pl.BlockSpec((CHUNK, 2 * LANES), lambda b, c: (row_map(b, c), 0)),
            pl.BlockSpec((2 * SUBLANES, CHUNK), lambda b, c: (0, row_map(b, c))),
            full((1, 2 * LANES)),
            full((2 * SUBLANES, 1)),
            full((QK_CONV, MLSTM_WIDTH)),
            full((QK_CONV, MLSTM_WIDTH)),
            full((1, MLSTM_WIDTH)),
        ],
        out_specs=pl.BlockSpec((CHUNK, MLSTM_WIDTH), lambda b, c: (row_map(b, c), 0)),
        out_shape=jax.ShapeDtypeStruct((m, MLSTM_WIDTH), F32),
        scratch_shapes=[
            pltpu.VMEM((HEADS, HEAD_DIM, 2 * HEAD_DIM), F32),
            pltpu.VMEM((SUBLANES, LANES), F32),
            pltpu.VMEM((SUBLANES + CHUNK, MLSTM_WIDTH), F32),
            pltpu.VMEM((SUBLANES + CHUNK, MLSTM_WIDTH), F32),
        ],
        compiler_params=pltpu.CompilerParams(
            dimension_semantics=("arbitrary", "arbitrary"), vmem_limit_bytes=VMEM_LIMIT),
        name="mlstm",
    )(proj, proj, proj, proj, gcol, grow, b_col, b_row, wq, wk, gn)


def _mix_kernel(hml_ref, u_ref, x_ref, wpool_ref, pscale_ref, wo_ml_ref, wo_pool_ref, g_ref,
                out_ref, ue_ref, *, tiles_per_seq):
    tm = u_ref.shape[0]
    t = pl.program_id(0) % tiles_per_seq

    @pl.when(t == 0)
    def _():
        ue_ref[0:POOL_HALO, :] = jnp.zeros((POOL_HALO, POOL_WIDTH), F32)

    u = u_ref[...]
    ue_ref[POOL_HALO:POOL_HALO + tm, :] = u
    pos = (lax.broadcasted_iota(jnp.int32, (tm, 1), 0) + (t * tm + 1)).astype(F32)
    mix = jnp.dot(hml_ref[...].astype(BF16), wo_ml_ref[...], preferred_element_type=F32)
    for g, window in enumerate(POOL_WINDOWS):
        sl = slice(g * POOL_GROUP_WIDTH, (g + 1) * POOL_GROUP_WIDTH)
        win = ue_ref[:, sl]
        span = 1
        while span < window:
            win = win + pltpu.roll(win, span, axis=0)
            span *= 2
        mean = win[POOL_HALO:, :] / jnp.minimum(pos, float(window))
        pooled = (mean - u[:, sl]).astype(BF16)
        mixed = jnp.dot(pooled, wpool_ref[g], preferred_element_type=F32) * pscale_ref[:, sl]
        mix = mix + jnp.dot(mixed.astype(BF16), wo_pool_ref[sl, :], preferred_element_type=F32)
    ue_ref[0:POOL_HALO, :] = u[tm - POOL_HALO:tm, :]
    out_ref[...] = x_ref[...] + _rms(mix, g_ref[...])


def _mix(hml, proj, x2, w_pool, pool_scale, wo_ml, wo_pool, g_post, seq, tm):
    m, d = x2.shape
    u_blk = (4 * MLSTM_WIDTH) // POOL_WIDTH
    const = lambda shape: pl.BlockSpec(shape, lambda i: (0,) * len(shape),
                                       pipeline_mode=pl.Buffered(1))
    return pl.pallas_call(
        functools.partial(_mix_kernel, tiles_per_seq=seq // tm),
        grid=(m // tm,),
        in_specs=[
            pl.BlockSpec((tm, MLSTM_WIDTH), lambda i: (i, 0)),
            pl.BlockSpec((tm, POOL_WIDTH), lambda i: (i, u_blk)),
            pl.BlockSpec((tm, d), lambda i: (i, 0)),
            const(w_pool.shape),
            const((1, POOL_WIDTH)),
            const(wo_ml.shape),
            const(wo_pool.shape),
            const((1, d)),
        ],
        out_specs=pl.BlockSpec((tm, d), lambda i: (i, 0)),
        out_shape=jax.ShapeDtypeStruct((m, d), F32),
        scratch_shapes=[pltpu.VMEM((POOL_HALO + tm, POOL_WIDTH), F32)],
        compiler_params=pltpu.CompilerParams(
            dimension_semantics=("arbitrary",), vmem_limit_bytes=VMEM_LIMIT),
        name="mix",
    )(hml, proj, x2, w_pool, pool_scale, wo_ml, wo_pool, g_post)


def _ffn_kernel(h_ref, gpre_ref, w1_ref, w2_ref, gpost_ref, out_ref, hn_ref):
    j = pl.program_id(1)

    @pl.when(j == 0)
    def _():
        hn_ref[...] = _rms(h_ref[...], gpre_ref[...]).astype(BF16)
        out_ref[...] = jnp.zeros_like(out_ref)

    a = jnp.dot(hn_ref[...], w1_ref[...], preferred_element_type=F32)
    a = jnp.square(jnp.maximum(a, 0.0)).astype(BF16)
    out_ref[...] += jnp.dot(a, w2_ref[...], preferred_element_type=F32)

    @pl.when(j == pl.num_programs(1) - 1)
    def _():
        out_ref[...] = h_ref[...] + _rms(out_ref[...], gpost_ref[...])


def _ffn(h, g_pre, w1, w2, g_post, tm, tf):
    m, d = h.shape
    f = w1.shape[1]
    return pl.pallas_call(
        _ffn_kernel,
        grid=(m // tm, f // tf),
        in_specs=[
            pl.BlockSpec((tm, d), lambda i, j: (i, 0)),
            pl.BlockSpec((1, d), lambda i, j: (0, 0)),
            pl.BlockSpec((d, tf), lambda i, j: (0, j)),
            pl.BlockSpec((tf, d), lambda i, j: (j, 0)),
            pl.BlockSpec((1, d), lambda i, j: (0, 0)),
        ],
        out_specs=pl.BlockSpec((tm, d), lambda i, j: (i, 0)),
        out_shape=jax.ShapeDtypeStruct((m, d), F32),
        scratch_shapes=[pltpu.VMEM((tm, d), BF16)],
        compiler_params=pltpu.CompilerParams(
            dimension_semantics=("arbitrary", "arbitrary"), vmem_limit_bytes=VMEM_LIMIT),
        name="ffn",
    )(h, g_pre, w1, w2, g_post)


def _ple_kernel(h_ref, p_ref, ggate_ref, wgate_ref, wproj_ref, gpost_ref, out_ref):
    h = h_ref[...]
    hn = _rms(h, ggate_ref[...]).astype(BF16)
    gate = _sigmoid(jnp.dot(hn, wgate_ref[...], preferred_element_type=F32))
    e = jnp.dot(p_ref[...].astype(BF16), wproj_ref[...], preferred_element_type=F32)
    out_ref[...] = h + _rms(e * gate, gpost_ref[...])


def _ple(h, p2, g_gate, w_gate, w_proj, g_post, tm):
    m, d = h.shape
    pd = p2.shape[1]
    const = lambda shape: pl.BlockSpec(shape, lambda i: (0,) * len(shape),
                                       pipeline_mode=pl.Buffered(1))
    return pl.pallas_call(
        _ple_kernel,
        grid=(m // tm,),
        in_specs=[
            pl.BlockSpec((tm, d), lambda i: (i, 0)),
            pl.BlockSpec((tm, pd), lambda i: (i, 0)),
            const((1, d)),
            const(w_gate.shape),
            const(w_proj.shape),
            const((1, d)),
        ],
        out_specs=pl.BlockSpec((tm, d), lambda i: (i, 0)),
        out_shape=jax.ShapeDtypeStruct((m, d), F32),
        compiler_params=pltpu.CompilerParams(
            dimension_semantics=("arbitrary",), vmem_limit_bytes=VMEM_LIMIT),
        name="ple",
    )(h, p2, g_gate, w_gate, w_proj, g_post)


def _layer(h, p2, batch, seq, w_in, b_gates, w_qk_conv, g_mlstm, w_pool, pool_scale, w_out,
           g_mix_pre, g_mix_post, w_ff1, w_ff2, g_ff_pre, g_ff_post, w_ple_proj, w_ple_gate,
           g_ple_gate, g_ple_post):
    d = h.shape[1]
    qkvo = 4 * MLSTM_WIDTH
    row = lambda v: v.reshape(1, -1).astype(F32)
    w_main = jnp.concatenate([w_in[:, :qkvo], w_in[:, qkvo + 2 * HEADS:]], axis=1).astype(BF16)
    w_gi = w_in[:, qkvo:qkvo + HEADS]
    w_gf = w_in[:, qkvo + HEADS:qkvo + 2 * HEADS]
    lane_pad = lambda w: jnp.pad(w, ((0, 0), (0, LANES - HEADS)))
    w_gcol = jnp.concatenate([lane_pad(w_gi), lane_pad(w_gf)], axis=1).astype(BF16)
    w_grow = jnp.concatenate([w_gi, w_gf], axis=1).T.astype(BF16)
    b_col = jnp.concatenate([lane_pad(b_gates[None, :HEADS]), lane_pad(b_gates[None, HEADS:])],
                            axis=1).astype(F32)
    b_row = b_gates.reshape(2 * HEADS, 1).astype(F32)

    proj, gcol, grow = _in_proj(h, row(g_mix_pre), w_main, w_gcol, w_grow, tm=1024, tn=1024)
    hml = _mlstm(proj, gcol, grow, b_col, b_row,
                 w_qk_conv[:, :MLSTM_WIDTH].astype(F32), w_qk_conv[:, MLSTM_WIDTH:].astype(F32),
                 row(g_mlstm), batch, seq)
    w_out16 = w_out.astype(BF16)
    h = _mix(hml, proj, h, w_pool.astype(BF16), row(pool_scale), w_out16[:MLSTM_WIDTH],
             w_out16[MLSTM_WIDTH:], row(g_mix_post), seq, tm=256)
    h = _ffn(h, row(g_ff_pre), w_ff1.astype(BF16), w_ff2.astype(BF16), row(g_ff_post),
             tm=512, tf=512)
    h = _ple(h, p2, row(g_ple_gate), w_ple_gate.astype(BF16), w_ple_proj.astype(BF16),
             row(g_ple_post), tm=256)
    return h


def kernel(x, p, w_in, b_gates, w_qk_conv, g_mlstm, w_pool, pool_scale, w_out, g_mix_pre,
           g_mix_post, w_ff1, w_ff2, g_ff_pre, g_ff_post, w_ple_proj, w_ple_gate, g_ple_gate,
           g_ple_post):
    batch, seq, d = x.shape
    h = x.reshape(batch * seq, d)
    for i in range(p.shape[0]):
        h = _layer(h, p[i].reshape(batch * seq, -1), batch, seq, w_in[i], b_gates[i],
                   w_qk_conv[i], g_mlstm[i], w_pool[i], pool_scale[i], w_out[i], g_mix_pre[i],
                   g_mix_post[i], w_ff1[i], w_ff2[i], g_ff_pre[i], g_ff_post[i], w_ple_proj[i],
                   w_ple_gate[i], g_ple_gate[i], g_ple_post[i])
    return h.reshape(batch, seq, d)
```

```python
import functools

import jax
import jax.numpy as jnp
from jax import lax
from jax.experimental import pallas as pl
from jax.experimental.pallas import tpu as pltpu

EPS = 1e-6
HEADS = 8
HEAD_DIM = 128
MLSTM_WIDTH = HEADS * HEAD_DIM
QK_CONV = 4
POOL_WINDOWS = (2, 4, 8, 16)
POOL_GROUP_WIDTH = 256
POOL_WIDTH = POOL_GROUP_WIDTH * len(POOL_WINDOWS)

LANES = 128
SUBLANES = 8
POOL_HALO = 16
CHUNK = 128
VMEM_LIMIT = 56 * 1024 * 1024

COL_TILES = 4
HEADS_PER_TILE = HEADS // COL_TILES
PAIR = HEADS_PER_TILE * HEAD_DIM
TILE_COLS = 4 * PAIR + POOL_GROUP_WIDTH
ROW_TILE = COL_TILES * CHUNK

F32 = jnp.float32
BF16 = jnp.bfloat16


def _rms(x, g):
    return x * lax.rsqrt(jnp.mean(x * x, axis=-1, keepdims=True) + EPS) * g


def _log_sigmoid(x):
    return jnp.minimum(x, 0.0) - jnp.log1p(jnp.exp(-jnp.abs(x)))


def _sigmoid(x):
    return 1.0 / (1.0 + jnp.exp(-x))


def _mlstm_chunk(first, proj_ref, gc, gr, bc_ref, br_ref, wqk_ref, gn_ref, out_ref, ct_ref, m_ref,
                 xe_ref, between):
    L = CHUNK
    between(0)

    xe_ref[0:SUBLANES, :] = jnp.where(first, 0.0, xe_ref[0:SUBLANES, :])
    qk = []
    for t in range(COL_TILES):
        cs = slice(t * 2 * PAIR, (t + 1) * 2 * PAIR)
        x = proj_ref[t, :, 0:2 * PAIR]
        xe_ref[SUBLANES:SUBLANES + L, cs] = x
        acc = x * wqk_ref[QK_CONV - 1:QK_CONV, cs]
        for back in range(1, QK_CONV):
            tap = QK_CONV - 1 - back
            acc = acc + xe_ref[SUBLANES - back:SUBLANES - back + L, cs] * wqk_ref[tap:tap + 1, cs]
        xe_ref[0:SUBLANES, cs] = x[L - SUBLANES:L, :]
        qk.append(acc * _sigmoid(acc))

    gc = gc + bc_ref[...]
    li_c = gc[:, :LANES]
    lf_c = _log_sigmoid(gc[:, LANES:])
    gr = gr + br_ref[...]
    li_r = gr[0:SUBLANES, :]
    lf_r = _log_sigmoid(gr[SUBLANES:, :])
    row = lax.broadcasted_iota(jnp.int32, (L, L), 0)
    col = lax.broadcasted_iota(jnp.int32, (L, L), 1)
    causal = row >= col
    tril = jnp.where(causal, 1.0, 0.0).astype(F32)
    triu = jnp.where(row <= col, 1.0, 0.0).astype(F32)
    b_c = jnp.dot(tril, lf_c, preferred_element_type=F32, precision=lax.Precision.HIGHEST)
    b_r = jnp.dot(lf_r, triu, preferred_element_type=F32, precision=lax.Precision.HIGHEST)
    r_r = li_r - b_r
    btot = b_c[L - 1:L, :]
    a_c = btot - b_c + li_c
    m_prev = jnp.where(first, 0.0, m_ref[0:1, :])
    m_new = jnp.maximum(btot + m_prev, jnp.max(a_c, axis=0, keepdims=True))
    decay = jnp.exp(btot + m_prev - m_new)
    w_c = jnp.exp(a_c - m_new)
    inter_log = b_c + m_prev
    m_ref[0:1, :] = m_new

    ones_col = jnp.where(lax.broadcasted_iota(jnp.int32, (L, LANES), 1) == 0, 1.0, 0.0).astype(F32)

    for h in range(HEADS):
        t, e = divmod(h, HEADS_PER_TILE)
        if e == 0:
            between(t + 1)
        es = slice(e * HEAD_DIM, (e + 1) * HEAD_DIM)
        sl = slice(h * HEAD_DIM, (h + 1) * HEAD_DIM)
        qh = qk[t][:, es].astype(BF16)
        kh = (qk[t][:, PAIR + e * HEAD_DIM:PAIR + (e + 1) * HEAD_DIM] * (HEAD_DIM ** -0.5)).astype(BF16)
        vh = proj_ref[t, :, 2 * PAIR + e * HEAD_DIM:2 * PAIR + (e + 1) * HEAD_DIM]
        oh = proj_ref[t, :, 3 * PAIR + e * HEAD_DIM:3 * PAIR + (e + 1) * HEAD_DIM]
        vaug = jnp.concatenate([vh, ones_col], axis=1)
        dlog = jnp.where(causal, b_c[:, h:h + 1] + r_r[h:h + 1, :], -jnp.inf)
        inter = inter_log[:, h:h + 1]
        m_t = jnp.maximum(inter, jnp.max(dlog, axis=1, keepdims=True))
        dw = jnp.exp(dlog - m_t)
        inter_w = jnp.exp(inter - m_t)
        s = lax.dot_general(qh, kh, (((1,), (1,)), ((), ())), preferred_element_type=F32) * dw
        ct = jnp.where(first, 0.0, ct_ref[h])
        tot = (inter_w * jnp.dot(qh, ct.astype(BF16), preferred_element_type=F32)
               + jnp.dot(s.astype(BF16), vaug.astype(BF16), preferred_element_type=F32))
        num = tot[:, :HEAD_DIM]
        den = tot[:, HEAD_DIM:HEAD_DIM + 1]
        cell = num / jnp.maximum(jnp.abs(den), jnp.exp(-m_t))
        cell = _rms(cell, gn_ref[:, sl])
        out_ref[:, sl] = _sigmoid(oh) * cell
        wv = (w_c[:, h:h + 1] * vaug).astype(BF16)
        upd = lax.dot_general(kh, wv, (((0,), (0,)), ((), ())), preferred_element_type=F32)
        ct_ref[h] = decay[:, h:h + 1] * ct + upd


def _proj_mlstm_kernel(x_ref, g_ref, w_ref, wgc_ref, wgr_ref, bc_ref, br_ref, wqk_ref, gn_ref,
                       u_ref, hml_ref, xn_ref, proj_a, proj_b, gcol_a, gcol_b, grow_a, grow_b,
                       ct_ref, m_ref, xe_ref, *, tiles_per_seq):
    i = pl.program_id(0)
    j = pl.program_id(1)

    @pl.when((i == 0) & (j == 0))
    def _():
        proj_b[...] = jnp.zeros_like(proj_b)
        gcol_b[...] = jnp.zeros_like(gcol_b)
        grow_b[...] = jnp.zeros_like(grow_b)
        ct_ref[...] = jnp.zeros_like(ct_ref)
        m_ref[...] = jnp.zeros_like(m_ref)
        xe_ref[...] = jnp.zeros_like(xe_ref)

    def step(proj_w, gcol_w, grow_w, proj_r, gcol_r, grow_r):
        @pl.when(j == 0)
        def _():
            xn = _rms(x_ref[...], g_ref[...]).astype(BF16)
            xn_ref[...] = xn
            gcol_w[...] = jnp.dot(xn, wgc_ref[...], preferred_element_type=F32)
            gr = lax.dot_general(wgr_ref[...], xn, (((1,), (1,)), ((), ())),
                                 preferred_element_type=F32)
            for c in range(COL_TILES):
                grow_w[c] = gr[:, c * CHUNK:(c + 1) * CHUNK]

        def project_piece(k):
            cs = slice(k * PAIR, (k + 1) * PAIR)
            y = jnp.dot(xn_ref[...], w_ref[:, cs], preferred_element_type=F32)
            proj_w[j, :, cs] = y
            if k == 4:
                u_ref[...] = y

        first = ((i + tiles_per_seq - 1) % tiles_per_seq == 0) & (j == 0)
        r0 = pl.multiple_of(j * CHUNK, CHUNK)
        _mlstm_chunk(first, proj_r.at[:, pl.ds(r0, CHUNK), :], gcol_r[pl.ds(r0, CHUNK), :],
                     grow_r[j], bc_ref, br_ref, wqk_ref, gn_ref, hml_ref, ct_ref, m_ref, xe_ref,
                     project_piece)

    @pl.when(i % 2 == 0)
    def _():
        step(proj_a, gcol_a, grow_a, proj_b, gcol_b, grow_b)

    @pl.when(i % 2 == 1)
    def _():
        step(proj_b, gcol_b, grow_b, proj_a, gcol_a, grow_a)


def _proj_mlstm(x2, g, w_main, w_gcol, w_grow, b_col, b_row, wqk, gn, seq):
    m, d = x2.shape
    nt = m // ROW_TILE
    last = nt - 1
    const = lambda shape: pl.BlockSpec(shape, lambda i, j: (0,) * len(shape))
    return pl.pallas_call(
        functools.partial(_proj_mlstm_kernel, tiles_per_seq=seq // ROW_TILE),
        grid=(nt + 1, COL_TILES),
        in_specs=[
            pl.BlockSpec((ROW_TILE, d), lambda i, j: (jnp.minimum(i, last), 0)),
            const((1, d)),
            pl.BlockSpec((d, TILE_COLS), lambda i, j: (0, j)),
            const((d, 2 * LANES)),
            const((2 * SUBLANES, d)),
            const((1, 2 * LANES)),
            const((2 * SUBLANES, 1)),
            const((QK_CONV, 2 * MLSTM_WIDTH)),
            const((1, MLSTM_WIDTH)),
        ],
        out_specs=[
            pl.BlockSpec((ROW_TILE, POOL_GROUP_WIDTH), lambda i, j: (i, j)),
            pl.BlockSpec((CHUNK, MLSTM_WIDTH),
                         lambda i, j: (jnp.maximum((i - 1) * COL_TILES + j, 0), 0)),
        ],
        out_shape=[
            jax.ShapeDtypeStruct((m + ROW_TILE, POOL_WIDTH), F32),
            jax.ShapeDtypeStruct((m, MLSTM_WIDTH), F32),
        ],
        scratch_shapes=[
            pltpu.VMEM((ROW_TILE, d), BF16),
            pltpu.VMEM((COL_TILES, ROW_TILE, TILE_COLS), F32),
            pltpu.VMEM((COL_TILES, ROW_TILE, TILE_COLS), F32),
            pltpu.VMEM((ROW_TILE, 2 * LANES), F32),
            pltpu.VMEM((ROW_TILE, 2 * LANES), F32),
            pltpu.VMEM((COL_TILES, 2 * SUBLANES, CHUNK), F32),
            pltpu.VMEM((COL_TILES, 2 * SUBLANES, CHUNK), F32),
            pltpu.VMEM((HEADS, HEAD_DIM, 2 * HEAD_DIM), F32),
            pltpu.VMEM((SUBLANES, LANES), F32),
            pltpu.VMEM((SUBLANES + CHUNK, 2 * MLSTM_WIDTH), F32),
        ],
        compiler_params=pltpu.CompilerParams(
            dimension_semantics=("arbitrary", "arbitrary"), vmem_limit_bytes=VMEM_LIMIT),
        name="proj_mlstm",
    )(x2, g, w_main, w_gcol, w_grow, b_col, b_row, wqk, gn)


def _mix_kernel(hml_ref, u_ref, x_ref, wpool_ref, pscale_ref, wo_ml_ref, wo_pool_ref, g_ref,
                out_ref, ue_ref, *, tiles_per_seq):
    tm = u_ref.shape[0]
    t = pl.program_id(0) % tiles_per_seq

    @pl.when(t == 0)
    def _():
        ue_ref[0:POOL_HALO, :] = jnp.zeros((POOL_HALO, POOL_WIDTH), F32)

    u = u_ref[...]
    ue_ref[POOL_HALO:POOL_HALO + tm, :] = u
    pos = (lax.broadcasted_iota(jnp.int32, (tm, 1), 0) + (t * tm + 1)).astype(F32)
    mix = jnp.dot(hml_ref[...].astype(BF16), wo_ml_ref[...], preferred_element_type=F32)
    for g, window in enumerate(POOL_WINDOWS):
        sl = slice(g * POOL_GROUP_WIDTH, (g + 1) * POOL_GROUP_WIDTH)
        win = ue_ref[:, sl]
        span = 1
        while span < window:
            win = win + pltpu.roll(win, span, axis=0)
            span *= 2
        mean = win[POOL_HALO:, :] / jnp.minimum(pos, float(window))
        pooled = (mean - u[:, sl]).astype(BF16)
        mixed = jnp.dot(pooled, wpool_ref[g], preferred_element_type=F32) * pscale_ref[:, sl]
        mix = mix + jnp.dot(mixed.astype(BF16), wo_pool_ref[sl, :], preferred_element_type=F32)
    ue_ref[0:POOL_HALO, :] = u[tm - POOL_HALO:tm, :]
    out_ref[...] = x_ref[...] + _rms(mix, g_ref[...])


def _mix(hml, u, x2, w_pool, pool_scale, wo_ml, wo_pool, g_post, seq, tm):
    m, d = x2.shape
    const = lambda shape: pl.BlockSpec(shape, lambda i: (0,) * len(shape),
                                       pipeline_mode=pl.Buffered(1))
    return pl.pallas_call(
        functools.partial(_mix_kernel, tiles_per_seq=seq // tm),
        grid=(m // tm,),
        in_specs=[
            pl.BlockSpec((tm, MLSTM_WIDTH), lambda i: (i, 0)),
            pl.BlockSpec((tm, POOL_WIDTH), lambda i: (i, 0)),
            pl.BlockSpec((tm, d), lambda i: (i, 0)),
            const(w_pool.shape),
            const((1, POOL_WIDTH)),
            const(wo_ml.shape),
            const(wo_pool.shape),
            const((1, d)),
        ],
        out_specs=pl.BlockSpec((tm, d), lambda i: (i, 0)),
        out_shape=jax.ShapeDtypeStruct((m, d), F32),
        scratch_shapes=[pltpu.VMEM((POOL_HALO + tm, POOL_WIDTH), F32)],
        compiler_params=pltpu.CompilerParams(
            dimension_semantics=("arbitrary",), vmem_limit_bytes=VMEM_LIMIT),
        name="mix",
    )(hml, u, x2, w_pool, pool_scale, wo_ml, wo_pool, g_post)


def _ffn_kernel(h_ref, gpre_ref, w1_ref, w2_ref, gpost_ref, out_ref, hn_ref):
    j = pl.program_id(1)

    @pl.when(j == 0)
    def _():
        hn_ref[...] = _rms(h_ref[...], gpre_ref[...]).astype(BF16)
        out_ref[...] = jnp.zeros_like(out_ref)

    a = jnp.dot(hn_ref[...], w1_ref[...], preferred_element_type=F32)
    a = jnp.square(jnp.maximum(a, 0.0)).astype(BF16)
    out_ref[...] += jnp.dot(a, w2_ref[...], preferred_element_type=F32)

    @pl.when(j == pl.num_programs(1) - 1)
    def _():
        out_ref[...] = h_ref[...] + _rms(out_ref[...], gpost_ref[...])


def _ffn(h, g_pre, w1, w2, g_post, tm, tf):
    m, d = h.shape
    f = w1.shape[1]
    return pl.pallas_call(
        _ffn_kernel,
        grid=(m // tm, f // tf),
        in_specs=[
            pl.BlockSpec((tm, d), lambda i, j: (i, 0)),
            pl.BlockSpec((1, d), lambda i, j: (0, 0)),
            pl.BlockSpec((d, tf), lambda i, j: (0, j)),
            pl.BlockSpec((tf, d), lambda i, j: (j, 0)),
            pl.BlockSpec((1, d), lambda i, j: (0, 0)),
        ],
        out_specs=pl.BlockSpec((tm, d), lambda i, j: (i, 0)),
        out_shape=jax.ShapeDtypeStruct((m, d), F32),
        scratch_shapes=[pltpu.VMEM((tm, d), BF16)],
        compiler_params=pltpu.CompilerParams(
            dimension_semantics=("arbitrary", "arbitrary"), vmem_limit_bytes=VMEM_LIMIT),
        name="ffn",
    )(h, g_pre, w1, w2, g_post)


def _ple_kernel(h_ref, p_ref, ggate_ref, wgate_ref, wproj_ref, gpost_ref, out_ref):
    h = h_ref[...]
    hn = _rms(h, ggate_ref[...]).astype(BF16)
    gate = _sigmoid(jnp.dot(hn, wgate_ref[...], preferred_element_type=F32))
    e = jnp.dot(p_ref[...].astype(BF16), wproj_ref[...], preferred_element_type=F32)
    out_ref[...] = h + _rms(e * gate, gpost_ref[...])


def _ple(h, p2, g_gate, w_gate, w_proj, g_post, tm):
    m, d = h.shape
    pd = p2.shape[1]
    const = lambda shape: pl.BlockSpec(shape, lambda i: (0,) * len(shape),
                                       pipeline_mode=pl.Buffered(1))
    return pl.pallas_call(
        _ple_kernel,
        grid=(m // tm,),
        in_specs=[
            pl.BlockSpec((tm, d), lambda i: (i, 0)),
            pl.BlockSpec((tm, pd), lambda i: (i, 0)),
            const((1, d)),
            const(w_gate.shape),
            const(w_proj.shape),
            const((1, d)),
        ],
        out_specs=pl.BlockSpec((tm, d), lambda i: (i, 0)),
        out_shape=jax.ShapeDtypeStruct((m, d), F32),
        compiler_params=pltpu.CompilerParams(
            dimension_semantics=("arbitrary",), vmem_limit_bytes=VMEM_LIMIT),
        name="ple",
    )(h, p2, g_gate, w_gate, w_proj, g_post)


def _tile_columns(w_qkvo, w_u):
    d = w_qkvo.shape[0]
    a = w_qkvo.reshape(d, 4, COL_TILES, PAIR).transpose(0, 2, 1, 3)
    b = w_u.reshape(d, COL_TILES, 1, POOL_GROUP_WIDTH)
    return jnp.concatenate([a, b], axis=2).reshape(d, COL_TILES * TILE_COLS)


def _tile_qk_columns(w_q, w_k):
    r = w_q.shape[0]
    a = jnp.stack([w_q.reshape(r, COL_TILES, PAIR), w_k.reshape(r, COL_TILES, PAIR)], axis=2)
    return a.reshape(r, 2 * MLSTM_WIDTH)


def _layer(h, p2, seq, w_in, b_gates, w_qk_conv, g_mlstm, w_pool, pool_scale, w_out,
           g_mix_pre, g_mix_post, w_ff1, w_ff2, g_ff_pre, g_ff_post, w_ple_proj, w_ple_gate,
           g_ple_gate, g_ple_post):
    qkvo = 4 * MLSTM_WIDTH
    row = lambda v: v.reshape(1, -1).astype(F32)
    w_main = _tile_columns(w_in[:, :qkvo], w_in[:, qkvo + 2 * HEADS:]).astype(BF16)
    w_gi = w_in[:, qkvo:qkvo + HEADS]
    w_gf = w_in[:, qkvo + HEADS:qkvo + 2 * HEADS]
    lane_pad = lambda w: jnp.pad(w, ((0, 0), (0, LANES - HEADS)))
    w_gcol = jnp.concatenate([lane_pad(w_gi), lane_pad(w_gf)], axis=1).astype(BF16)
    w_grow = jnp.concatenate([w_gi, w_gf], axis=1).T.astype(BF16)
    b_col = jnp.concatenate([lane_pad(b_gates[None, :HEADS]), lane_pad(b_gates[None, HEADS:])],
                            axis=1).astype(F32)
    b_row = b_gates.reshape(2 * HEADS, 1).astype(F32)
    wqk = _tile_qk_columns(w_qk_conv[:, :MLSTM_WIDTH], w_qk_conv[:, MLSTM_WIDTH:]).astype(F32)

    u, hml = _proj_mlstm(h, row(g_mix_pre), w_main, w_gcol, w_grow, b_col, b_row, wqk,
                         row(g_mlstm), seq)
    w_out16 = w_out.astype(BF16)
    h = _mix(hml, u, h, w_pool.astype(BF16), row(pool_scale), w_out16[:MLSTM_WIDTH],
             w_out16[MLSTM_WIDTH:], row(g_mix_post), seq, tm=256)
    h = _ffn(h, row(g_ff_pre), w_ff1.astype(BF16), w_ff2.astype(BF16), row(g_ff_post),
             tm=512, tf=1024)
    h = _ple(h, p2, row(g_ple_gate), w_ple_gate.astype(BF16), w_ple_proj.astype(BF16),
             row(g_ple_post), tm=256)
    return h


def kernel(x, p, w_in, b_gates, w_qk_conv, g_mlstm, w_pool, pool_scale, w_out, g_mix_pre,
           g_mix_post, w_ff1, w_ff2, g_ff_pre, g_ff_post, w_ple_proj, w_ple_gate, g_ple_gate,
           g_ple_post):
    batch, seq, d = x.shape
    h = x.reshape(batch * seq, d)
    for i in range(p.shape[0]):
        h = _layer(h, p[i].reshape(batch * seq, -1), seq, w_in[i], b_gates[i],
                   w_qk_conv[i], g_mlstm[i], w_pool[i], pool_scale[i], w_out[i], g_mix_pre[i],
                   g_mix_post[i], w_ff1[i], w_ff2[i], g_ff_pre[i], g_ff_post[i], w_ple_proj[i],
                   w_ple_gate[i], g_ple_gate[i], g_ple_post[i])
    return h.reshape(batch, seq, d)
```

```python
import functools

import jax
import jax.numpy as jnp
from jax import lax
from jax.experimental import pallas as pl
from jax.experimental.pallas import tpu as pltpu

EPS = 1e-6
HEADS = 8
HEAD_DIM = 128
MLSTM_WIDTH = HEADS * HEAD_DIM
QK_CONV = 4
POOL_WINDOWS = (2, 4, 8, 16)
POOL_GROUP_WIDTH = 256
POOL_WIDTH = POOL_GROUP_WIDTH * len(POOL_WINDOWS)

LANES = 128
SUBLANES = 8
POOL_HALO = 16
CHUNK = 128
VMEM_LIMIT = 56 * 1024 * 1024

COL_TILES = 4
HEADS_PER_TILE = HEADS // COL_TILES
PAIR = HEADS_PER_TILE * HEAD_DIM
TILE_COLS = 4 * PAIR + POOL_GROUP_WIDTH
ROW_TILE = COL_TILES * CHUNK

F32 = jnp.float32
BF16 = jnp.bfloat16


def _rms(x, g):
    return x * lax.rsqrt(jnp.mean(x * x, axis=-1, keepdims=True) + EPS) * g


def _log_sigmoid(x):
    return jnp.minimum(x, 0.0) - jnp.log1p(jnp.exp(-jnp.abs(x)))


def _sigmoid(x):
    return 1.0 / (1.0 + jnp.exp(-x))


def _mlstm_chunk(first, proj_ref, gc, gr, bc_ref, br_ref, wqk_ref, gn_ref, out_ref, ct_ref, m_ref,
                 xe_ref, between):
    L = CHUNK
    between(0)

    xe_ref[0:SUBLANES, :] = jnp.where(first, 0.0, xe_ref[0:SUBLANES, :])
    qk = []
    for t in range(COL_TILES):
        cs = slice(t * 2 * PAIR, (t + 1) * 2 * PAIR)
        x = proj_ref[t, :, 0:2 * PAIR]
        xe_ref[SUBLANES:SUBLANES + L, cs] = x
        acc = x * wqk_ref[QK_CONV - 1:QK_CONV, cs]
        for back in range(1, QK_CONV):
            tap = QK_CONV - 1 - back
            acc = acc + xe_ref[SUBLANES - back:SUBLANES - back + L, cs] * wqk_ref[tap:tap + 1, cs]
        xe_ref[0:SUBLANES, cs] = x[L - SUBLANES:L, :]
        qk.append(acc * _sigmoid(acc))

    gc = gc + bc_ref[...]
    li_c = gc[:, :LANES]
    lf_c = _log_sigmoid(gc[:, LANES:])
    gr = gr + br_ref[...]
    li_r = gr[0:SUBLANES, :]
    lf_r = _log_sigmoid(gr[SUBLANES:, :])
    row = lax.broadcasted_iota(jnp.int32, (L, L), 0)
    col = lax.broadcasted_iota(jnp.int32, (L, L), 1)
    causal = row >= col
    tril = jnp.where(causal, 1.0, 0.0).astype(F32)
    triu = jnp.where(row <= col, 1.0, 0.0).astype(F32)
    b_c = jnp.dot(tril, lf_c, preferred_element_type=F32, precision=lax.Precision.HIGHEST)
    b_r = jnp.dot(lf_r, triu, preferred_element_type=F32, precision=lax.Precision.HIGHEST)
    r_r = li_r - b_r
    btot = b_c[L - 1:L, :]
    a_c = btot - b_c + li_c
    m_prev = jnp.where(first, 0.0, m_ref[0:1, :])
    m_new = jnp.maximum(btot + m_prev, jnp.max(a_c, axis=0, keepdims=True))
    decay = jnp.exp(btot + m_prev - m_new)
    w_c = jnp.exp(a_c - m_new)
    inter_log = b_c + m_prev
    m_ref[0:1, :] = m_new

    ones_col = jnp.where(lax.broadcasted_iota(jnp.int32, (L, LANES), 1) == 0, 1.0, 0.0).astype(F32)

    for h in range(HEADS):
        t, e = divmod(h, HEADS_PER_TILE)
        if e == 0:
            between(t + 1)
        es = slice(e * HEAD_DIM, (e + 1) * HEAD_DIM)
        sl = slice(h * HEAD_DIM, (h + 1) * HEAD_DIM)
        qh = qk[t][:, es].astype(BF16)
        kh = (qk[t][:, PAIR + e * HEAD_DIM:PAIR + (e + 1) * HEAD_DIM] * (HEAD_DIM ** -0.5)).astype(BF16)
        vh = proj_ref[t, :, 2 * PAIR + e * HEAD_DIM:2 * PAIR + (e + 1) * HEAD_DIM]
        oh = proj_ref[t, :, 3 * PAIR + e * HEAD_DIM:3 * PAIR + (e + 1) * HEAD_DIM]
        vaug = jnp.concatenate([vh, ones_col], axis=1)
        dlog = jnp.where(causal, b_c[:, h:h + 1] + r_r[h:h + 1, :], -jnp.inf)
        inter = inter_log[:, h:h + 1]
        m_t = jnp.maximum(inter, jnp.max(dlog, axis=1, keepdims=True))
        dw = jnp.exp(dlog - m_t)
        inter_w = jnp.exp(inter - m_t)
        s = lax.dot_general(qh, kh, (((1,), (1,)), ((), ())), preferred_element_type=F32) * dw
        ct = jnp.where(first, 0.0, ct_ref[h])
        tot = (inter_w * jnp.dot(qh, ct.astype(BF16), preferred_element_type=F32)
               + jnp.dot(s.astype(BF16), vaug.astype(BF16), preferred_element_type=F32))
        num = tot[:, :HEAD_DIM]
        den = tot[:, HEAD_DIM:HEAD_DIM + 1]
        cell = num / jnp.maximum(jnp.abs(den), jnp.exp(-m_t))
        cell = _rms(cell, gn_ref[:, sl])
        out_ref[:, sl] = _sigmoid(oh) * cell
        wv = (w_c[:, h:h + 1] * vaug).astype(BF16)
        upd = lax.dot_general(kh, wv, (((0,), (0,)), ((), ())), preferred_element_type=F32)
        ct_ref[h] = decay[:, h:h + 1] * ct + upd


def _proj_mlstm_kernel(x_ref, g_ref, wq_ref, wk_ref, wv_ref, wo_ref, wu_ref, wgc_ref, wgr_ref,
                       bc_ref, br_ref, wqk_ref, gn_ref,
                       u_ref, hml_ref, xn_ref, proj_a, proj_b, gcol_a, gcol_b, grow_a, grow_b,
                       ct_ref, m_ref, xe_ref, *, tiles_per_seq):
    i = pl.program_id(0)
    j = pl.program_id(1)
    w_refs = (wq_ref, wk_ref, wv_ref, wo_ref, wu_ref)

    @pl.when((i == 0) & (j == 0))
    def _():
        proj_b[...] = jnp.zeros_like(proj_b)
        gcol_b[...] = jnp.zeros_like(gcol_b)
        grow_b[...] = jnp.zeros_like(grow_b)
        ct_ref[...] = jnp.zeros_like(ct_ref)
        m_ref[...] = jnp.zeros_like(m_ref)
        xe_ref[...] = jnp.zeros_like(xe_ref)

    def step(proj_w, gcol_w, grow_w, proj_r, gcol_r, grow_r):
        @pl.when(j == 0)
        def _():
            xn = _rms(x_ref[...], g_ref[...]).astype(BF16)
            xn_ref[...] = xn
            gcol_w[...] = jnp.dot(xn, wgc_ref[...], preferred_element_type=F32)
            gr = lax.dot_general(wgr_ref[...], xn, (((1,), (1,)), ((), ())),
                                 preferred_element_type=F32)
            for c in range(COL_TILES):
                grow_w[c] = gr[:, c * CHUNK:(c + 1) * CHUNK]

        def project_piece(k):
            cs = slice(k * PAIR, (k + 1) * PAIR)
            y = jnp.dot(xn_ref[...], w_refs[k][...], preferred_element_type=F32)
            proj_w[j, :, cs] = y
            if k == 4:
                u_ref[...] = y

        first = ((i + tiles_per_seq - 1) % tiles_per_seq == 0) & (j == 0)
        r0 = pl.multiple_of(j * CHUNK, CHUNK)
        _mlstm_chunk(first, proj_r.at[:, pl.ds(r0, CHUNK), :], gcol_r[pl.ds(r0, CHUNK), :],
                     grow_r[j], bc_ref, br_ref, wqk_ref, gn_ref, hml_ref, ct_ref, m_ref, xe_ref,
                     project_piece)

    @pl.when(i % 2 == 0)
    def _():
        step(proj_a, gcol_a, grow_a, proj_b, gcol_b, grow_b)

    @pl.when(i % 2 == 1)
    def _():
        step(proj_b, gcol_b, grow_b, proj_a, gcol_a, grow_a)


def _proj_mlstm(x2, g, w_in16, w_u, w_gcol, w_grow, b_col, b_row, wqk, gn, seq):
    m, d = x2.shape
    nt = m // ROW_TILE
    last = nt - 1
    const = lambda shape: pl.BlockSpec(shape, lambda i, j: (0,) * len(shape))
    slab = lambda s: pl.BlockSpec((d, PAIR), lambda i, j: (0, s * COL_TILES + j))
    return pl.pallas_call(
        functools.partial(_proj_mlstm_kernel, tiles_per_seq=seq // ROW_TILE),
        grid=(nt + 1, COL_TILES),
        in_specs=[
            pl.BlockSpec((ROW_TILE, d), lambda i, j: (jnp.minimum(i, last), 0)),
            const((1, d)),
            slab(0), slab(1), slab(2), slab(3),
            pl.BlockSpec((d, POOL_GROUP_WIDTH), lambda i, j: (0, j)),
            const((d, 2 * LANES)),
            const((2 * SUBLANES, d)),
            const((1, 2 * LANES)),
            const((2 * SUBLANES, 1)),
            const((QK_CONV, 2 * MLSTM_WIDTH)),
            const((1, MLSTM_WIDTH)),
        ],
        out_specs=[
            pl.BlockSpec((ROW_TILE, POOL_GROUP_WIDTH), lambda i, j: (i, j)),
            pl.BlockSpec((CHUNK, MLSTM_WIDTH),
                         lambda i, j: (jnp.maximum((i - 1) * COL_TILES + j, 0), 0)),
        ],
        out_shape=[
            jax.ShapeDtypeStruct((m + ROW_TILE, POOL_WIDTH), F32),
            jax.ShapeDtypeStruct((m, MLSTM_WIDTH), F32),
        ],
        scratch_shapes=[
            pltpu.VMEM((ROW_TILE, d), BF16),
            pltpu.VMEM((COL_TILES, ROW_TILE, TILE_COLS), F32),
            pltpu.VMEM((COL_TILES, ROW_TILE, TILE_COLS), F32),
            pltpu.VMEM((ROW_TILE, 2 * LANES), F32),
            pltpu.VMEM((ROW_TILE, 2 * LANES), F32),
            pltpu.VMEM((COL_TILES, 2 * SUBLANES, CHUNK), F32),
            pltpu.VMEM((COL_TILES, 2 * SUBLANES, CHUNK), F32),
            pltpu.VMEM((HEADS, HEAD_DIM, 2 * HEAD_DIM), F32),
            pltpu.VMEM((SUBLANES, LANES), F32),
            pltpu.VMEM((SUBLANES + CHUNK, 2 * MLSTM_WIDTH), F32),
        ],
        compiler_params=pltpu.CompilerParams(
            dimension_semantics=("arbitrary", "arbitrary"), vmem_limit_bytes=VMEM_LIMIT),
        name="proj_mlstm",
    )(x2, g, w_in16, w_in16, w_in16, w_in16, w_u, w_gcol, w_grow, b_col, b_row, wqk, gn)


def _mix_kernel(hml_ref, u_ref, x_ref, wpool_ref, pscale_ref, wo_ml_ref, wo_pool_ref, g_ref,
                gnext_ref, out_ref, hn_ref, ue_ref, *, tiles_per_seq):
    tm = u_ref.shape[0]
    t = pl.program_id(0) % tiles_per_seq

    @pl.when(t == 0)
    def _():
        ue_ref[0:POOL_HALO, :] = jnp.zeros((POOL_HALO, POOL_WIDTH), F32)

    u = u_ref[...]
    ue_ref[POOL_HALO:POOL_HALO + tm, :] = u
    pos = (lax.broadcasted_iota(jnp.int32, (tm, 1), 0) + (t * tm + 1)).astype(F32)
    mix = jnp.dot(hml_ref[...].astype(BF16), wo_ml_ref[...], preferred_element_type=F32)
    for g, window in enumerate(POOL_WINDOWS):
        sl = slice(g * POOL_GROUP_WIDTH, (g + 1) * POOL_GROUP_WIDTH)
        win = ue_ref[:, sl]
        span = 1
        while span < window:
            win = win + pltpu.roll(win, span, axis=0)
            span *= 2
        mean = win[POOL_HALO:, :] / jnp.minimum(pos, float(window))
        pooled = (mean - u[:, sl]).astype(BF16)
        mixed = jnp.dot(pooled, wpool_ref[g], preferred_element_type=F32) * pscale_ref[:, sl]
        mix = mix + jnp.dot(mixed.astype(BF16), wo_pool_ref[sl, :], preferred_element_type=F32)
    ue_ref[0:POOL_HALO, :] = u[tm - POOL_HALO:tm, :]
    h = x_ref[...] + _rms(mix, g_ref[...])
    out_ref[...] = h
    hn_ref[...] = _rms(h, gnext_ref[...]).astype(BF16)


def _mix(hml, u, x2, w_pool, pool_scale, wo_ml, wo_pool, g_post, g_next, seq, tm):
    m, d = x2.shape
    const = lambda shape: pl.BlockSpec(shape, lambda i: (0,) * len(shape),
                                       pipeline_mode=pl.Buffered(1))
    return pl.pallas_call(
        functools.partial(_mix_kernel, tiles_per_seq=seq // tm),
        grid=(m // tm,),
        in_specs=[
            pl.BlockSpec((tm, MLSTM_WIDTH), lambda i: (i, 0)),
            pl.BlockSpec((tm, POOL_WIDTH), lambda i: (i, 0)),
            pl.BlockSpec((tm, d), lambda i: (i, 0)),
            const(w_pool.shape),
            const((1, POOL_WIDTH)),
            const(wo_ml.shape),
            const(wo_pool.shape),
            const((1, d)),
            const((1, d)),
        ],
        out_specs=[pl.BlockSpec((tm, d), lambda i: (i, 0)), pl.BlockSpec((tm, d), lambda i: (i, 0))],
        out_shape=[jax.ShapeDtypeStruct((m, d), F32), jax.ShapeDtypeStruct((m, d), BF16)],
        scratch_shapes=[pltpu.VMEM((POOL_HALO + tm, POOL_WIDTH), F32)],
        compiler_params=pltpu.CompilerParams(
            dimension_semantics=("arbitrary",), vmem_limit_bytes=VMEM_LIMIT),
        name="mix",
    )(hml, u, x2, w_pool, pool_scale, wo_ml, wo_pool, g_post, g_next)


FFN_ROW_SPLIT = 2


def _ffn_kernel(hn_ref, w1_ref, w2_ref, acc_ref):
    j = pl.program_id(1)
    rows = hn_ref.shape[0] // FFN_ROW_SPLIT
    groups = [slice(r * rows, (r + 1) * rows) for r in range(FFN_ROW_SPLIT)]
    acts = []
    for g in groups:
        a = jnp.dot(hn_ref[g, :], w1_ref[...], preferred_element_type=F32)
        acts.append(jnp.square(jnp.maximum(a, 0.0)).astype(BF16))
    for g, a in zip(groups, acts):
        prev = jnp.where(j == 0, 0.0, acc_ref[g, :])
        acc_ref[g, :] = prev + jnp.dot(a, w2_ref[...], preferred_element_type=F32)


def _ffn(hn, w1, w2, tm, tf):
    m, d = hn.shape
    f = w1.shape[1]
    return pl.pallas_call(
        _ffn_kernel,
        grid=(m // tm, f // tf),
        in_specs=[
            pl.BlockSpec((tm, d), lambda i, j: (i, 0)),
            pl.BlockSpec((d, tf), lambda i, j: (0, j)),
            pl.BlockSpec((tf, d), lambda i, j: (j, 0)),
        ],
        out_specs=pl.BlockSpec((tm, d), lambda i, j: (i, 0)),
        out_shape=jax.ShapeDtypeStruct((m, d), F32),
        compiler_params=pltpu.CompilerParams(
            dimension_semantics=("arbitrary", "arbitrary"), vmem_limit_bytes=VMEM_LIMIT),
        name="ffn",
    )(hn, w1, w2)


def _ple_kernel(h_ref, ff_ref, p_ref, gff_ref, ggate_ref, wgate_ref, wproj_ref, gpost_ref, out_ref):
    h = h_ref[...] + _rms(ff_ref[...], gff_ref[...])
    hn = _rms(h, ggate_ref[...]).astype(BF16)
    gate = _sigmoid(jnp.dot(hn, wgate_ref[...], preferred_element_type=F32))
    e = jnp.dot(p_ref[...].astype(BF16), wproj_ref[...], preferred_element_type=F32)
    out_ref[...] = h + _rms(e * gate, gpost_ref[...])


def _ple(h, ff, p2, g_ff_post, g_gate, w_gate, w_proj, g_post, tm):
    m, d = h.shape
    pd = p2.shape[1]
    const = lambda shape: pl.BlockSpec(shape, lambda i: (0,) * len(shape),
                                       pipeline_mode=pl.Buffered(1))
    return pl.pallas_call(
        _ple_kernel,
        grid=(m // tm,),
        in_specs=[
            pl.BlockSpec((tm, d), lambda i: (i, 0)),
            pl.BlockSpec((tm, d), lambda i: (i, 0)),
            pl.BlockSpec((tm, pd), lambda i: (i, 0)),
            const((1, d)),
            const((1, d)),
            const(w_gate.shape),
            const(w_proj.shape),
            const((1, d)),
        ],
        out_specs=pl.BlockSpec((tm, d), lambda i: (i, 0)),
        out_shape=jax.ShapeDtypeStruct((m, d), F32),
        compiler_params=pltpu.CompilerParams(
            dimension_semantics=("arbitrary",), vmem_limit_bytes=VMEM_LIMIT),
        name="ple",
    )(h, ff, p2, g_ff_post, g_gate, w_gate, w_proj, g_post)


def _tile_qk_columns(w_q, w_k):
    r = w_q.shape[0]
    a = jnp.stack([w_q.reshape(r, COL_TILES, PAIR), w_k.reshape(r, COL_TILES, PAIR)], axis=2)
    return a.reshape(r, 2 * MLSTM_WIDTH)


def _layer(h, p2, seq, w_in, b_gates, w_qk_conv, g_mlstm, w_pool, pool_scale, w_out,
           g_mix_pre, g_mix_post, w_ff1, w_ff2, g_ff_pre, g_ff_post, w_ple_proj, w_ple_gate,
           g_ple_gate, g_ple_post):
    qkvo = 4 * MLSTM_WIDTH
    row = lambda v: v.reshape(1, -1).astype(F32)
    w_in16 = w_in.astype(BF16)
    w_u = w_in16[:, qkvo + 2 * HEADS:]
    w_gi = w_in16[:, qkvo:qkvo + HEADS]
    w_gf = w_in16[:, qkvo + HEADS:qkvo + 2 * HEADS]
    lane_pad = lambda w: jnp.pad(w, ((0, 0), (0, LANES - HEADS)))
    w_gcol = jnp.concatenate([lane_pad(w_gi), lane_pad(w_gf)], axis=1).astype(BF16)
    w_grow = jnp.concatenate([w_gi, w_gf], axis=1).T.astype(BF16)
    b_col = jnp.concatenate([lane_pad(b_gates[None, :HEADS]), lane_pad(b_gates[None, HEADS:])],
                            axis=1).astype(F32)
    b_row = b_gates.reshape(2 * HEADS, 1).astype(F32)
    wqk = _tile_qk_columns(w_qk_conv[:, :MLSTM_WIDTH], w_qk_conv[:, MLSTM_WIDTH:]).astype(F32)

    u, hml = _proj_mlstm(h, row(g_mix_pre), w_in16, w_u, w_gcol, w_grow, b_col, b_row, wqk,
                         row(g_mlstm), seq)
    w_out16 = w_out.astype(BF16)
    h, hn = _mix(hml, u, h, w_pool.astype(BF16), row(pool_scale), w_out16[:MLSTM_WIDTH],
                 w_out16[MLSTM_WIDTH:], row(g_mix_post), row(g_ff_pre), seq, tm=256)
    ff = _ffn(hn, w_ff1.astype(BF16), w_ff2.astype(BF16), tm=1024, tf=512)
    return _ple(h, ff, p2, row(g_ff_post), row(g_ple_gate), w_ple_gate.astype(BF16),
                w_ple_proj.astype(BF16), row(g_ple_post), tm=256)


def kernel(x, p, w_in, b_gates, w_qk_conv, g_mlstm, w_pool, pool_scale, w_out, g_mix_pre,
           g_mix_post, w_ff1, w_ff2, g_ff_pre, g_ff_post, w_ple_proj, w_ple_gate, g_ple_gate,
           g_ple_post):
    batch, seq, d = x.shape
    h = x.reshape(batch * seq, d)
    for i in range(p.shape[0]):
        h = _layer(h, p[i].reshape(batch * seq, -1), seq, w_in[i], b_gates[i],
                   w_qk_conv[i], g_mlstm[i], w_pool[i], pool_scale[i], w_out[i], g_mix_pre[i],
                   g_mix_post[i], w_ff1[i], w_ff2[i], g_ff_pre[i], g_ff_post[i], w_ple_proj[i],
                   w_ple_gate[i], g_ple_gate[i], g_ple_post[i])
    return h.reshape(batch, seq, d)
```

```python
import functools

import jax
import jax.numpy as jnp
from jax import lax
from jax.experimental import pallas as pl
from jax.experimental.pallas import tpu as pltpu

EPS = 1e-6
HEADS = 8
HEAD_DIM = 128
MLSTM_WIDTH = HEADS * HEAD_DIM
QK_CONV = 4
POOL_WINDOWS = (2, 4, 8, 16)
POOL_GROUP_WIDTH = 256
POOL_WIDTH = POOL_GROUP_WIDTH * len(POOL_WINDOWS)

LANES = 128
SUBLANES = 8
POOL_HALO = 16
CHUNK = 128
VMEM_LIMIT = 56 * 1024 * 1024

COL_TILES = 4
HEADS_PER_TILE = HEADS // COL_TILES
PAIR = HEADS_PER_TILE * HEAD_DIM
TILE_COLS = 4 * PAIR + POOL_GROUP_WIDTH
ROW_TILE = COL_TILES * CHUNK

F32 = jnp.float32
BF16 = jnp.bfloat16


def _rms(x, g):
    return x * lax.rsqrt(jnp.mean(x * x, axis=-1, keepdims=True) + EPS) * g


def _log_sigmoid(x):
    return jnp.minimum(x, 0.0) - jnp.log1p(jnp.exp(-jnp.abs(x)))


def _sigmoid(x):
    return 1.0 / (1.0 + jnp.exp(-x))


def _mlstm_chunk(first, proj_ref, gc, gr, bc_ref, br_ref, wqk_ref, gn_ref, out_ref, ct_ref, m_ref,
                 xe_ref, between):
    L = CHUNK
    between(0)

    xe_ref[0:SUBLANES, :] = jnp.where(first, 0.0, xe_ref[0:SUBLANES, :])
    qk = []
    for t in range(COL_TILES):
        cs = slice(t * 2 * PAIR, (t + 1) * 2 * PAIR)
        x = proj_ref[t, :, 0:2 * PAIR]
        xe_ref[SUBLANES:SUBLANES + L, cs] = x
        acc = x * wqk_ref[QK_CONV - 1:QK_CONV, cs]
        for back in range(1, QK_CONV):
            tap = QK_CONV - 1 - back
            acc = acc + xe_ref[SUBLANES - back:SUBLANES - back + L, cs] * wqk_ref[tap:tap + 1, cs]
        xe_ref[0:SUBLANES, cs] = x[L - SUBLANES:L, :]
        qk.append(acc * _sigmoid(acc))

    gc = gc + bc_ref[...]
    li_c = gc[:, :LANES]
    lf_c = _log_sigmoid(gc[:, LANES:])
    gr = gr + br_ref[...]
    li_r = gr[0:SUBLANES, :]
    lf_r = _log_sigmoid(gr[SUBLANES:, :])
    row = lax.broadcasted_iota(jnp.int32, (L, L), 0)
    col = lax.broadcasted_iota(jnp.int32, (L, L), 1)
    causal = row >= col
    tril = jnp.where(causal, 1.0, 0.0).astype(F32)
    triu = jnp.where(row <= col, 1.0, 0.0).astype(F32)
    b_c = jnp.dot(tril, lf_c, preferred_element_type=F32, precision=lax.Precision.HIGHEST)
    b_r = jnp.dot(lf_r, triu, preferred_element_type=F32, precision=lax.Precision.HIGHEST)
    r_r = li_r - b_r
    btot = b_c[L - 1:L, :]
    a_c = btot - b_c + li_c
    m_prev = jnp.where(first, 0.0, m_ref[0:1, :])
    m_new = jnp.maximum(btot + m_prev, jnp.max(a_c, axis=0, keepdims=True))
    decay = jnp.exp(btot + m_prev - m_new)
    w_c = jnp.exp(a_c - m_new)
    inter_log = b_c + m_prev
    m_ref[0:1, :] = m_new

    ones_col = jnp.where(lax.broadcasted_iota(jnp.int32, (L, LANES), 1) == 0, 1.0, 0.0).astype(F32)

    for h in range(HEADS):
        t, e = divmod(h, HEADS_PER_TILE)
        if e == 0:
            between(t + 1)
        es = slice(e * HEAD_DIM, (e + 1) * HEAD_DIM)
        sl = slice(h * HEAD_DIM, (h + 1) * HEAD_DIM)
        qh = qk[t][:, es].astype(BF16)
        kh = (qk[t][:, PAIR + e * HEAD_DIM:PAIR + (e + 1) * HEAD_DIM] * (HEAD_DIM ** -0.5)).astype(BF16)
        vh = proj_ref[t, :, 2 * PAIR + e * HEAD_DIM:2 * PAIR + (e + 1) * HEAD_DIM]
        oh = proj_ref[t, :, 3 * PAIR + e * HEAD_DIM:3 * PAIR + (e + 1) * HEAD_DIM]
        vaug = jnp.concatenate([vh, ones_col], axis=1)
        dlog = jnp.where(causal, b_c[:, h:h + 1] + r_r[h:h + 1, :], -jnp.inf)
        inter = inter_log[:, h:h + 1]
        m_t = jnp.maximum(inter, jnp.max(dlog, axis=1, keepdims=True))
        dw = jnp.exp(dlog - m_t)
        inter_w = jnp.exp(inter - m_t)
        s = lax.dot_general(qh, kh, (((1,), (1,)), ((), ())), preferred_element_type=F32) * dw
        ct = jnp.where(first, 0.0, ct_ref[h])
        tot = (inter_w * jnp.dot(qh, ct.astype(BF16), preferred_element_type=F32)
               + jnp.dot(s.astype(BF16), vaug.astype(BF16), preferred_element_type=F32))
        num = tot[:, :HEAD_DIM]
        den = tot[:, HEAD_DIM:HEAD_DIM + 1]
        cell = num / jnp.maximum(jnp.abs(den), jnp.exp(-m_t))
        cell = _rms(cell, gn_ref[:, sl])
        out_ref[:, sl] = _sigmoid(oh) * cell
        wv = (w_c[:, h:h + 1] * vaug).astype(BF16)
        upd = lax.dot_general(kh, wv, (((0,), (0,)), ((), ())), preferred_element_type=F32)
        ct_ref[h] = decay[:, h:h + 1] * ct + upd


def _proj_mlstm_kernel(x_ref, g_ref, wq_ref, wk_ref, wv_ref, wo_ref, wu_ref, wgc_ref, wgr_ref,
                       bc_ref, br_ref, wqk_ref, gn_ref,
                       u_ref, hml_ref, xn_ref, proj_a, proj_b, gcol_a, gcol_b, grow_a, grow_b,
                       ct_ref, m_ref, xe_ref, *, tiles_per_seq):
    i = pl.program_id(0)
    j = pl.program_id(1)
    w_refs = (wq_ref, wk_ref, wv_ref, wo_ref, wu_ref)

    @pl.when((i == 0) & (j == 0))
    def _():
        proj_b[...] = jnp.zeros_like(proj_b)
        gcol_b[...] = jnp.zeros_like(gcol_b)
        grow_b[...] = jnp.zeros_like(grow_b)
        ct_ref[...] = jnp.zeros_like(ct_ref)
        m_ref[...] = jnp.zeros_like(m_ref)
        xe_ref[...] = jnp.zeros_like(xe_ref)

    def step(proj_w, gcol_w, grow_w, proj_r, gcol_r, grow_r):
        @pl.when(j == 0)
        def _():
            xn = _rms(x_ref[...], g_ref[...]).astype(BF16)
            xn_ref[...] = xn
            gcol_w[...] = jnp.dot(xn, wgc_ref[...], preferred_element_type=F32)
            gr = lax.dot_general(wgr_ref[...], xn, (((1,), (1,)), ((), ())),
                                 preferred_element_type=F32)
            for c in range(COL_TILES):
                grow_w[c] = gr[:, c * CHUNK:(c + 1) * CHUNK]

        def project_piece(k):
            cs = slice(k * PAIR, (k + 1) * PAIR)
            y = lax.dot_general(xn_ref[...], w_refs[k][...], (((1,), (1,)), ((), ())),
                                preferred_element_type=F32)
            proj_w[j, :, cs] = y
            if k == 4:
                u_ref[...] = y

        first = ((i + tiles_per_seq - 1) % tiles_per_seq == 0) & (j == 0)
        r0 = pl.multiple_of(j * CHUNK, CHUNK)
        _mlstm_chunk(first, proj_r.at[:, pl.ds(r0, CHUNK), :], gcol_r[pl.ds(r0, CHUNK), :],
                     grow_r[j], bc_ref, br_ref, wqk_ref, gn_ref, hml_ref, ct_ref, m_ref, xe_ref,
                     project_piece)

    @pl.when(i % 2 == 0)
    def _():
        step(proj_a, gcol_a, grow_a, proj_b, gcol_b, grow_b)

    @pl.when(i % 2 == 1)
    def _():
        step(proj_b, gcol_b, grow_b, proj_a, gcol_a, grow_a)


def _proj_mlstm(x2, g, w_t, w_ut, w_gcol, w_grow, b_col, b_row, wqk, gn, seq):
    m, d = x2.shape
    nt = m // ROW_TILE
    last = nt - 1
    const = lambda shape: pl.BlockSpec(shape, lambda i, j: (0,) * len(shape))
    slab = lambda s: pl.BlockSpec((PAIR, d), lambda i, j: (s * COL_TILES + j, 0))
    return pl.pallas_call(
        functools.partial(_proj_mlstm_kernel, tiles_per_seq=seq // ROW_TILE),
        grid=(nt + 1, COL_TILES),
        in_specs=[
            pl.BlockSpec((ROW_TILE, d), lambda i, j: (jnp.minimum(i, last), 0)),
            const((1, d)),
            slab(0), slab(1), slab(2), slab(3),
            pl.BlockSpec((POOL_GROUP_WIDTH, d), lambda i, j: (j, 0)),
            const((d, 2 * LANES)),
            const((2 * SUBLANES, d)),
            const((1, 2 * LANES)),
            const((2 * SUBLANES, 1)),
            const((QK_CONV, 2 * MLSTM_WIDTH)),
            const((1, MLSTM_WIDTH)),
        ],
        out_specs=[
            pl.BlockSpec((ROW_TILE, POOL_GROUP_WIDTH), lambda i, j: (i, j)),
            pl.BlockSpec((CHUNK, MLSTM_WIDTH),
                         lambda i, j: (jnp.maximum((i - 1) * COL_TILES + j, 0), 0)),
        ],
        out_shape=[
            jax.ShapeDtypeStruct((m + ROW_TILE, POOL_WIDTH), F32),
            jax.ShapeDtypeStruct((m, MLSTM_WIDTH), F32),
        ],
        scratch_shapes=[
            pltpu.VMEM((ROW_TILE, d), BF16),
            pltpu.VMEM((COL_TILES, ROW_TILE, TILE_COLS), F32),
            pltpu.VMEM((COL_TILES, ROW_TILE, TILE_COLS), F32),
            pltpu.VMEM((ROW_TILE, 2 * LANES), F32),
            pltpu.VMEM((ROW_TILE, 2 * LANES), F32),
            pltpu.VMEM((COL_TILES, 2 * SUBLANES, CHUNK), F32),
            pltpu.VMEM((COL_TILES, 2 * SUBLANES, CHUNK), F32),
            pltpu.VMEM((HEADS, HEAD_DIM, 2 * HEAD_DIM), F32),
            pltpu.VMEM((SUBLANES, LANES), F32),
            pltpu.VMEM((SUBLANES + CHUNK, 2 * MLSTM_WIDTH), F32),
        ],
        compiler_params=pltpu.CompilerParams(
            dimension_semantics=("arbitrary", "arbitrary"), vmem_limit_bytes=VMEM_LIMIT),
        name="proj_mlstm",
    )(x2, g, w_t, w_t, w_t, w_t, w_ut, w_gcol, w_grow, b_col, b_row, wqk, gn)


MIX_ROW_SPLIT = 2


def _mix_kernel(hml_ref, u_ref, x_ref, wpool_ref, pscale_ref, wo_ml_ref, wo_pool_ref, g_ref,
                gnext_ref, out_ref, hn_ref, ue_ref, *, tiles_per_seq):
    tm = u_ref.shape[0]
    t = pl.program_id(0) % tiles_per_seq

    @pl.when(t == 0)
    def _():
        ue_ref[0:POOL_HALO, :] = jnp.zeros((POOL_HALO, POOL_WIDTH), F32)

    u = u_ref[...]
    ue_ref[POOL_HALO:POOL_HALO + tm, :] = u
    pos = (lax.broadcasted_iota(jnp.int32, (tm, 1), 0) + (t * tm + 1)).astype(F32)
    mixed = []
    for g, window in enumerate(POOL_WINDOWS):
        sl = slice(g * POOL_GROUP_WIDTH, (g + 1) * POOL_GROUP_WIDTH)
        win = ue_ref[:, sl]
        span = 1
        while span < window:
            win = win + pltpu.roll(win, span, axis=0)
            span *= 2
        mean = win[POOL_HALO:, :] / jnp.minimum(pos, float(window))
        pooled = (mean - u[:, sl]).astype(BF16)
        mixed.append((jnp.dot(pooled, wpool_ref[g], preferred_element_type=F32)
                      * pscale_ref[:, sl]).astype(BF16))
    ue_ref[0:POOL_HALO, :] = u[tm - POOL_HALO:tm, :]
    hpool = jnp.concatenate(mixed, axis=1)
    rows = tm // MIX_ROW_SPLIT
    for r in range(MIX_ROW_SPLIT):
        rs = slice(r * rows, (r + 1) * rows)
        mix = (jnp.dot(hml_ref[rs, :].astype(BF16), wo_ml_ref[...], preferred_element_type=F32)
               + jnp.dot(hpool[rs, :], wo_pool_ref[...], preferred_element_type=F32))
        h = x_ref[rs, :] + _rms(mix, g_ref[...])
        out_ref[rs, :] = h
        hn_ref[rs, :] = _rms(h, gnext_ref[...]).astype(BF16)


def _mix(hml, u, x2, w_pool, pool_scale, wo_ml, wo_pool, g_post, g_next, seq, tm):
    m, d = x2.shape
    const = lambda shape: pl.BlockSpec(shape, lambda i: (0,) * len(shape),
                                       pipeline_mode=pl.Buffered(1))
    return pl.pallas_call(
        functools.partial(_mix_kernel, tiles_per_seq=seq // tm),
        grid=(m // tm,),
        in_specs=[
            pl.BlockSpec((tm, MLSTM_WIDTH), lambda i: (i, 0)),
            pl.BlockSpec((tm, POOL_WIDTH), lambda i: (i, 0)),
            pl.BlockSpec((tm, d), lambda i: (i, 0)),
            const(w_pool.shape),
            const((1, POOL_WIDTH)),
            const(wo_ml.shape),
            const(wo_pool.shape),
            const((1, d)),
            const((1, d)),
        ],
        out_specs=[pl.BlockSpec((tm, d), lambda i: (i, 0)), pl.BlockSpec((tm, d), lambda i: (i, 0))],
        out_shape=[jax.ShapeDtypeStruct((m, d), F32), jax.ShapeDtypeStruct((m, d), BF16)],
        scratch_shapes=[pltpu.VMEM((POOL_HALO + tm, POOL_WIDTH), F32)],
        compiler_params=pltpu.CompilerParams(
            dimension_semantics=("arbitrary",), vmem_limit_bytes=VMEM_LIMIT),
        name="mix",
    )(hml, u, x2, w_pool, pool_scale, wo_ml, wo_pool, g_post, g_next)


FFN_ROW_SPLIT = 2


def _ffn_kernel(hn_ref, w1_ref, w2_ref, acc_ref):
    j = pl.program_id(1)
    rows = hn_ref.shape[0] // FFN_ROW_SPLIT
    groups = [slice(r * rows, (r + 1) * rows) for r in range(FFN_ROW_SPLIT)]
    acts = []
    for g in groups:
        a = jnp.dot(hn_ref[g, :], w1_ref[...], preferred_element_type=F32)
        acts.append(jnp.square(jnp.maximum(a, 0.0)).astype(BF16))
    for g, a in zip(groups, acts):
        prev = jnp.where(j == 0, 0.0, acc_ref[g, :])
        acc_ref[g, :] = prev + jnp.dot(a, w2_ref[...], preferred_element_type=F32)


def _ffn(hn, w1, w2, tm, tf):
    m, d = hn.shape
    f = w1.shape[1]
    return pl.pallas_call(
        _ffn_kernel,
        grid=(m // tm, f // tf),
        in_specs=[
            pl.BlockSpec((tm, d), lambda i, j: (i, 0)),
            pl.BlockSpec((d, tf), lambda i, j: (0, j)),
            pl.BlockSpec((tf, d), lambda i, j: (j, 0)),
        ],
        out_specs=pl.BlockSpec((tm, d), lambda i, j: (i, 0)),
        out_shape=jax.ShapeDtypeStruct((m, d), F32),
        compiler_params=pltpu.CompilerParams(
            dimension_semantics=("arbitrary", "arbitrary"), vmem_limit_bytes=VMEM_LIMIT),
        name="ffn",
    )(hn, w1, w2)


def _ple_kernel(h_ref, ff_ref, p_ref, gff_ref, ggate_ref, wgate_ref, wproj_ref, gpost_ref, out_ref):
    rows = h_ref.shape[0] // MIX_ROW_SPLIT
    groups = [slice(r * rows, (r + 1) * rows) for r in range(MIX_ROW_SPLIT)]
    hns = []
    for rs in groups:
        h = h_ref[rs, :] + _rms(ff_ref[rs, :], gff_ref[...])
        out_ref[rs, :] = h
        hns.append(_rms(h, ggate_ref[...]).astype(BF16))
    for rs, hn in zip(groups, hns):
        gate = _sigmoid(jnp.dot(hn, wgate_ref[...], preferred_element_type=F32))
        e = jnp.dot(p_ref[rs, :].astype(BF16), wproj_ref[...], preferred_element_type=F32)
        out_ref[rs, :] += _rms(e * gate, gpost_ref[...])


def _ple(h, ff, p2, g_ff_post, g_gate, w_gate, w_proj, g_post, tm):
    m, d = h.shape
    pd = p2.shape[1]
    const = lambda shape: pl.BlockSpec(shape, lambda i: (0,) * len(shape),
                                       pipeline_mode=pl.Buffered(1))
    return pl.pallas_call(
        _ple_kernel,
        grid=(m // tm,),
        in_specs=[
            pl.BlockSpec((tm, d), lambda i: (i, 0)),
            pl.BlockSpec((tm, d), lambda i: (i, 0)),
            pl.BlockSpec((tm, pd), lambda i: (i, 0)),
            const((1, d)),
            const((1, d)),
            const(w_gate.shape),
            const(w_proj.shape),
            const((1, d)),
        ],
        out_specs=pl.BlockSpec((tm, d), lambda i: (i, 0)),
        out_shape=jax.ShapeDtypeStruct((m, d), F32),
        compiler_params=pltpu.CompilerParams(
            dimension_semantics=("arbitrary",), vmem_limit_bytes=VMEM_LIMIT),
        name="ple",
    )(h, ff, p2, g_ff_post, g_gate, w_gate, w_proj, g_post)


def _tile_qk_columns(w_q, w_k):
    r = w_q.shape[0]
    a = jnp.stack([w_q.reshape(r, COL_TILES, PAIR), w_k.reshape(r, COL_TILES, PAIR)], axis=2)
    return a.reshape(r, 2 * MLSTM_WIDTH)


def _layer(h, p2, seq, w_in, b_gates, w_qk_conv, g_mlstm, w_pool, pool_scale, w_out,
           g_mix_pre, g_mix_post, w_ff1, w_ff2, g_ff_pre, g_ff_post, w_ple_proj, w_ple_gate,
           g_ple_gate, g_ple_post):
    qkvo = 4 * MLSTM_WIDTH
    row = lambda v: v.reshape(1, -1).astype(F32)
    w_t = w_in.T.astype(BF16)
    w_ut = w_t[qkvo + 2 * HEADS:]
    w_gi = w_in[:, qkvo:qkvo + HEADS]
    w_gf = w_in[:, qkvo + HEADS:qkvo + 2 * HEADS]
    lane_pad = lambda w: jnp.pad(w, ((0, 0), (0, LANES - HEADS)))
    w_gcol = jnp.concatenate([lane_pad(w_gi), lane_pad(w_gf)], axis=1).astype(BF16)
    w_grow = jnp.concatenate([w_gi, w_gf], axis=1).T.astype(BF16)
    b_col = jnp.concatenate([lane_pad(b_gates[None, :HEADS]), lane_pad(b_gates[None, HEADS:])],
                            axis=1).astype(F32)
    b_row = b_gates.reshape(2 * HEADS, 1).astype(F32)
    wqk = _tile_qk_columns(w_qk_conv[:, :MLSTM_WIDTH], w_qk_conv[:, MLSTM_WIDTH:]).astype(F32)

    u, hml = _proj_mlstm(h, row(g_mix_pre), w_t, w_ut, w_gcol, w_grow, b_col, b_row, wqk,
                         row(g_mlstm), seq)
    w_out16 = w_out.astype(BF16)
    h, hn = _mix(hml, u, h, w_pool.astype(BF16), row(pool_scale), w_out16[:MLSTM_WIDTH],
                 w_out16[MLSTM_WIDTH:], row(g_mix_post), row(g_ff_pre), seq, tm=512)
    ff = _ffn(hn, w_ff1.astype(BF16), w_ff2.astype(BF16), tm=1024, tf=512)
    return _ple(h, ff, p2, row(g_ff_post), row(g_ple_gate), w_ple_gate.astype(BF16),
                w_ple_proj.astype(BF16), row(g_ple_post), tm=512)


def kernel(x, p, w_in, b_gates, w_qk_conv, g_mlstm, w_pool, pool_scale, w_out, g_mix_pre,
           g_mix_post, w_ff1, w_ff2, g_ff_pre, g_ff_post, w_ple_proj, w_ple_gate, g_ple_gate,
           g_ple_post):
    batch, seq, d = x.shape
    h = x.reshape(batch * seq, d)
    for i in range(p.shape[0]):
        h = _layer(h, p[i].reshape(batch * seq, -1), seq, w_in[i], b_gates[i],
                   w_qk_conv[i], g_mlstm[i], w_pool[i], pool_scale[i], w_out[i], g_mix_pre[i],
                   g_mix_post[i], w_ff1[i], w_ff2[i], g_ff_pre[i], g_ff_post[i], w_ple_proj[i],
                   w_ple_gate[i], g_ple_gate[i], g_ple_post[i])
    return h.reshape(batch, seq, d)
```

```python
import functools

import jax
import jax.numpy as jnp
from jax import lax
from jax.experimental import pallas as pl
from jax.experimental.pallas import tpu as pltpu

EPS = 1e-6
HEADS = 8
HEAD_DIM = 128
MLSTM_WIDTH = HEADS * HEAD_DIM
QK_CONV = 4
POOL_WINDOWS = (2, 4, 8, 16)
POOL_GROUP_WIDTH = 256
POOL_WIDTH = POOL_GROUP_WIDTH * len(POOL_WINDOWS)

LANES = 128
SUBLANES = 8
POOL_HALO = 16
CHUNK = 128
VMEM_LIMIT = 56 * 1024 * 1024

COL_TILES = 4
HEADS_PER_TILE = HEADS // COL_TILES
PAIR = HEADS_PER_TILE * HEAD_DIM
TILE_COLS = 4 * PAIR + POOL_GROUP_WIDTH
ROW_TILE = COL_TILES * CHUNK

F32 = jnp.float32
BF16 = jnp.bfloat16


def _rms(x, g):
    return x * lax.rsqrt(jnp.mean(x * x, axis=-1, keepdims=True) + EPS) * g


def _log_sigmoid(x):
    return jnp.minimum(x, 0.0) - jnp.log1p(jnp.exp(-jnp.abs(x)))


def _sigmoid(x):
    return 1.0 / (1.0 + jnp.exp(-x))


def _mlstm_chunk(first, proj_ref, gc, gr, bc_ref, br_ref, wqk_ref, gn_ref, out_ref, ct_ref, m_ref,
                 xe_ref, between):
    L = CHUNK
    between(0)

    xe_ref[0:SUBLANES, :] = jnp.where(first, 0.0, xe_ref[0:SUBLANES, :])
    qk = []
    for t in range(COL_TILES):
        cs = slice(t * 2 * PAIR, (t + 1) * 2 * PAIR)
        x = proj_ref[t, :, 0:2 * PAIR]
        xe_ref[SUBLANES:SUBLANES + L, cs] = x
        acc = x * wqk_ref[QK_CONV - 1:QK_CONV, cs]
        for back in range(1, QK_CONV):
            tap = QK_CONV - 1 - back
            acc = acc + xe_ref[SUBLANES - back:SUBLANES - back + L, cs] * wqk_ref[tap:tap + 1, cs]
        xe_ref[0:SUBLANES, cs] = x[L - SUBLANES:L, :]
        qk.append(acc * _sigmoid(acc))

    gc = gc + bc_ref[...]
    li_c = gc[:, :LANES]
    lf_c = _log_sigmoid(gc[:, LANES:])
    gr = gr + br_ref[...]
    li_r = gr[0:SUBLANES, :]
    lf_r = _log_sigmoid(gr[SUBLANES:, :])
    row = lax.broadcasted_iota(jnp.int32, (L, L), 0)
    col = lax.broadcasted_iota(jnp.int32, (L, L), 1)
    causal = row >= col
    tril = jnp.where(causal, 1.0, 0.0).astype(F32)
    triu = jnp.where(row <= col, 1.0, 0.0).astype(F32)
    b_c = jnp.dot(tril, lf_c, preferred_element_type=F32, precision=lax.Precision.HIGHEST)
    b_r = jnp.dot(lf_r, triu, preferred_element_type=F32, precision=lax.Precision.HIGHEST)
    r_r = li_r - b_r
    btot = b_c[L - 1:L, :]
    a_c = btot - b_c + li_c
    m_prev = jnp.where(first, 0.0, m_ref[0:1, :])
    m_new = jnp.maximum(btot + m_prev, jnp.max(a_c, axis=0, keepdims=True))
    decay = jnp.exp(btot + m_prev - m_new)
    w_c = jnp.exp(a_c - m_new)
    inter_log = b_c + m_prev
    m_ref[0:1, :] = m_new

    ones_col = jnp.where(lax.broadcasted_iota(jnp.int32, (L, LANES), 1) == 0, 1.0, 0.0).astype(F32)

    for h in range(HEADS):
        t, e = divmod(h, HEADS_PER_TILE)
        if e == 0:
            between(t + 1)
        es = slice(e * HEAD_DIM, (e + 1) * HEAD_DIM)
        sl = slice(h * HEAD_DIM, (h + 1) * HEAD_DIM)
        qf = qk[t][:, es]
        qh = qf.astype(BF16)
        kh = (qk[t][:, PAIR + e * HEAD_DIM:PAIR + (e + 1) * HEAD_DIM] * (HEAD_DIM ** -0.5)).astype(BF16)
        vh = proj_ref[t, :, 2 * PAIR + e * HEAD_DIM:2 * PAIR + (e + 1) * HEAD_DIM]
        oh = proj_ref[t, :, 3 * PAIR + e * HEAD_DIM:3 * PAIR + (e + 1) * HEAD_DIM]
        vaug = jnp.concatenate([vh, ones_col], axis=1)
        dlog = jnp.where(causal, b_c[:, h:h + 1] + r_r[h:h + 1, :], -jnp.inf)
        inter = inter_log[:, h:h + 1]
        m_t = jnp.maximum(inter, jnp.max(dlog, axis=1, keepdims=True))
        dw = jnp.exp(dlog - m_t)
        inter_w = jnp.exp(inter - m_t)
        s = lax.dot_general(qh, kh, (((1,), (1,)), ((), ())), preferred_element_type=F32) * dw
        ct = jnp.where(first, 0.0, ct_ref[h])
        lhs = jnp.concatenate([s.astype(BF16), (inter_w * qf).astype(BF16)], axis=1)
        rhs = jnp.concatenate([vaug.astype(BF16), ct.astype(BF16)], axis=0)
        tot = jnp.dot(lhs, rhs, preferred_element_type=F32)
        num = tot[:, :HEAD_DIM]
        den = tot[:, HEAD_DIM:HEAD_DIM + 1]
        cell = num / jnp.maximum(jnp.abs(den), jnp.exp(-m_t))
        cell = _rms(cell, gn_ref[:, sl])
        out_ref[:, sl] = _sigmoid(oh) * cell
        wv = (w_c[:, h:h + 1] * vaug).astype(BF16)
        upd = lax.dot_general(kh, wv, (((0,), (0,)), ((), ())), preferred_element_type=F32)
        ct_ref[h] = decay[:, h:h + 1] * ct + upd


def _proj_mlstm_kernel(x_ref, g_ref, wq_ref, wk_ref, wv_ref, wo_ref, wu_ref, wgc_ref, wgr_ref,
                       bc_ref, br_ref, wqk_ref, gn_ref,
                       u_ref, hml_ref, xn_ref, proj_a, proj_b, gcol_a, gcol_b, grow_a, grow_b,
                       ct_ref, m_ref, xe_ref, *, tiles_per_seq):
    i = pl.program_id(0)
    j = pl.program_id(1)
    w_refs = (wq_ref, wk_ref, wv_ref, wo_ref, wu_ref)

    @pl.when((i == 0) & (j == 0))
    def _():
        proj_b[...] = jnp.zeros_like(proj_b)
        gcol_b[...] = jnp.zeros_like(gcol_b)
        grow_b[...] = jnp.zeros_like(grow_b)
        ct_ref[...] = jnp.zeros_like(ct_ref)
        m_ref[...] = jnp.zeros_like(m_ref)
        xe_ref[...] = jnp.zeros_like(xe_ref)

    def step(proj_w, gcol_w, grow_w, proj_r, gcol_r, grow_r):
        @pl.when(j == 0)
        def _():
            xn = _rms(x_ref[...], g_ref[...]).astype(BF16)
            xn_ref[...] = xn
            gcol_w[...] = jnp.dot(xn, wgc_ref[...], preferred_element_type=F32)
            gr = lax.dot_general(wgr_ref[...], xn, (((1,), (1,)), ((), ())),
                                 preferred_element_type=F32)
            for c in range(COL_TILES):
                grow_w[c] = gr[:, c * CHUNK:(c + 1) * CHUNK]

        def project_piece(k):
            cs = slice(k * PAIR, (k + 1) * PAIR)
            y = lax.dot_general(xn_ref[...], w_refs[k][...], (((1,), (1,)), ((), ())),
                                preferred_element_type=F32)
            proj_w[j, :, cs] = y
            if k == 4:
                u_ref[...] = y

        first = ((i + tiles_per_seq - 1) % tiles_per_seq == 0) & (j == 0)
        r0 = pl.multiple_of(j * CHUNK, CHUNK)
        _mlstm_chunk(first, proj_r.at[:, pl.ds(r0, CHUNK), :], gcol_r[pl.ds(r0, CHUNK), :],
                     grow_r[j], bc_ref, br_ref, wqk_ref, gn_ref, hml_ref, ct_ref, m_ref, xe_ref,
                     project_piece)

    @pl.when(i % 2 == 0)
    def _():
        step(proj_a, gcol_a, grow_a, proj_b, gcol_b, grow_b)

    @pl.when(i % 2 == 1)
    def _():
        step(proj_b, gcol_b, grow_b, proj_a, gcol_a, grow_a)


def _proj_mlstm(x2, g, w_t, w_ut, w_gcol, w_grow, b_col, b_row, wqk, gn, seq):
    m, d = x2.shape
    nt = m // ROW_TILE
    last = nt - 1
    const = lambda shape: pl.BlockSpec(shape, lambda i, j: (0,) * len(shape))
    slab = lambda s: pl.BlockSpec((PAIR, d), lambda i, j: (s * COL_TILES + j, 0))
    return pl.pallas_call(
        functools.partial(_proj_mlstm_kernel, tiles_per_seq=seq // ROW_TILE),
        grid=(nt + 1, COL_TILES),
        in_specs=[
            pl.BlockSpec((ROW_TILE, d), lambda i, j: (jnp.minimum(i, last), 0)),
            const((1, d)),
            slab(0), slab(1), slab(2), slab(3),
            pl.BlockSpec((POOL_GROUP_WIDTH, d), lambda i, j: (j, 0)),
            const((d, 2 * LANES)),
            const((2 * SUBLANES, d)),
            const((1, 2 * LANES)),
            const((2 * SUBLANES, 1)),
            const((QK_CONV, 2 * MLSTM_WIDTH)),
            const((1, MLSTM_WIDTH)),
        ],
        out_specs=[
            pl.BlockSpec((ROW_TILE, POOL_GROUP_WIDTH), lambda i, j: (i, j)),
            pl.BlockSpec((CHUNK, MLSTM_WIDTH),
                         lambda i, j: (jnp.maximum((i - 1) * COL_TILES + j, 0), 0)),
        ],
        out_shape=[
            jax.ShapeDtypeStruct((m + ROW_TILE, POOL_WIDTH), F32),
            jax.ShapeDtypeStruct((m, MLSTM_WIDTH), F32),
        ],
        scratch_shapes=[
            pltpu.VMEM((ROW_TILE, d), BF16),
            pltpu.VMEM((COL_TILES, ROW_TILE, TILE_COLS), F32),
            pltpu.VMEM((COL_TILES, ROW_TILE, TILE_COLS), F32),
            pltpu.VMEM((ROW_TILE, 2 * LANES), F32),
            pltpu.VMEM((ROW_TILE, 2 * LANES), F32),
            pltpu.VMEM((COL_TILES, 2 * SUBLANES, CHUNK), F32),
            pltpu.VMEM((COL_TILES, 2 * SUBLANES, CHUNK), F32),
            pltpu.VMEM((HEADS, HEAD_DIM, 2 * HEAD_DIM), F32),
            pltpu.VMEM((SUBLANES, LANES), F32),
            pltpu.VMEM((SUBLANES + CHUNK, 2 * MLSTM_WIDTH), F32),
        ],
        compiler_params=pltpu.CompilerParams(
            dimension_semantics=("arbitrary", "arbitrary"), vmem_limit_bytes=VMEM_LIMIT),
        name="proj_mlstm",
    )(x2, g, w_t, w_t, w_t, w_t, w_ut, w_gcol, w_grow, b_col, b_row, wqk, gn)


MIX_ROW_SPLIT = 2


def _mix_kernel(hml_ref, u_ref, x_ref, wpool_ref, pscale_ref, wo_ml_ref, wo_pool_ref, g_ref,
                gnext_ref, out_ref, hn_ref, ue_ref, *, tiles_per_seq):
    tm = u_ref.shape[0]
    t = pl.program_id(0) % tiles_per_seq

    @pl.when(t == 0)
    def _():
        ue_ref[0:POOL_HALO, :] = jnp.zeros((POOL_HALO, POOL_WIDTH), F32)

    u = u_ref[...]
    ue_ref[POOL_HALO:POOL_HALO + tm, :] = u
    pos = (lax.broadcasted_iota(jnp.int32, (tm, 1), 0) + (t * tm + 1)).astype(F32)
    mixed = []
    for g, window in enumerate(POOL_WINDOWS):
        sl = slice(g * POOL_GROUP_WIDTH, (g + 1) * POOL_GROUP_WIDTH)
        win = ue_ref[:, sl]
        span = 1
        while span < window:
            win = win + pltpu.roll(win, span, axis=0)
            span *= 2
        mean = win[POOL_HALO:, :] / jnp.minimum(pos, float(window))
        pooled = (mean - u[:, sl]).astype(BF16)
        mixed.append((jnp.dot(pooled, wpool_ref[g], preferred_element_type=F32)
                      * pscale_ref[:, sl]).astype(BF16))
    ue_ref[0:POOL_HALO, :] = u[tm - POOL_HALO:tm, :]
    hpool = jnp.concatenate(mixed, axis=1)
    rows = tm // MIX_ROW_SPLIT
    for r in range(MIX_ROW_SPLIT):
        rs = slice(r * rows, (r + 1) * rows)
        mix = (jnp.dot(hml_ref[rs, :].astype(BF16), wo_ml_ref[...], preferred_element_type=F32)
               + jnp.dot(hpool[rs, :], wo_pool_ref[...], preferred_element_type=F32))
        h = x_ref[rs, :] + _rms(mix, g_ref[...])
        out_ref[rs, :] = h
        hn_ref[rs, :] = _rms(h, gnext_ref[...]).astype(BF16)


def _mix(hml, u, x2, w_pool, pool_scale, wo_ml, wo_pool, g_post, g_next, seq, tm):
    m, d = x2.shape
    const = lambda shape: pl.BlockSpec(shape, lambda i: (0,) * len(shape),
                                       pipeline_mode=pl.Buffered(1))
    return pl.pallas_call(
        functools.partial(_mix_kernel, tiles_per_seq=seq // tm),
        grid=(m // tm,),
        in_specs=[
            pl.BlockSpec((tm, MLSTM_WIDTH), lambda i: (i, 0)),
            pl.BlockSpec((tm, POOL_WIDTH), lambda i: (i, 0)),
            pl.BlockSpec((tm, d), lambda i: (i, 0)),
            const(w_pool.shape),
            const((1, POOL_WIDTH)),
            const(wo_ml.shape),
            const(wo_pool.shape),
            const((1, d)),
            const((1, d)),
        ],
        out_specs=[pl.BlockSpec((tm, d), lambda i: (i, 0)), pl.BlockSpec((tm, d), lambda i: (i, 0))],
        out_shape=[jax.ShapeDtypeStruct((m, d), F32), jax.ShapeDtypeStruct((m, d), BF16)],
        scratch_shapes=[pltpu.VMEM((POOL_HALO + tm, POOL_WIDTH), F32)],
        compiler_params=pltpu.CompilerParams(
            dimension_semantics=("arbitrary",), vmem_limit_bytes=VMEM_LIMIT),
        name="mix",
    )(hml, u, x2, w_pool, pool_scale, wo_ml, wo_pool, g_post, g_next)


FFN_ROW_SPLIT = 2


def _ffn_kernel(hn_ref, w1_ref, w2_ref, acc_ref):
    j = pl.program_id(1)
    rows = hn_ref.shape[0] // FFN_ROW_SPLIT
    groups = [slice(r * rows, (r + 1) * rows) for r in range(FFN_ROW_SPLIT)]
    w1 = w1_ref[...].astype(BF16)
    w2 = w2_ref[...].astype(BF16)
    acts = []
    for g in groups:
        a = jnp.dot(hn_ref[g, :], w1, preferred_element_type=F32)
        acts.append(jnp.square(jnp.maximum(a, 0.0)).astype(BF16))
    for g, a in zip(groups, acts):
        prev = jnp.where(j == 0, 0.0, acc_ref[g, :])
        acc_ref[g, :] = prev + jnp.dot(a, w2, preferred_element_type=F32)


def _ffn(hn, w1, w2, tm, tf):
    m, d = hn.shape
    f = w1.shape[1]
    return pl.pallas_call(
        _ffn_kernel,
        grid=(m // tm, f // tf),
        in_specs=[
            pl.BlockSpec((tm, d), lambda i, j: (i, 0)),
            pl.BlockSpec((d, tf), lambda i, j: (0, j)),
            pl.BlockSpec((tf, d), lambda i, j: (j, 0)),
        ],
        out_specs=pl.BlockSpec((tm, d), lambda i, j: (i, 0)),
        out_shape=jax.ShapeDtypeStruct((m, d), F32),
        compiler_params=pltpu.CompilerParams(
            dimension_semantics=("arbitrary", "arbitrary"), vmem_limit_bytes=VMEM_LIMIT),
        name="ffn",
    )(hn, w1, w2)


def _ple_kernel(h_ref, ff_ref, p_ref, gff_ref, ggate_ref, wgate_ref, wproj_ref, gpost_ref, out_ref):
    rows = h_ref.shape[0] // MIX_ROW_SPLIT
    groups = [slice(r * rows, (r + 1) * rows) for r in range(MIX_ROW_SPLIT)]
    hns = []
    for rs in groups:
        h = h_ref[rs, :] + _rms(ff_ref[rs, :], gff_ref[...])
        out_ref[rs, :] = h
        hns.append(_rms(h, ggate_ref[...]).astype(BF16))
    for rs, hn in zip(groups, hns):
        gate = _sigmoid(jnp.dot(hn, wgate_ref[...], preferred_element_type=F32))
        e = jnp.dot(p_ref[rs, :].astype(BF16), wproj_ref[...], preferred_element_type=F32)
        out_ref[rs, :] += _rms(e * gate, gpost_ref[...])


def _ple(h, ff, p2, g_ff_post, g_gate, w_gate, w_proj, g_post, tm):
    m, d = h.shape
    pd = p2.shape[1]
    const = lambda shape: pl.BlockSpec(shape, lambda i: (0,) * len(shape),
                                       pipeline_mode=pl.Buffered(1))
    return pl.pallas_call(
        _ple_kernel,
        grid=(m // tm,),
        in_specs=[
            pl.BlockSpec((tm, d), lambda i: (i, 0)),
            pl.BlockSpec((tm, d), lambda i: (i, 0)),
            pl.BlockSpec((tm, pd), lambda i: (i, 0)),
            const((1, d)),
            const((1, d)),
            const(w_gate.shape),
            const(w_proj.shape),
            const((1, d)),
        ],
        out_specs=pl.BlockSpec((tm, d), lambda i: (i, 0)),
        out_shape=jax.ShapeDtypeStruct((m, d), F32),
        compiler_params=pltpu.CompilerParams(
            dimension_semantics=("arbitrary",), vmem_limit_bytes=VMEM_LIMIT),
        name="ple",
    )(h, ff, p2, g_ff_post, g_gate, w_gate, w_proj, g_post)


def _tile_qk_columns(w_q, w_k):
    r = w_q.shape[0]
    a = jnp.stack([w_q.reshape(r, COL_TILES, PAIR), w_k.reshape(r, COL_TILES, PAIR)], axis=2)
    return a.reshape(r, 2 * MLSTM_WIDTH)


def _layer(h, p2, seq, w_in, b_gates, w_qk_conv, g_mlstm, w_pool, pool_scale, w_out,
           g_mix_pre, g_mix_post, w_ff1, w_ff2, g_ff_pre, g_ff_post, w_ple_proj, w_ple_gate,
           g_ple_gate, g_ple_post):
    qkvo = 4 * MLSTM_WIDTH
    row = lambda v: v.reshape(1, -1).astype(F32)
    w_t = w_in.T.astype(BF16)
    w_ut = w_t[qkvo + 2 * HEADS:]
    w_gi = w_in[:, qkvo:qkvo + HEADS]
    w_gf = w_in[:, qkvo + HEADS:qkvo + 2 * HEADS]
    lane_pad = lambda w: jnp.pad(w, ((0, 0), (0, LANES - HEADS)))
    w_gcol = jnp.concatenate([lane_pad(w_gi), lane_pad(w_gf)], axis=1).astype(BF16)
    w_grow = jnp.concatenate([w_gi, w_gf], axis=1).T.astype(BF16)
    b_col = jnp.concatenate([lane_pad(b_gates[None, :HEADS]), lane_pad(b_gates[None, HEADS:])],
                            axis=1).astype(F32)
    b_row = b_gates.reshape(2 * HEADS, 1).astype(F32)
    wqk = _tile_qk_columns(w_qk_conv[:, :MLSTM_WIDTH], w_qk_conv[:, MLSTM_WIDTH:]).astype(F32)

    u, hml = _proj_mlstm(h, row(g_mix_pre), w_t, w_ut, w_gcol, w_grow, b_col, b_row, wqk,
                         row(g_mlstm), seq)
    w_out16 = w_out.astype(BF16)
    h, hn = _mix(hml, u, h, w_pool.astype(BF16), row(pool_scale), w_out16[:MLSTM_WIDTH],
                 w_out16[MLSTM_WIDTH:], row(g_mix_post), row(g_ff_pre), seq, tm=512)
    ff = _ffn(hn, w_ff1, w_ff2, tm=1024, tf=512)
    return _ple(h, ff, p2, row(g_ff_post), row(g_ple_gate), w_ple_gate.astype(BF16),
                w_ple_proj.astype(BF16), row(g_ple_post), tm=512)


def kernel(x, p, w_in, b_gates, w_qk_conv, g_mlstm, w_pool, pool_scale, w_out, g_mix_pre,
           g_mix_post, w_ff1, w_ff2, g_ff_pre, g_ff_post, w_ple_proj, w_ple_gate, g_ple_gate,
           g_ple_post):
    batch, seq, d = x.shape
    h = x.reshape(batch * seq, d)
    for i in range(p.shape[0]):
        h = _layer(h, p[i].reshape(batch * seq, -1), seq, w_in[i], b_gates[i],
                   w_qk_conv[i], g_mlstm[i], w_pool[i], pool_scale[i], w_out[i], g_mix_pre[i],
                   g_mix_post[i], w_ff1[i], w_ff2[i], g_ff_pre[i], g_ff_post[i], w_ple_proj[i],
                   w_ple_gate[i], g_ple_gate[i], g_ple_post[i])
    return h.reshape(batch, seq, d)
```

```python
import functools

import jax
import jax.numpy as jnp
from jax import lax
from jax.experimental import pallas as pl
from jax.experimental.pallas import tpu as pltpu

EPS = 1e-6
HEADS = 8
HEAD_DIM = 128
MLSTM_WIDTH = HEADS * HEAD_DIM
QK_CONV = 4
POOL_WINDOWS = (2, 4, 8, 16)
POOL_GROUP_WIDTH = 256
POOL_WIDTH = POOL_GROUP_WIDTH * len(POOL_WINDOWS)

LANES = 128
SUBLANES = 8
POOL_HALO = 16
CHUNK = 128
VMEM_LIMIT = 56 * 1024 * 1024

COL_TILES = 4
HEADS_PER_TILE = HEADS // COL_TILES
PAIR = HEADS_PER_TILE * HEAD_DIM
TILE_COLS = 4 * PAIR + POOL_GROUP_WIDTH
ROW_TILE = COL_TILES * CHUNK

F32 = jnp.float32
BF16 = jnp.bfloat16


def _rms(x, g):
    return x * lax.rsqrt(jnp.mean(x * x, axis=-1, keepdims=True) + EPS) * g


def _log_sigmoid(x):
    return jnp.minimum(x, 0.0) - jnp.log1p(jnp.exp(-jnp.abs(x)))


def _sigmoid(x):
    return 1.0 / (1.0 + jnp.exp(-x))


def _mlstm_chunk(first, proj_ref, gc, gr, bc_ref, br_ref, wqk_ref, gn_ref, out_ref, ct_ref, m_ref,
                 xe_ref, between):
    L = CHUNK
    between(0)

    xe_ref[0:SUBLANES, :] = jnp.where(first, 0.0, xe_ref[0:SUBLANES, :])
    qk = []
    for t in range(COL_TILES):
        cs = slice(t * 2 * PAIR, (t + 1) * 2 * PAIR)
        x = proj_ref[t, :, 0:2 * PAIR]
        xe_ref[SUBLANES:SUBLANES + L, cs] = x
        acc = x * wqk_ref[QK_CONV - 1:QK_CONV, cs]
        for back in range(1, QK_CONV):
            tap = QK_CONV - 1 - back
            acc = acc + xe_ref[SUBLANES - back:SUBLANES - back + L, cs] * wqk_ref[tap:tap + 1, cs]
        xe_ref[0:SUBLANES, cs] = x[L - SUBLANES:L, :]
        qk.append(acc * _sigmoid(acc))

    gc = gc + bc_ref[...]
    li_c = gc[:, :LANES]
    lf_c = _log_sigmoid(gc[:, LANES:])
    gr = gr + br_ref[...]
    li_r = gr[0:SUBLANES, :]
    lf_r = _log_sigmoid(gr[SUBLANES:, :])
    row = lax.broadcasted_iota(jnp.int32, (L, L), 0)
    col = lax.broadcasted_iota(jnp.int32, (L, L), 1)
    causal = row >= col
    tril = jnp.where(causal, 1.0, 0.0).astype(F32)
    triu = jnp.where(row <= col, 1.0, 0.0).astype(F32)
    b_c = jnp.dot(tril, lf_c, preferred_element_type=F32, precision=lax.Precision.HIGHEST)
    b_r = jnp.dot(lf_r, triu, preferred_element_type=F32, precision=lax.Precision.HIGHEST)
    r_r = li_r - b_r
    btot = b_c[L - 1:L, :]
    a_c = btot - b_c + li_c
    m_prev = jnp.where(first, 0.0, m_ref[0:1, :])
    m_new = jnp.maximum(btot + m_prev, jnp.max(a_c, axis=0, keepdims=True))
    decay = jnp.exp(btot + m_prev - m_new)
    w_c = jnp.exp(a_c - m_new)
    inter_log = b_c + m_prev
    m_ref[0:1, :] = m_new

    ones_col = jnp.where(lax.broadcasted_iota(jnp.int32, (L, LANES), 1) == 0, 1.0, 0.0).astype(F32)

    for h in range(HEADS):
        t, e = divmod(h, HEADS_PER_TILE)
        if e == 0:
            between(t + 1)
        es = slice(e * HEAD_DIM, (e + 1) * HEAD_DIM)
        sl = slice(h * HEAD_DIM, (h + 1) * HEAD_DIM)
        qf = qk[t][:, es]
        qh = qf.astype(BF16)
        kh = (qk[t][:, PAIR + e * HEAD_DIM:PAIR + (e + 1) * HEAD_DIM] * (HEAD_DIM ** -0.5)).astype(BF16)
        vh = proj_ref[t, :, 2 * PAIR + e * HEAD_DIM:2 * PAIR + (e + 1) * HEAD_DIM]
        oh = proj_ref[t, :, 3 * PAIR + e * HEAD_DIM:3 * PAIR + (e + 1) * HEAD_DIM]
        vaug = jnp.concatenate([vh, ones_col], axis=1)
        dlog = jnp.where(causal, b_c[:, h:h + 1] + r_r[h:h + 1, :], -jnp.inf)
        inter = inter_log[:, h:h + 1]
        m_t = jnp.maximum(inter, jnp.max(dlog, axis=1, keepdims=True))
        dw = jnp.exp(dlog - m_t)
        inter_w = jnp.exp(inter - m_t)
        s = lax.dot_general(qh, kh, (((1,), (1,)), ((), ())), preferred_element_type=F32) * dw
        ct = jnp.where(first, 0.0, ct_ref[h])
        lhs = jnp.concatenate([s.astype(BF16), (inter_w * qf).astype(BF16)], axis=1)
        rhs = jnp.concatenate([vaug.astype(BF16), ct.astype(BF16)], axis=0)
        tot = jnp.dot(lhs, rhs, preferred_element_type=F32)
        num = tot[:, :HEAD_DIM]
        den = tot[:, HEAD_DIM:HEAD_DIM + 1]
        cell = num / jnp.maximum(jnp.abs(den), jnp.exp(-m_t))
        cell = _rms(cell, gn_ref[:, sl])
        out_ref[:, sl] = _sigmoid(oh) * cell
        wv = (w_c[:, h:h + 1] * vaug).astype(BF16)
        upd = lax.dot_general(kh, wv, (((0,), (0,)), ((), ())), preferred_element_type=F32)
        ct_ref[h] = decay[:, h:h + 1] * ct + upd


def _proj_mlstm_kernel(x0_ref, xnext_ref, g_ref, wq_ref, wk_ref, wv_ref, wo_ref, wu_ref, wgc_ref,
                       wgr_ref, bc_ref, br_ref, wqk_ref, gn_ref,
                       u_ref, hml_ref, xn_a, xn_b, proj_a, proj_b, gcol_a, gcol_b, grow_a, grow_b,
                       ct_ref, m_ref, xe_ref, *, tiles_per_seq):
    i = pl.program_id(0)
    j = pl.program_id(1)
    w_refs = (wq_ref, wk_ref, wv_ref, wo_ref, wu_ref)

    @pl.when((i == 0) & (j == 0))
    def _():
        xn_a[...] = _rms(x0_ref[...], g_ref[...]).astype(BF16)
        proj_b[...] = jnp.zeros_like(proj_b)
        gcol_b[...] = jnp.zeros_like(gcol_b)
        grow_b[...] = jnp.zeros_like(grow_b)
        ct_ref[...] = jnp.zeros_like(ct_ref)
        m_ref[...] = jnp.zeros_like(m_ref)
        xe_ref[...] = jnp.zeros_like(xe_ref)

    def step(xn_cur, xn_nxt, proj_w, gcol_w, grow_w, proj_r, gcol_r, grow_r):
        r0 = pl.multiple_of(j * CHUNK, CHUNK)

        def project_piece(k):
            cs = slice(k * PAIR, (k + 1) * PAIR)
            y = lax.dot_general(xn_cur[...], w_refs[k][...], (((1,), (1,)), ((), ())),
                                preferred_element_type=F32)
            proj_w[j, :, cs] = y
            if k == 4:
                u_ref[...] = y

        first = ((i + tiles_per_seq - 1) % tiles_per_seq == 0) & (j == 0)
        _mlstm_chunk(first, proj_r.at[:, pl.ds(r0, CHUNK), :], gcol_r[pl.ds(r0, CHUNK), :],
                     grow_r[j], bc_ref, br_ref, wqk_ref, gn_ref, hml_ref, ct_ref, m_ref, xe_ref,
                     project_piece)

        xc = xn_cur[pl.ds(r0, CHUNK), :]
        gcol_w[pl.ds(r0, CHUNK), :] = jnp.dot(xc, wgc_ref[...], preferred_element_type=F32)
        grow_w[j] = lax.dot_general(wgr_ref[...], xc, (((1,), (1,)), ((), ())),
                                    preferred_element_type=F32)
        xn_nxt[pl.ds(r0, CHUNK), :] = _rms(xnext_ref[...], g_ref[...]).astype(BF16)

    @pl.when(i % 2 == 0)
    def _():
        step(xn_a, xn_b, proj_a, gcol_a, grow_a, proj_b, gcol_b, grow_b)

    @pl.when(i % 2 == 1)
    def _():
        step(xn_b, xn_a, proj_b, gcol_b, grow_b, proj_a, gcol_a, grow_a)


def _proj_mlstm(x2, g, w_t, w_ut, w_gcol, w_grow, b_col, b_row, wqk, gn, seq):
    m, d = x2.shape
    nt = m // ROW_TILE
    last_chunk = m // CHUNK - 1
    const = lambda shape: pl.BlockSpec(shape, lambda i, j: (0,) * len(shape))
    slab = lambda s: pl.BlockSpec((PAIR, d), lambda i, j: (s * COL_TILES + j, 0))
    return pl.pallas_call(
        functools.partial(_proj_mlstm_kernel, tiles_per_seq=seq // ROW_TILE),
        grid=(nt + 1, COL_TILES),
        in_specs=[
            const((ROW_TILE, d)),
            pl.BlockSpec((CHUNK, d),
                         lambda i, j: (jnp.minimum((i + 1) * COL_TILES + j, last_chunk), 0)),
            const((1, d)),
            slab(0), slab(1), slab(2), slab(3),
            pl.BlockSpec((POOL_GROUP_WIDTH, d), lambda i, j: (j, 0)),
            const((d, 2 * LANES)),
            const((2 * SUBLANES, d)),
            const((1, 2 * LANES)),
            const((2 * SUBLANES, 1)),
            const((QK_CONV, 2 * MLSTM_WIDTH)),
            const((1, MLSTM_WIDTH)),
        ],
        out_specs=[
            pl.BlockSpec((ROW_TILE, POOL_GROUP_WIDTH), lambda i, j: (i, j)),
            pl.BlockSpec((CHUNK, MLSTM_WIDTH),
                         lambda i, j: (jnp.maximum((i - 1) * COL_TILES + j, 0), 0)),
        ],
        out_shape=[
            jax.ShapeDtypeStruct((m + ROW_TILE, POOL_WIDTH), F32),
            jax.ShapeDtypeStruct((m, MLSTM_WIDTH), F32),
        ],
        scratch_shapes=[
            pltpu.VMEM((ROW_TILE, d), BF16),
            pltpu.VMEM((ROW_TILE, d), BF16),
            pltpu.VMEM((COL_TILES, ROW_TILE, TILE_COLS), F32),
            pltpu.VMEM((COL_TILES, ROW_TILE, TILE_COLS), F32),
            pltpu.VMEM((ROW_TILE, 2 * LANES), F32),
            pltpu.VMEM((ROW_TILE, 2 * LANES), F32),
            pltpu.VMEM((COL_TILES, 2 * SUBLANES, CHUNK), F32),
            pltpu.VMEM((COL_TILES, 2 * SUBLANES, CHUNK), F32),
            pltpu.VMEM((HEADS, HEAD_DIM, 2 * HEAD_DIM), F32),
            pltpu.VMEM((SUBLANES, LANES), F32),
            pltpu.VMEM((SUBLANES + CHUNK, 2 * MLSTM_WIDTH), F32),
        ],
        compiler_params=pltpu.CompilerParams(
            dimension_semantics=("arbitrary", "arbitrary"), vmem_limit_bytes=VMEM_LIMIT),
        name="proj_mlstm",
    )(x2, x2, g, w_t, w_t, w_t, w_t, w_ut, w_gcol, w_grow, b_col, b_row, wqk, gn)


MIX_ROW_SPLIT = 2


def _mix_kernel(hml_ref, u_ref, x_ref, wpool_ref, pscale_ref, wo_ml_ref, wo_pool_ref, g_ref,
                gnext_ref, out_ref, hn_ref, ue_ref, *, tiles_per_seq):
    tm = u_ref.shape[0]
    t = pl.program_id(0) % tiles_per_seq

    @pl.when(t == 0)
    def _():
        ue_ref[0:POOL_HALO, :] = jnp.zeros((POOL_HALO, POOL_WIDTH), F32)

    u = u_ref[...]
    ue_ref[POOL_HALO:POOL_HALO + tm, :] = u
    pos = (lax.broadcasted_iota(jnp.int32, (tm, 1), 0) + (t * tm + 1)).astype(F32)
    mixed = []
    for g, window in enumerate(POOL_WINDOWS):
        sl = slice(g * POOL_GROUP_WIDTH, (g + 1) * POOL_GROUP_WIDTH)
        win = ue_ref[:, sl]
        span = 1
        while span < window:
            win = win + pltpu.roll(win, span, axis=0)
            span *= 2
        mean = win[POOL_HALO:, :] / jnp.minimum(pos, float(window))
        pooled = (mean - u[:, sl]).astype(BF16)
        mixed.append((jnp.dot(pooled, wpool_ref[g], preferred_element_type=F32)
                      * pscale_ref[:, sl]).astype(BF16))
    ue_ref[0:POOL_HALO, :] = u[tm - POOL_HALO:tm, :]
    hpool = jnp.concatenate(mixed, axis=1)
    rows = tm // MIX_ROW_SPLIT
    for r in range(MIX_ROW_SPLIT):
        rs = slice(r * rows, (r + 1) * rows)
        mix = (jnp.dot(hml_ref[rs, :].astype(BF16), wo_ml_ref[...], preferred_element_type=F32)
               + jnp.dot(hpool[rs, :], wo_pool_ref[...], preferred_element_type=F32))
        h = x_ref[rs, :] + _rms(mix, g_ref[...])
        out_ref[rs, :] = h
        hn_ref[rs, :] = _rms(h, gnext_ref[...]).astype(BF16)


def _mix(hml, u, x2, w_pool, pool_scale, wo_ml, wo_pool, g_post, g_next, seq, tm):
    m, d = x2.shape
    const = lambda shape: pl.BlockSpec(shape, lambda i: (0,) * len(shape),
                                       pipeline_mode=pl.Buffered(1))
    return pl.pallas_call(
        functools.partial(_mix_kernel, tiles_per_seq=seq // tm),
        grid=(m // tm,),
        in_specs=[
            pl.BlockSpec((tm, MLSTM_WIDTH), lambda i: (i, 0)),
            pl.BlockSpec((tm, POOL_WIDTH), lambda i: (i, 0)),
            pl.BlockSpec((tm, d), lambda i: (i, 0)),
            const(w_pool.shape),
            const((1, POOL_WIDTH)),
            const(wo_ml.shape),
            const(wo_pool.shape),
            const((1, d)),
            const((1, d)),
        ],
        out_specs=[pl.BlockSpec((tm, d), lambda i: (i, 0)), pl.BlockSpec((tm, d), lambda i: (i, 0))],
        out_shape=[jax.ShapeDtypeStruct((m, d), F32), jax.ShapeDtypeStruct((m, d), BF16)],
        scratch_shapes=[pltpu.VMEM((POOL_HALO + tm, POOL_WIDTH), F32)],
        compiler_params=pltpu.CompilerParams(
            dimension_semantics=("arbitrary",), vmem_limit_bytes=VMEM_LIMIT),
        name="mix",
    )(hml, u, x2, w_pool, pool_scale, wo_ml, wo_pool, g_post, g_next)


FFN_ROW_SPLIT = 2


def _ffn_kernel(hn_ref, w1_ref, w2_ref, acc_ref):
    j = pl.program_id(1)
    rows = hn_ref.shape[0] // FFN_ROW_SPLIT
    groups = [slice(r * rows, (r + 1) * rows) for r in range(FFN_ROW_SPLIT)]
    w1 = w1_ref[...].astype(BF16)
    w2 = w2_ref[...].astype(BF16)
    acts = []
    for g in groups:
        a = jnp.dot(hn_ref[g, :], w1, preferred_element_type=F32)
        acts.append(jnp.square(jnp.maximum(a, 0.0)).astype(BF16))
    for g, a in zip(groups, acts):
        prev = jnp.where(j == 0, 0.0, acc_ref[g, :])
        acc_ref[g, :] = prev + jnp.dot(a, w2, preferred_element_type=F32)


def _ffn(hn, w1, w2, tm, tf):
    m, d = hn.shape
    f = w1.shape[1]
    return pl.pallas_call(
        _ffn_kernel,
        grid=(m // tm, f // tf),
        in_specs=[
            pl.BlockSpec((tm, d), lambda i, j: (i, 0)),
            pl.BlockSpec((d, tf), lambda i, j: (0, j)),
            pl.BlockSpec((tf, d), lambda i, j: (j, 0)),
        ],
        out_specs=pl.BlockSpec((tm, d), lambda i, j: (i, 0)),
        out_shape=jax.ShapeDtypeStruct((m, d), F32),
        compiler_params=pltpu.CompilerParams(
            dimension_semantics=("arbitrary", "arbitrary"), vmem_limit_bytes=VMEM_LIMIT),
        name="ffn",
    )(hn, w1, w2)


def _ple_kernel(h_ref, ff_ref, p_ref, gff_ref, ggate_ref, wgate_ref, wproj_ref, gpost_ref, out_ref):
    rows = h_ref.shape[0] // MIX_ROW_SPLIT
    groups = [slice(r * rows, (r + 1) * rows) for r in range(MIX_ROW_SPLIT)]
    hns = []
    for rs in groups:
        h = h_ref[rs, :] + _rms(ff_ref[rs, :], gff_ref[...])
        out_ref[rs, :] = h
        hns.append(_rms(h, ggate_ref[...]).astype(BF16))
    for rs, hn in zip(groups, hns):
        gate = _sigmoid(jnp.dot(hn, wgate_ref[...], preferred_element_type=F32))
        e = jnp.dot(p_ref[rs, :].astype(BF16), wproj_ref[...], preferred_element_type=F32)
        out_ref[rs, :] += _rms(e * gate, gpost_ref[...])


def _ple(h, ff, p2, g_ff_post, g_gate, w_gate, w_proj, g_post, tm):
    m, d = h.shape
    pd = p2.shape[1]
    const = lambda shape: pl.BlockSpec(shape, lambda i: (0,) * len(shape),
                                       pipeline_mode=pl.Buffered(1))
    return pl.pallas_call(
        _ple_kernel,
        grid=(m // tm,),
        in_specs=[
            pl.BlockSpec((tm, d), lambda i: (i, 0)),
            pl.BlockSpec((tm, d), lambda i: (i, 0)),
            pl.BlockSpec((tm, pd), lambda i: (i, 0)),
            const((1, d)),
            const((1, d)),
            const(w_gate.shape),
            const(w_proj.shape),
            const((1, d)),
        ],
        out_specs=pl.BlockSpec((tm, d), lambda i: (i, 0)),
        out_shape=jax.ShapeDtypeStruct((m, d), F32),
        compiler_params=pltpu.CompilerParams(
            dimension_semantics=("arbitrary",), vmem_limit_bytes=VMEM_LIMIT),
        name="ple",
    )(h, ff, p2, g_ff_post, g_gate, w_gate, w_proj, g_post)


def _tile_qk_columns(w_q, w_k):
    r = w_q.shape[0]
    a = jnp.stack([w_q.reshape(r, COL_TILES, PAIR), w_k.reshape(r, COL_TILES, PAIR)], axis=2)
    return a.reshape(r, 2 * MLSTM_WIDTH)


def _layer(h, p2, seq, w_in, b_gates, w_qk_conv, g_mlstm, w_pool, pool_scale, w_out,
           g_mix_pre, g_mix_post, w_ff1, w_ff2, g_ff_pre, g_ff_post, w_ple_proj, w_ple_gate,
           g_ple_gate, g_ple_post):
    qkvo = 4 * MLSTM_WIDTH
    row = lambda v: v.reshape(1, -1).astype(F32)
    w_t = w_in.T.astype(BF16)
    w_ut = w_t[qkvo + 2 * HEADS:]
    w_gi = w_in[:, qkvo:qkvo + HEADS]
    w_gf = w_in[:, qkvo + HEADS:qkvo + 2 * HEADS]
    lane_pad = lambda w: jnp.pad(w, ((0, 0), (0, LANES - HEADS)))
    w_gcol = jnp.concatenate([lane_pad(w_gi), lane_pad(w_gf)], axis=1).astype(BF16)
    w_grow = jnp.concatenate([w_gi, w_gf], axis=1).T.astype(BF16)
    b_col = jnp.concatenate([lane_pad(b_gates[None, :HEADS]), lane_pad(b_gates[None, HEADS:])],
                            axis=1).astype(F32)
    b_row = b_gates.reshape(2 * HEADS, 1).astype(F32)
    wqk = _tile_qk_columns(w_qk_conv[:, :MLSTM_WIDTH], w_qk_conv[:, MLSTM_WIDTH:]).astype(F32)

    u, hml = _proj_mlstm(h, row(g_mix_pre), w_t, w_ut, w_gcol, w_grow, b_col, b_row, wqk,
                         row(g_mlstm), seq)
    w_out16 = w_out.astype(BF16)
    h, hn = _mix(hml, u, h, w_pool.astype(BF16), row(pool_scale), w_out16[:MLSTM_WIDTH],
                 w_out16[MLSTM_WIDTH:], row(g_mix_post), row(g_ff_pre), seq, tm=512)
    ff = _ffn(hn, w_ff1, w_ff2, tm=1024, tf=512)
    return _ple(h, ff, p2, row(g_ff_post), row(g_ple_gate), w_ple_gate.astype(BF16),
                w_ple_proj.astype(BF16), row(g_ple_post), tm=512)


def kernel(x, p, w_in, b_gates, w_qk_conv, g_mlstm, w_pool, pool_scale, w_out, g_mix_pre,
           g_mix_post, w_ff1, w_ff2, g_ff_pre, g_ff_post, w_ple_proj, w_ple_gate, g_ple_gate,
           g_ple_post):
    batch, seq, d = x.shape
    h = x.reshape(batch * seq, d)
    for i in range(p.shape[0]):
        h = _layer(h, p[i].reshape(batch * seq, -1), seq, w_in[i], b_gates[i],
                   w_qk_conv[i], g_mlstm[i], w_pool[i], pool_scale[i], w_out[i], g_mix_pre[i],
                   g_mix_post[i], w_ff1[i], w_ff2[i], g_ff_pre[i], g_ff_post[i], w_ple_proj[i],
                   w_ple_gate[i], g_ple_gate[i], g_ple_post[i])
    return h.reshape(batch, seq, d)
```

```python
import functools

import jax
import jax.numpy as jnp
from jax import lax
from jax.experimental import pallas as pl
from jax.experimental.pallas import tpu as pltpu

EPS = 1e-6
HEADS = 8
HEAD_DIM = 128
MLSTM_WIDTH = HEADS * HEAD_DIM
QK_CONV = 4
POOL_WINDOWS = (2, 4, 8, 16)
POOL_GROUP_WIDTH = 256
POOL_WIDTH = POOL_GROUP_WIDTH * len(POOL_WINDOWS)

LANES = 128
SUBLANES = 8
POOL_HALO = 16
CHUNK = 128
VMEM_LIMIT = 56 * 1024 * 1024

COL_TILES = 4
HEADS_PER_TILE = HEADS // COL_TILES
PAIR = HEADS_PER_TILE * HEAD_DIM
TILE_COLS = 4 * PAIR + POOL_GROUP_WIDTH
ROW_TILE = COL_TILES * CHUNK

F32 = jnp.float32
BF16 = jnp.bfloat16


def _rms(x, g):
    return x * lax.rsqrt(jnp.mean(x * x, axis=-1, keepdims=True) + EPS) * g


def _log_sigmoid(x):
    return jnp.minimum(x, 0.0) - jnp.log1p(jnp.exp(-jnp.abs(x)))


def _sigmoid(x):
    return 1.0 / (1.0 + jnp.exp(-x))


def _mlstm_chunk(first, proj_ref, gc, gr, bc_ref, br_ref, wqk_ref, gn_ref, out_ref, ct_ref, m_ref,
                 xe_ref, between):
    L = CHUNK
    between(0)

    xe_ref[0:SUBLANES, :] = jnp.where(first, 0.0, xe_ref[0:SUBLANES, :])
    qk = []
    for t in range(COL_TILES):
        cs = slice(t * 2 * PAIR, (t + 1) * 2 * PAIR)
        x = proj_ref[t, :, 0:2 * PAIR]
        xe_ref[SUBLANES:SUBLANES + L, cs] = x
        acc = x * wqk_ref[QK_CONV - 1:QK_CONV, cs]
        for back in range(1, QK_CONV):
            tap = QK_CONV - 1 - back
            acc = acc + xe_ref[SUBLANES - back:SUBLANES - back + L, cs] * wqk_ref[tap:tap + 1, cs]
        xe_ref[0:SUBLANES, cs] = x[L - SUBLANES:L, :]
        qk.append(acc * _sigmoid(acc))

    gc = gc + bc_ref[...]
    li_c = gc[:, :LANES]
    lf_c = _log_sigmoid(gc[:, LANES:])
    gr = gr + br_ref[...]
    li_r = gr[0:SUBLANES, :]
    lf_r = _log_sigmoid(gr[SUBLANES:, :])
    row = lax.broadcasted_iota(jnp.int32, (L, L), 0)
    col = lax.broadcasted_iota(jnp.int32, (L, L), 1)
    causal = row >= col
    tril = jnp.where(causal, 1.0, 0.0).astype(F32)
    triu = jnp.where(row <= col, 1.0, 0.0).astype(F32)
    b_c = jnp.dot(tril, lf_c, preferred_element_type=F32, precision=lax.Precision.HIGHEST)
    b_r = jnp.dot(lf_r, triu, preferred_element_type=F32, precision=lax.Precision.HIGHEST)
    r_r = li_r - b_r
    btot = b_c[L - 1:L, :]
    a_c = btot - b_c + li_c
    m_prev = jnp.where(first, 0.0, m_ref[0:1, :])
    m_new = jnp.maximum(btot + m_prev, jnp.max(a_c, axis=0, keepdims=True))
    decay = jnp.exp(btot + m_prev - m_new)
    w_c = jnp.exp(a_c - m_new)
    inter_log = b_c + m_prev
    m_ref[0:1, :] = m_new

    ones_col = jnp.where(lax.broadcasted_iota(jnp.int32, (L, LANES), 1) == 0, 1.0, 0.0).astype(F32)

    for h in range(HEADS):
        t, e = divmod(h, HEADS_PER_TILE)
        if e == 0:
            between(t + 1)
        es = slice(e * HEAD_DIM, (e + 1) * HEAD_DIM)
        sl = slice(h * HEAD_DIM, (h + 1) * HEAD_DIM)
        qf = qk[t][:, es]
        qh = qf.astype(BF16)
        kf = qk[t][:, PAIR + e * HEAD_DIM:PAIR + (e + 1) * HEAD_DIM] * (HEAD_DIM ** -0.5)
        kh = kf.astype(BF16)
        vh = proj_ref[t, :, 2 * PAIR + e * HEAD_DIM:2 * PAIR + (e + 1) * HEAD_DIM]
        oh = proj_ref[t, :, 3 * PAIR + e * HEAD_DIM:3 * PAIR + (e + 1) * HEAD_DIM]
        vaug = jnp.concatenate([vh, ones_col], axis=1)
        dlog = jnp.where(causal, b_c[:, h:h + 1] + r_r[h:h + 1, :], -jnp.inf)
        inter = inter_log[:, h:h + 1]
        m_t = jnp.maximum(inter, jnp.max(dlog, axis=1, keepdims=True))
        dw = jnp.exp(dlog - m_t)
        inter_w = jnp.exp(inter - m_t)
        s = lax.dot_general(qh, kh, (((1,), (1,)), ((), ())), preferred_element_type=F32) * dw
        ct = jnp.where(first, 0.0, ct_ref[h])
        lhs = jnp.concatenate([s.astype(BF16), (inter_w * qf).astype(BF16)], axis=1)
        rhs = jnp.concatenate([vaug.astype(BF16), ct.astype(BF16)], axis=0)
        tot = jnp.dot(lhs, rhs, preferred_element_type=F32)
        num = tot[:, :HEAD_DIM]
        den = tot[:, HEAD_DIM:HEAD_DIM + 1]
        cell = num / jnp.maximum(jnp.abs(den), jnp.exp(-m_t))
        cell = _rms(cell, gn_ref[:, sl])
        out_ref[:, sl] = _sigmoid(oh) * cell
        wv = (w_c[:, h:h + 1] * vaug).astype(BF16)
        upd = lax.dot_general(kh, wv, (((0,), (0,)), ((), ())), preferred_element_type=F32)
        ct_ref[h] = decay[:, h:h + 1] * ct + upd


def _proj_mlstm_kernel(x0_ref, xnext_ref, g_ref, wq_ref, wk_ref, wv_ref, wo_ref, wu_ref, wgc_ref,
                       bc_ref, br_ref, wqk_ref, gn_ref,
                       u_ref, hml_ref, xn_a, xn_b, proj_a, proj_b, gcol_a, gcol_b, grow_a, grow_b,
                       ct_ref, m_ref, xe_ref, *, tiles_per_seq):
    i = pl.program_id(0)
    j = pl.program_id(1)
    w_refs = (wq_ref, wk_ref, wv_ref, wo_ref, wu_ref)

    @pl.when((i == 0) & (j == 0))
    def _():
        xn_a[...] = _rms(x0_ref[...], g_ref[...]).astype(BF16)
        proj_b[...] = jnp.zeros_like(proj_b)
        gcol_b[...] = jnp.zeros_like(gcol_b)
        grow_b[...] = jnp.zeros_like(grow_b)
        ct_ref[...] = jnp.zeros_like(ct_ref)
        m_ref[...] = jnp.zeros_like(m_ref)
        xe_ref[...] = jnp.zeros_like(xe_ref)

    def step(xn_cur, xn_nxt, proj_w, gcol_w, grow_w, proj_r, gcol_r, grow_r):
        r0 = pl.multiple_of(j * CHUNK, CHUNK)

        def project_piece(k):
            cs = slice(k * PAIR, (k + 1) * PAIR)
            y = jnp.dot(xn_cur[...], w_refs[k][...], preferred_element_type=F32)
            proj_w[j, :, cs] = y
            if k == 4:
                u_ref[...] = y

        first = ((i + tiles_per_seq - 1) % tiles_per_seq == 0) & (j == 0)
        _mlstm_chunk(first, proj_r.at[:, pl.ds(r0, CHUNK), :], gcol_r[pl.ds(r0, CHUNK), :],
                     grow_r[j], bc_ref, br_ref, wqk_ref, gn_ref, hml_ref, ct_ref, m_ref, xe_ref,
                     project_piece)

        gc = jnp.dot(xn_cur[pl.ds(r0, CHUNK), :], wgc_ref[...], preferred_element_type=F32)
        gcol_w[pl.ds(r0, CHUNK), :] = gc
        grow_w[j] = jnp.concatenate([gc[:, :LANES].T[0:SUBLANES, :], gc[:, LANES:].T[0:SUBLANES, :]],
                                    axis=0)
        xn_nxt[pl.ds(r0, CHUNK), :] = _rms(xnext_ref[...], g_ref[...]).astype(BF16)

    @pl.when(i % 2 == 0)
    def _():
        step(xn_a, xn_b, proj_a, gcol_a, grow_a, proj_b, gcol_b, grow_b)

    @pl.when(i % 2 == 1)
    def _():
        step(xn_b, xn_a, proj_b, gcol_b, grow_b, proj_a, gcol_a, grow_a)


def _proj_mlstm(x2, g, w_in16, w_u16, w_gcol, b_col, b_row, wqk, gn, seq):
    m, d = x2.shape
    nt = m // ROW_TILE
    last_chunk = m // CHUNK - 1
    const = lambda shape: pl.BlockSpec(shape, lambda i, j: (0,) * len(shape))
    slab = lambda s: pl.BlockSpec((d, PAIR), lambda i, j: (0, s * COL_TILES + j))
    return pl.pallas_call(
        functools.partial(_proj_mlstm_kernel, tiles_per_seq=seq // ROW_TILE),
        grid=(nt + 1, COL_TILES),
        in_specs=[
            const((ROW_TILE, d)),
            pl.BlockSpec((CHUNK, d),
                         lambda i, j: (jnp.minimum((i + 1) * COL_TILES + j, last_chunk), 0)),
            const((1, d)),
            slab(0), slab(1), slab(2), slab(3),
            pl.BlockSpec((d, POOL_GROUP_WIDTH), lambda i, j: (0, j)),
            const((d, 2 * LANES)),
            const((1, 2 * LANES)),
            const((2 * SUBLANES, 1)),
            const((QK_CONV, 2 * MLSTM_WIDTH)),
            const((1, MLSTM_WIDTH)),
        ],
        out_specs=[
            pl.BlockSpec((ROW_TILE, POOL_GROUP_WIDTH), lambda i, j: (i, j)),
            pl.BlockSpec((CHUNK, MLSTM_WIDTH),
                         lambda i, j: (jnp.maximum((i - 1) * COL_TILES + j, 0), 0)),
        ],
        out_shape=[
            jax.ShapeDtypeStruct((m + ROW_TILE, POOL_WIDTH), F32),
            jax.ShapeDtypeStruct((m, MLSTM_WIDTH), F32),
        ],
        scratch_shapes=[
            pltpu.VMEM((ROW_TILE, d), BF16),
            pltpu.VMEM((ROW_TILE, d), BF16),
            pltpu.VMEM((COL_TILES, ROW_TILE, TILE_COLS), F32),
            pltpu.VMEM((COL_TILES, ROW_TILE, TILE_COLS), F32),
            pltpu.VMEM((ROW_TILE, 2 * LANES), F32),
            pltpu.VMEM((ROW_TILE, 2 * LANES), F32),
            pltpu.VMEM((COL_TILES, 2 * SUBLANES, CHUNK), F32),
            pltpu.VMEM((COL_TILES, 2 * SUBLANES, CHUNK), F32),
            pltpu.VMEM((HEADS, HEAD_DIM, 2 * HEAD_DIM), F32),
            pltpu.VMEM((SUBLANES, LANES), F32),
            pltpu.VMEM((SUBLANES + CHUNK, 2 * MLSTM_WIDTH), F32),
        ],
        compiler_params=pltpu.CompilerParams(
            dimension_semantics=("arbitrary", "arbitrary"), vmem_limit_bytes=VMEM_LIMIT),
        name="proj_mlstm",
    )(x2, x2, g, w_in16, w_in16, w_in16, w_in16, w_u16, w_gcol, b_col, b_row, wqk, gn)


MIX_ROW_SPLIT = 2


def _mix_kernel(hml_ref, u_ref, x_ref, wpool_ref, pscale_ref, wo_ml_ref, wo_pool_ref, g_ref,
                gnext_ref, out_ref, hn_ref, ue_ref, *, tiles_per_seq):
    tm = u_ref.shape[0]
    t = pl.program_id(0) % tiles_per_seq

    @pl.when(t == 0)
    def _():
        ue_ref[0:POOL_HALO, :] = jnp.zeros((POOL_HALO, POOL_WIDTH), F32)

    u = u_ref[...]
    ue_ref[POOL_HALO:POOL_HALO + tm, :] = u
    pos = (lax.broadcasted_iota(jnp.int32, (tm, 1), 0) + (t * tm + 1)).astype(F32)
    mixed = []
    for g, window in enumerate(POOL_WINDOWS):
        sl = slice(g * POOL_GROUP_WIDTH, (g + 1) * POOL_GROUP_WIDTH)
        win = ue_ref[:, sl]
        span = 1
        while span < window:
            win = win + pltpu.roll(win, span, axis=0)
            span *= 2
        mean = win[POOL_HALO:, :] / jnp.minimum(pos, float(window))
        pooled = (mean - u[:, sl]).astype(BF16)
        mixed.append((jnp.dot(pooled, wpool_ref[g], preferred_element_type=F32)
                      * pscale_ref[:, sl]).astype(BF16))
    ue_ref[0:POOL_HALO, :] = u[tm - POOL_HALO:tm, :]
    hpool = jnp.concatenate(mixed, axis=1)
    rows = tm // MIX_ROW_SPLIT
    for r in range(MIX_ROW_SPLIT):
        rs = slice(r * rows, (r + 1) * rows)
        mix = (jnp.dot(hml_ref[rs, :].astype(BF16), wo_ml_ref[...], preferred_element_type=F32)
               + jnp.dot(hpool[rs, :], wo_pool_ref[...], preferred_element_type=F32))
        h = x_ref[rs, :] + _rms(mix, g_ref[...])
        out_ref[rs, :] = h
        hn_ref[rs, :] = _rms(h, gnext_ref[...]).astype(BF16)


def _mix(hml, u, x2, w_pool, pool_scale, wo_ml, wo_pool, g_post, g_next, seq, tm):
    m, d = x2.shape
    const = lambda shape: pl.BlockSpec(shape, lambda i: (0,) * len(shape),
                                       pipeline_mode=pl.Buffered(1))
    return pl.pallas_call(
        functools.partial(_mix_kernel, tiles_per_seq=seq // tm),
        grid=(m // tm,),
        in_specs=[
            pl.BlockSpec((tm, MLSTM_WIDTH), lambda i: (i, 0)),
            pl.BlockSpec((tm, POOL_WIDTH), lambda i: (i, 0)),
            pl.BlockSpec((tm, d), lambda i: (i, 0)),
            const(w_pool.shape),
            const((1, POOL_WIDTH)),
            const(wo_ml.shape),
            const(wo_pool.shape),
            const((1, d)),
            const((1, d)),
        ],
        out_specs=[pl.BlockSpec((tm, d), lambda i: (i, 0)), pl.BlockSpec((tm, d), lambda i: (i, 0))],
        out_shape=[jax.ShapeDtypeStruct((m, d), F32), jax.ShapeDtypeStruct((m, d), BF16)],
        scratch_shapes=[pltpu.VMEM((POOL_HALO + tm, POOL_WIDTH), F32)],
        compiler_params=pltpu.CompilerParams(
            dimension_semantics=("arbitrary",), vmem_limit_bytes=VMEM_LIMIT),
        name="mix",
    )(hml, u, x2, w_pool, pool_scale, wo_ml, wo_pool, g_post, g_next)


FFN_ROW_SPLIT = 2


def _ffn_kernel(hn_ref, w1_ref, w2_ref, acc_ref):
    j = pl.program_id(1)
    rows = hn_ref.shape[0] // FFN_ROW_SPLIT
    groups = [slice(r * rows, (r + 1) * rows) for r in range(FFN_ROW_SPLIT)]
    w1 = w1_ref[...].astype(BF16)
    w2 = w2_ref[...].astype(BF16)
    acts = []
    for g in groups:
        a = jnp.dot(hn_ref[g, :], w1, preferred_element_type=F32)
        acts.append(jnp.square(jnp.maximum(a, 0.0)).astype(BF16))
    for g, a in zip(groups, acts):
        prev = jnp.where(j == 0, 0.0, acc_ref[g, :])
        acc_ref[g, :] = prev + jnp.dot(a, w2, preferred_element_type=F32)


def _ffn(hn, w1, w2, tm, tf):
    m, d = hn.shape
    f = w1.shape[1]
    return pl.pallas_call(
        _ffn_kernel,
        grid=(m // tm, f // tf),
        in_specs=[
            pl.BlockSpec((tm, d), lambda i, j: (i, 0)),
            pl.BlockSpec((d, tf), lambda i, j: (0, j)),
            pl.BlockSpec((tf, d), lambda i, j: (j, 0)),
        ],
        out_specs=pl.BlockSpec((tm, d), lambda i, j: (i, 0)),
        out_shape=jax.ShapeDtypeStruct((m, d), F32),
        compiler_params=pltpu.CompilerParams(
            dimension_semantics=("arbitrary", "arbitrary"), vmem_limit_bytes=VMEM_LIMIT),
        name="ffn",
    )(hn, w1, w2)


def _ple_kernel(h_ref, ff_ref, p_ref, gff_ref, ggate_ref, wgate_ref, wproj_ref, gpost_ref, out_ref):
    rows = h_ref.shape[0] // MIX_ROW_SPLIT
    groups = [slice(r * rows, (r + 1) * rows) for r in range(MIX_ROW_SPLIT)]
    hns = []
    for rs in groups:
        h = h_ref[rs, :] + _rms(ff_ref[rs, :], gff_ref[...])
        out_ref[rs, :] = h
        hns.append(_rms(h, ggate_ref[...]).astype(BF16))
    for rs, hn in zip(groups, hns):
        gate = _sigmoid(jnp.dot(hn, wgate_ref[...], preferred_element_type=F32))
        e = jnp.dot(p_ref[rs, :].astype(BF16), wproj_ref[...], preferred_element_type=F32)
        out_ref[rs, :] += _rms(e * gate, gpost_ref[...])


def _ple(h, ff, p2, g_ff_post, g_gate, w_gate, w_proj, g_post, tm):
    m, d = h.shape
    pd = p2.shape[1]
    const = lambda shape: pl.BlockSpec(shape, lambda i: (0,) * len(shape),
                                       pipeline_mode=pl.Buffered(1))
    return pl.pallas_call(
        _ple_kernel,
        grid=(m // tm,),
        in_specs=[
            pl.BlockSpec((tm, d), lambda i: (i, 0)),
            pl.BlockSpec((tm, d), lambda i: (i, 0)),
            pl.BlockSpec((tm, pd), lambda i: (i, 0)),
            const((1, d)),
            const((1, d)),
            const(w_gate.shape),
            const(w_proj.shape),
            const((1, d)),
        ],
        out_specs=pl.BlockSpec((tm, d), lambda i: (i, 0)),
        out_shape=jax.ShapeDtypeStruct((m, d), F32),
        compiler_params=pltpu.CompilerParams(
            dimension_semantics=("arbitrary",), vmem_limit_bytes=VMEM_LIMIT),
        name="ple",
    )(h, ff, p2, g_ff_post, g_gate, w_gate, w_proj, g_post)


def _tile_qk_columns(w_q, w_k):
    r = w_q.shape[0]
    a = jnp.stack([w_q.reshape(r, COL_TILES, PAIR), w_k.reshape(r, COL_TILES, PAIR)], axis=2)
    return a.reshape(r, 2 * MLSTM_WIDTH)


def _layer(h, p2, seq, w_in, b_gates, w_qk_conv, g_mlstm, w_pool, pool_scale, w_out,
           g_mix_pre, g_mix_post, w_ff1, w_ff2, g_ff_pre, g_ff_post, w_ple_proj, w_ple_gate,
           g_ple_gate, g_ple_post):
    qkvo = 4 * MLSTM_WIDTH
    row = lambda v: v.reshape(1, -1).astype(F32)
    w_in16 = w_in.astype(BF16)
    w_u16 = w_in16[:, qkvo + 2 * HEADS:]
    w_gi = w_in[:, qkvo:qkvo + HEADS]
    w_gf = w_in[:, qkvo + HEADS:qkvo + 2 * HEADS]
    lane_pad = lambda w: jnp.pad(w, ((0, 0), (0, LANES - HEADS)))
    w_gcol = jnp.concatenate([lane_pad(w_gi), lane_pad(w_gf)], axis=1).astype(BF16)
    b_col = jnp.concatenate([lane_pad(b_gates[None, :HEADS]), lane_pad(b_gates[None, HEADS:])],
                            axis=1).astype(F32)
    b_row = b_gates.reshape(2 * HEADS, 1).astype(F32)
    wqk = _tile_qk_columns(w_qk_conv[:, :MLSTM_WIDTH], w_qk_conv[:, MLSTM_WIDTH:]).astype(F32)

    u, hml = _proj_mlstm(h, row(g_mix_pre), w_in16, w_u16, w_gcol, b_col, b_row, wqk,
                         row(g_mlstm), seq)
    w_out16 = w_out.astype(BF16)
    h, hn = _mix(hml, u, h, w_pool.astype(BF16), row(pool_scale), w_out16[:MLSTM_WIDTH],
                 w_out16[MLSTM_WIDTH:], row(g_mix_post), row(g_ff_pre), seq, tm=512)
    ff = _ffn(hn, w_ff1, w_ff2, tm=1024, tf=512)
    return _ple(h, ff, p2, row(g_ff_post), row(g_ple_gate), w_ple_gate.astype(BF16),
                w_ple_proj.astype(BF16), row(g_ple_post), tm=512)


def kernel(x, p, w_in, b_gates, w_qk_conv, g_mlstm, w_pool, pool_scale, w_out, g_mix_pre,
           g_mix_post, w_ff1, w_ff2, g_ff_pre, g_ff_post, w_ple_proj, w_ple_gate, g_ple_gate,
           g_ple_post):
    batch, seq, d = x.shape
    h = x.reshape(batch * seq, d)
    for i in range(p.shape[0]):
        h = _layer(h, p[i].reshape(batch * seq, -1), seq, w_in[i], b_gates[i],
                   w_qk_conv[i], g_mlstm[i], w_pool[i], pool_scale[i], w_out[i], g_mix_pre[i],
                   g_mix_post[i], w_ff1[i], w_ff2[i], g_ff_pre[i], g_ff_post[i], w_ple_proj[i],
                   w_ple_gate[i], g_ple_gate[i], g_ple_post[i])
    return h.reshape(batch, seq, d)
```

```python
import functools

import jax
import jax.numpy as jnp
from jax import lax
from jax.experimental import pallas as pl
from jax.experimental.pallas import tpu as pltpu

EPS = 1e-6
HEADS = 8
HEAD_DIM = 128
MLSTM_WIDTH = HEADS * HEAD_DIM
QK_CONV = 4
POOL_WINDOWS = (2, 4, 8, 16)
POOL_GROUP_WIDTH = 256
POOL_WIDTH = POOL_GROUP_WIDTH * len(POOL_WINDOWS)

LANES = 128
SUBLANES = 8
POOL_HALO = 16
CHUNK = 128
VMEM_LIMIT = 56 * 1024 * 1024

COL_TILES = 4
HEADS_PER_TILE = HEADS // COL_TILES
PAIR = HEADS_PER_TILE * HEAD_DIM
TILE_COLS = 4 * PAIR
ROW_TILE = COL_TILES * CHUNK
TILES_PER_STEP = 1

F32 = jnp.float32
BF16 = jnp.bfloat16


def _rms(x, g):
    return x * lax.rsqrt(jnp.mean(x * x, axis=-1, keepdims=True) + EPS) * g


def _log_sigmoid(x):
    return jnp.minimum(x, 0.0) - jnp.log1p(jnp.exp(-jnp.abs(x)))


def _sigmoid(x):
    return 1.0 / (1.0 + jnp.exp(-x))


def _mlstm_chunk(first, proj_ref, gc, gr, bc_ref, br_ref, wqk_ref, gn_ref, out_ref, ct_ref, m_ref,
                 xe_ref, between):
    L = CHUNK
    reset = (lambda v: v) if first is None else (lambda v: jnp.where(first, 0.0, v))
    between(0)

    if first is not None:
        xe_ref[0:SUBLANES, :] = reset(xe_ref[0:SUBLANES, :])
    qk = []
    for t in range(COL_TILES):
        cs = slice(t * 2 * PAIR, (t + 1) * 2 * PAIR)
        x = proj_ref[t, :, 0:2 * PAIR]
        xe_ref[SUBLANES:SUBLANES + L, cs] = x
        acc = x * wqk_ref[QK_CONV - 1:QK_CONV, cs]
        for back in range(1, QK_CONV):
            tap = QK_CONV - 1 - back
            acc = acc + xe_ref[SUBLANES - back:SUBLANES - back + L, cs] * wqk_ref[tap:tap + 1, cs]
        xe_ref[0:SUBLANES, cs] = x[L - SUBLANES:L, :]
        qk.append(acc * _sigmoid(acc))

    gc = gc + bc_ref[...]
    li_c = gc[:, :LANES]
    lf_c = _log_sigmoid(gc[:, LANES:])
    gr = gr + br_ref[...]
    li_r = gr[0:SUBLANES, :]
    lf_r = _log_sigmoid(gr[SUBLANES:, :])
    row = lax.broadcasted_iota(jnp.int32, (L, L), 0)
    col = lax.broadcasted_iota(jnp.int32, (L, L), 1)
    causal = row >= col
    tril = jnp.where(causal, 1.0, 0.0).astype(F32)
    triu = jnp.where(row <= col, 1.0, 0.0).astype(F32)
    b_c = jnp.dot(tril, lf_c, preferred_element_type=F32, precision=lax.Precision.HIGHEST)
    b_r = jnp.dot(lf_r, triu, preferred_element_type=F32, precision=lax.Precision.HIGHEST)
    r_r = li_r - b_r
    btot = b_c[L - 1:L, :]
    a_c = btot - b_c + li_c
    m_prev = reset(m_ref[0:1, :])
    m_new = jnp.maximum(btot + m_prev, jnp.max(a_c, axis=0, keepdims=True))
    decay = jnp.exp(btot + m_prev - m_new)
    w_c = jnp.exp(a_c - m_new)
    inter_log = b_c + m_prev
    m_ref[0:1, :] = m_new

    ones_col = jnp.where(lax.broadcasted_iota(jnp.int32, (L, LANES), 1) == 0, 1.0, 0.0).astype(F32)

    for h in range(HEADS):
        t, e = divmod(h, HEADS_PER_TILE)
        if e == 0:
            between(t + 1)
        es = slice(e * HEAD_DIM, (e + 1) * HEAD_DIM)
        sl = slice(h * HEAD_DIM, (h + 1) * HEAD_DIM)
        qf = qk[t][:, es]
        qh = qf.astype(BF16)
        kf = qk[t][:, PAIR + e * HEAD_DIM:PAIR + (e + 1) * HEAD_DIM] * (HEAD_DIM ** -0.5)
        kh = kf.astype(BF16)
        vh = proj_ref[t, :, 2 * PAIR + e * HEAD_DIM:2 * PAIR + (e + 1) * HEAD_DIM]
        oh = proj_ref[t, :, 3 * PAIR + e * HEAD_DIM:3 * PAIR + (e + 1) * HEAD_DIM]
        vaug = jnp.concatenate([vh, ones_col], axis=1)
        dlog = jnp.where(causal, b_c[:, h:h + 1] + r_r[h:h + 1, :], -jnp.inf)
        inter = inter_log[:, h:h + 1]
        m_t = jnp.maximum(inter, jnp.max(dlog, axis=1, keepdims=True))
        dw = jnp.exp(dlog - m_t)
        inter_w = jnp.exp(inter - m_t)
        s = lax.dot_general(qh, kh, (((1,), (1,)), ((), ())), preferred_element_type=F32) * dw
        ct = reset(ct_ref[h])
        lhs = jnp.concatenate([s.astype(BF16), (inter_w * qf).astype(BF16)], axis=1)
        rhs = jnp.concatenate([vaug.astype(BF16), ct.astype(BF16)], axis=0)
        tot = jnp.dot(lhs, rhs, preferred_element_type=F32)
        num = tot[:, :HEAD_DIM]
        den = tot[:, HEAD_DIM:HEAD_DIM + 1]
        cell = num / jnp.maximum(jnp.abs(den), jnp.exp(-m_t))
        cell = _rms(cell, gn_ref[:, sl])
        out_ref[:, sl] = _sigmoid(oh) * cell
        wv = (w_c[:, h:h + 1] * vaug).astype(BF16)
        upd = lax.dot_general(kh, wv, (((0,), (0,)), ((), ())), preferred_element_type=F32)
        ct_ref[h] = decay[:, h:h + 1] * ct + upd


def _proj_mlstm_kernel(x0_ref, xnext_ref, g_ref, *refs, tiles_per_seq):
    n_w = 5 * TILES_PER_STEP
    w_refs = refs[:n_w]
    (wgc_ref, bc_ref, br_ref, wqk_ref, gn_ref, u_ref, hml_ref, xn_a, xn_b, proj_a, proj_b,
     gcol_a, gcol_b, grow_a, grow_b, ct_ref, m_ref, xe_ref) = refs[n_w:]
    i = pl.program_id(0)
    j = pl.program_id(1)

    @pl.when((i == 0) & (j == 0))
    def _():
        xn_a[...] = _rms(x0_ref[...], g_ref[...]).astype(BF16)
        proj_b[...] = jnp.zeros_like(proj_b)
        gcol_b[...] = jnp.zeros_like(gcol_b)
        grow_b[...] = jnp.zeros_like(grow_b)
        ct_ref[...] = jnp.zeros_like(ct_ref)
        m_ref[...] = jnp.zeros_like(m_ref)
        xe_ref[...] = jnp.zeros_like(xe_ref)

    def step(xn_cur, xn_nxt, proj_w, gcol_w, grow_w, proj_r, gcol_r, grow_r):
        tile_first = (i + tiles_per_seq - 1) % tiles_per_seq == 0
        for t in range(TILES_PER_STEP):
            c = j * TILES_PER_STEP + t
            r0 = pl.multiple_of(c * CHUNK, CHUNK)

            def project_piece(k, t=t, c=c):
                cs = slice(k * PAIR, (k + 1) * PAIR)
                y = jnp.dot(xn_cur[...], w_refs[k * TILES_PER_STEP + t][...],
                            preferred_element_type=F32)
                if k < 4:
                    proj_w[c, :, cs] = y
                else:
                    u_ref[:, t * POOL_GROUP_WIDTH:(t + 1) * POOL_GROUP_WIDTH] = y

            first = (tile_first & (j == 0)) if t == 0 else None
            _mlstm_chunk(first, proj_r.at[:, pl.ds(r0, CHUNK), :], gcol_r[pl.ds(r0, CHUNK), :],
                         grow_r[c], bc_ref, br_ref, wqk_ref, gn_ref,
                         hml_ref.at[t * CHUNK:(t + 1) * CHUNK, :], ct_ref, m_ref, xe_ref,
                         project_piece)

            gc = jnp.dot(xn_cur[pl.ds(r0, CHUNK), :], wgc_ref[...], preferred_element_type=F32)
            gcol_w[pl.ds(r0, CHUNK), :] = gc
            grow_w[c] = jnp.concatenate(
                [gc[:, :LANES].T[0:SUBLANES, :], gc[:, LANES:].T[0:SUBLANES, :]], axis=0)

        rows = TILES_PER_STEP * CHUNK
        rn = pl.multiple_of(j * rows, rows)
        xn_nxt[pl.ds(rn, rows), :] = _rms(xnext_ref[...], g_ref[...]).astype(BF16)

    @pl.when(i % 2 == 0)
    def _():
        step(xn_a, xn_b, proj_a, gcol_a, grow_a, proj_b, gcol_b, grow_b)

    @pl.when(i % 2 == 1)
    def _():
        step(xn_b, xn_a, proj_b, gcol_b, grow_b, proj_a, gcol_a, grow_a)


def _proj_mlstm(x2, g, w_in16, w_u16, w_gcol, b_col, b_row, wqk, gn, seq):
    m, d = x2.shape
    nt = m // ROW_TILE
    steps = COL_TILES // TILES_PER_STEP
    rows = TILES_PER_STEP * CHUNK
    last_block = m // rows - 1
    const = lambda shape: pl.BlockSpec(shape, lambda i, j: (0,) * len(shape),
                                       pipeline_mode=pl.Buffered(1))
    slab = lambda s, t: pl.BlockSpec(
        (d, PAIR), lambda i, j: (0, s * COL_TILES + j * TILES_PER_STEP + t))
    u_slab = lambda t: pl.BlockSpec((d, POOL_GROUP_WIDTH), lambda i, j: (0, j * TILES_PER_STEP + t))
    tiles = range(TILES_PER_STEP)
    w_specs = [slab(s, t) for s in range(4) for t in tiles] + [u_slab(t) for t in tiles]
    w_args = [w_in16] * (4 * TILES_PER_STEP) + [w_u16] * TILES_PER_STEP
    return pl.pallas_call(
        functools.partial(_proj_mlstm_kernel, tiles_per_seq=seq // ROW_TILE),
        grid=(nt + 1, steps),
        in_specs=[
            const((ROW_TILE, d)),
            pl.BlockSpec((rows, d),
                         lambda i, j: (jnp.minimum((i + 1) * steps + j, last_block), 0)),
            const((1, d)),
            *w_specs,
            const((d, 2 * LANES)),
            const((1, 2 * LANES)),
            const((2 * SUBLANES, 1)),
            const((QK_CONV, 2 * MLSTM_WIDTH)),
            const((1, MLSTM_WIDTH)),
        ],
        out_specs=[
            pl.BlockSpec((ROW_TILE, TILES_PER_STEP * POOL_GROUP_WIDTH), lambda i, j: (i, j)),
            pl.BlockSpec((rows, MLSTM_WIDTH),
                         lambda i, j: (jnp.maximum((i - 1) * steps + j, 0), 0)),
        ],
        out_shape=[
            jax.ShapeDtypeStruct((m + ROW_TILE, POOL_WIDTH), F32),
            jax.ShapeDtypeStruct((m, MLSTM_WIDTH), F32),
        ],
        scratch_shapes=[
            pltpu.VMEM((ROW_TILE, d), BF16),
            pltpu.VMEM((ROW_TILE, d), BF16),
            pltpu.VMEM((COL_TILES, ROW_TILE, TILE_COLS), F32),
            pltpu.VMEM((COL_TILES, ROW_TILE, TILE_COLS), F32),
            pltpu.VMEM((ROW_TILE, 2 * LANES), F32),
            pltpu.VMEM((ROW_TILE, 2 * LANES), F32),
            pltpu.VMEM((COL_TILES, 2 * SUBLANES, CHUNK), F32),
            pltpu.VMEM((COL_TILES, 2 * SUBLANES, CHUNK), F32),
            pltpu.VMEM((HEADS, HEAD_DIM, 2 * HEAD_DIM), F32),
            pltpu.VMEM((SUBLANES, LANES), F32),
            pltpu.VMEM((SUBLANES + CHUNK, 2 * MLSTM_WIDTH), F32),
        ],
        compiler_params=pltpu.CompilerParams(
            dimension_semantics=("arbitrary", "arbitrary"), vmem_limit_bytes=VMEM_LIMIT),
        name="proj_mlstm",
    )(x2, x2, g, *w_args, w_gcol, b_col, b_row, wqk, gn)


MIX_ROW_SPLIT = 2


def _mix_kernel(hml_ref, u_ref, x_ref, wpool_ref, pscale_ref, wo_ml_ref, wo_pool_ref, g_ref,
                gnext_ref, out_ref, hn_ref, ue_ref, *, tiles_per_seq):
    tm = u_ref.shape[0]
    t = pl.program_id(0) % tiles_per_seq

    @pl.when(t == 0)
    def _():
        ue_ref[0:POOL_HALO, :] = jnp.zeros((POOL_HALO, POOL_WIDTH), F32)

    u = u_ref[...]
    ue_ref[POOL_HALO:POOL_HALO + tm, :] = u
    pos = (lax.broadcasted_iota(jnp.int32, (tm, 1), 0) + (t * tm + 1)).astype(F32)
    mixed = []
    for g, window in enumerate(POOL_WINDOWS):
        sl = slice(g * POOL_GROUP_WIDTH, (g + 1) * POOL_GROUP_WIDTH)
        win = ue_ref[:, sl]
        span = 1
        while span < window:
            win = win + pltpu.roll(win, span, axis=0)
            span *= 2
        mean = win[POOL_HALO:, :] / jnp.minimum(pos, float(window))
        pooled = (mean - u[:, sl]).astype(BF16)
        mixed.append((jnp.dot(pooled, wpool_ref[g], preferred_element_type=F32)
                      * pscale_ref[:, sl]).astype(BF16))
    ue_ref[0:POOL_HALO, :] = u[tm - POOL_HALO:tm, :]
    hpool = jnp.concatenate(mixed, axis=1)
    rows = tm // MIX_ROW_SPLIT
    for r in range(MIX_ROW_SPLIT):
        rs = slice(r * rows, (r + 1) * rows)
        mix = (jnp.dot(hml_ref[rs, :].astype(BF16), wo_ml_ref[...], preferred_element_type=F32)
               + jnp.dot(hpool[rs, :], wo_pool_ref[...], preferred_element_type=F32))
        h = x_ref[rs, :] + _rms(mix, g_ref[...])
        out_ref[rs, :] = h
        hn_ref[rs, :] = _rms(h, gnext_ref[...]).astype(BF16)


def _mix(hml, u, x2, w_pool, pool_scale, wo_ml, wo_pool, g_post, g_next, seq, tm):
    m, d = x2.shape
    const = lambda shape: pl.BlockSpec(shape, lambda i: (0,) * len(shape),
                                       pipeline_mode=pl.Buffered(1))
    return pl.pallas_call(
        functools.partial(_mix_kernel, tiles_per_seq=seq // tm),
        grid=(m // tm,),
        in_specs=[
            pl.BlockSpec((tm, MLSTM_WIDTH), lambda i: (i, 0)),
            pl.BlockSpec((tm, POOL_WIDTH), lambda i: (i, 0)),
            pl.BlockSpec((tm, d), lambda i: (i, 0)),
            const(w_pool.shape),
            const((1, POOL_WIDTH)),
            const(wo_ml.shape),
            const(wo_pool.shape),
            const((1, d)),
            const((1, d)),
        ],
        out_specs=[pl.BlockSpec((tm, d), lambda i: (i, 0)), pl.BlockSpec((tm, d), lambda i: (i, 0))],
        out_shape=[jax.ShapeDtypeStruct((m, d), F32), jax.ShapeDtypeStruct((m, d), BF16)],
        scratch_shapes=[pltpu.VMEM((POOL_HALO + tm, POOL_WIDTH), F32)],
        compiler_params=pltpu.CompilerParams(
            dimension_semantics=("arbitrary",), vmem_limit_bytes=VMEM_LIMIT),
        name="mix",
    )(hml, u, x2, w_pool, pool_scale, wo_ml, wo_pool, g_post, g_next)


FFN_ROW_SPLIT = 2


def _ffn_kernel(hn_ref, w1_ref, w2_ref, acc_ref):
    j = pl.program_id(1)
    rows = hn_ref.shape[0] // FFN_ROW_SPLIT
    groups = [slice(r * rows, (r + 1) * rows) for r in range(FFN_ROW_SPLIT)]
    w1 = w1_ref[...].astype(BF16)
    w2 = w2_ref[...].astype(BF16)
    acts = []
    for g in groups:
        a = jnp.dot(hn_ref[g, :], w1, preferred_element_type=F32)
        acts.append(jnp.square(jnp.maximum(a, 0.0)).astype(BF16))
    for g, a in zip(groups, acts):
        prev = jnp.where(j == 0, 0.0, acc_ref[g, :])
        acc_ref[g, :] = prev + jnp.dot(a, w2, preferred_element_type=F32)


def _ffn(hn, w1, w2, tm, tf):
    m, d = hn.shape
    f = w1.shape[1]
    return pl.pallas_call(
        _ffn_kernel,
        grid=(m // tm, f // tf),
        in_specs=[
            pl.BlockSpec((tm, d), lambda i, j: (i, 0)),
            pl.BlockSpec((d, tf), lambda i, j: (0, j)),
            pl.BlockSpec((tf, d), lambda i, j: (j, 0)),
        ],
        out_specs=pl.BlockSpec((tm, d), lambda i, j: (i, 0)),
        out_shape=jax.ShapeDtypeStruct((m, d), F32),
        compiler_params=pltpu.CompilerParams(
            dimension_semantics=("arbitrary", "arbitrary"), vmem_limit_bytes=VMEM_LIMIT),
        name="ffn",
    )(hn, w1, w2)


def _ple_kernel(h_ref, ff_ref, p_ref, gff_ref, ggate_ref, wgate_ref, wproj_ref, gpost_ref, out_ref,
                hn_ref):
    rows = h_ref.shape[0] // MIX_ROW_SPLIT
    groups = [slice(r * rows, (r + 1) * rows) for r in range(MIX_ROW_SPLIT)]
    for rs in groups:
        h = h_ref[rs, :] + _rms(ff_ref[rs, :], gff_ref[...])
        out_ref[rs, :] = h
        hn_ref[rs, :] = _rms(h, ggate_ref[...]).astype(BF16)
    for rs in groups:
        gate = _sigmoid(jnp.concatenate(
            [jnp.dot(hn_ref[rs, :], wgate_ref[n], preferred_element_type=F32)
             for n in range(wgate_ref.shape[0])], axis=1))
        e = jnp.dot(p_ref[rs, :].astype(BF16), wproj_ref[...], preferred_element_type=F32)
        out_ref[rs, :] += _rms(e * gate, gpost_ref[...])


def _ple(h, ff, p2, g_ff_post, g_gate, w_gate, w_proj, g_post, tm):
    m, d = h.shape
    pd = p2.shape[1]
    const = lambda shape: pl.BlockSpec(shape, lambda i: (0,) * len(shape),
                                       pipeline_mode=pl.Buffered(1))
    return pl.pallas_call(
        _ple_kernel,
        grid=(m // tm,),
        in_specs=[
            pl.BlockSpec((tm, d), lambda i: (i, 0)),
            pl.BlockSpec((tm, d), lambda i: (i, 0)),
            pl.BlockSpec((tm, pd), lambda i: (i, 0)),
            const((1, d)),
            const((1, d)),
            const(w_gate.shape),
            const(w_proj.shape),
            const((1, d)),
        ],
        out_specs=pl.BlockSpec((tm, d), lambda i: (i, 0)),
        out_shape=jax.ShapeDtypeStruct((m, d), F32),
        scratch_shapes=[pltpu.VMEM((tm, d), BF16)],
        compiler_params=pltpu.CompilerParams(
            dimension_semantics=("arbitrary",), vmem_limit_bytes=VMEM_LIMIT),
        name="ple",
    )(h, ff, p2, g_ff_post, g_gate, w_gate, w_proj, g_post)


MXU_COLS = 256


def _column_slabs(w):
    k, n = w.shape
    return w.reshape(k, n // MXU_COLS, MXU_COLS).transpose(1, 0, 2)


def _tile_qk_columns(w_q, w_k):
    r = w_q.shape[0]
    a = jnp.stack([w_q.reshape(r, COL_TILES, PAIR), w_k.reshape(r, COL_TILES, PAIR)], axis=2)
    return a.reshape(r, 2 * MLSTM_WIDTH)


def _layer(h, p2, seq, w_in, b_gates, w_qk_conv, g_mlstm, w_pool, pool_scale, w_out,
           g_mix_pre, g_mix_post, w_ff1, w_ff2, g_ff_pre, g_ff_post, w_ple_proj, w_ple_gate,
           g_ple_gate, g_ple_post):
    qkvo = 4 * MLSTM_WIDTH
    row = lambda v: v.reshape(1, -1).astype(F32)
    w_in16 = w_in.astype(BF16)
    w_u16 = w_in16[:, qkvo + 2 * HEADS:]
    w_gi = w_in[:, qkvo:qkvo + HEADS]
    w_gf = w_in[:, qkvo + HEADS:qkvo + 2 * HEADS]
    lane_pad = lambda w: jnp.pad(w, ((0, 0), (0, LANES - HEADS)))
    w_gcol = jnp.concatenate([lane_pad(w_gi), lane_pad(w_gf)], axis=1).astype(BF16)
    b_col = jnp.concatenate([lane_pad(b_gates[None, :HEADS]), lane_pad(b_gates[None, HEADS:])],
                            axis=1).astype(F32)
    b_row = b_gates.reshape(2 * HEADS, 1).astype(F32)
    wqk = _tile_qk_columns(w_qk_conv[:, :MLSTM_WIDTH], w_qk_conv[:, MLSTM_WIDTH:]).astype(F32)

    u, hml = _proj_mlstm(h, row(g_mix_pre), w_in16, w_u16, w_gcol, b_col, b_row, wqk,
                         row(g_mlstm), seq)
    w_out16 = w_out.astype(BF16)
    h, hn = _mix(hml, u, h, w_pool.astype(BF16), row(pool_scale), w_out16[:MLSTM_WIDTH],
                 w_out16[MLSTM_WIDTH:], row(g_mix_post), row(g_ff_pre), seq, tm=512)
    ff = _ffn(hn, w_ff1, w_ff2.astype(BF16), tm=1024, tf=1024)
    return _ple(h, ff, p2, row(g_ff_post), row(g_ple_gate), _column_slabs(w_ple_gate.astype(BF16)),
                w_ple_proj.astype(BF16), row(g_ple_post), tm=512)


def kernel(x, p, w_in, b_gates, w_qk_conv, g_mlstm, w_pool, pool_scale, w_out, g_mix_pre,
           g_mix_post, w_ff1, w_ff2, g_ff_pre, g_ff_post, w_ple_proj, w_ple_gate, g_ple_gate,
           g_ple_post):
    batch, seq, d = x.shape
    h = x.reshape(batch * seq, d)
    for i in range(p.shape[0]):
        h = _layer(h, p[i].reshape(batch * seq, -1), seq, w_in[i], b_gates[i],
                   w_qk_conv[i], g_mlstm[i], w_pool[i], pool_scale[i], w_out[i], g_mix_pre[i],
                   g_mix_post[i], w_ff1[i], w_ff2[i], g_ff_pre[i], g_ff_post[i], w_ple_proj[i],
                   w_ple_gate[i], g_ple_gate[i], g_ple_post[i])
    return h.reshape(batch, seq, d)
```

```python
import functools

import jax
import jax.numpy as jnp
from jax import lax
from jax.experimental import pallas as pl
from jax.experimental.pallas import tpu as pltpu

EPS = 1e-6
HEADS = 8
HEAD_DIM = 128
MLSTM_WIDTH = HEADS * HEAD_DIM
QK_CONV = 4
POOL_WINDOWS = (2, 4, 8, 16)
POOL_GROUP_WIDTH = 256
POOL_WIDTH = POOL_GROUP_WIDTH * len(POOL_WINDOWS)

LANES = 128
SUBLANES = 8
POOL_HALO = 16
CHUNK = 128
VMEM_LIMIT = 56 * 1024 * 1024

COL_TILES = 4
HEADS_PER_TILE = HEADS // COL_TILES
PAIR = HEADS_PER_TILE * HEAD_DIM
TILE_COLS = 4 * PAIR
ROW_TILE = COL_TILES * CHUNK
TILES_PER_STEP = 1

F32 = jnp.float32
BF16 = jnp.bfloat16


def _rms(x, g):
    return x * lax.rsqrt(jnp.mean(x * x, axis=-1, keepdims=True) + EPS) * g


def _log_sigmoid(x):
    return jnp.minimum(x, 0.0) - jnp.log1p(jnp.exp(-jnp.abs(x)))


def _sigmoid(x):
    return 1.0 / (1.0 + jnp.exp(-x))


def _mlstm_chunk(first, proj_ref, gc, gr, bc_ref, br_ref, wqk_ref, gn_ref, out_ref, ct_ref, m_ref,
                 xe_ref, between):
    L = CHUNK
    reset = (lambda v: v) if first is None else (lambda v: jnp.where(first, 0.0, v))
    between(0)

    if first is not None:
        xe_ref[0:SUBLANES, :] = reset(xe_ref[0:SUBLANES, :])
    qk = []
    for t in range(COL_TILES):
        cs = slice(t * 2 * PAIR, (t + 1) * 2 * PAIR)
        x = proj_ref[t, :, 0:2 * PAIR]
        xe_ref[SUBLANES:SUBLANES + L, cs] = x
        acc = x * wqk_ref[QK_CONV - 1:QK_CONV, cs]
        for back in range(1, QK_CONV):
            tap = QK_CONV - 1 - back
            acc = acc + xe_ref[SUBLANES - back:SUBLANES - back + L, cs] * wqk_ref[tap:tap + 1, cs]
        xe_ref[0:SUBLANES, cs] = x[L - SUBLANES:L, :]
        qk.append(acc * _sigmoid(acc))

    gc = gc + bc_ref[...]
    li_c = gc[:, :LANES]
    lf_c = _log_sigmoid(gc[:, LANES:])
    gr = gr + br_ref[...]
    li_r = gr[0:SUBLANES, :]
    lf_r = _log_sigmoid(gr[SUBLANES:, :])
    row = lax.broadcasted_iota(jnp.int32, (L, L), 0)
    col = lax.broadcasted_iota(jnp.int32, (L, L), 1)
    causal = row >= col
    tril = jnp.where(causal, 1.0, 0.0).astype(F32)
    triu = jnp.where(row <= col, 1.0, 0.0).astype(F32)
    b_c = jnp.dot(tril, lf_c, preferred_element_type=F32, precision=lax.Precision.HIGHEST)
    b_r = jnp.dot(lf_r, triu, preferred_element_type=F32, precision=lax.Precision.HIGHEST)
    r_r = li_r - b_r
    btot = b_c[L - 1:L, :]
    a_c = btot - b_c + li_c
    m_prev = reset(m_ref[0:1, :])
    m_new = jnp.maximum(btot + m_prev, jnp.max(a_c, axis=0, keepdims=True))
    decay = jnp.exp(btot + m_prev - m_new)
    w_c = jnp.exp(a_c - m_new)
    inter_log = b_c + m_prev
    m_ref[0:1, :] = m_new

    ones_col = jnp.where(lax.broadcasted_iota(jnp.int32, (L, LANES), 1) == 0, 1.0, 0.0).astype(F32)

    for h in range(HEADS):
        t, e = divmod(h, HEADS_PER_TILE)
        if e == 0:
            between(t + 1)
        es = slice(e * HEAD_DIM, (e + 1) * HEAD_DIM)
        sl = slice(h * HEAD_DIM, (h + 1) * HEAD_DIM)
        qf = qk[t][:, es]
        qh = qf.astype(BF16)
        kf = qk[t][:, PAIR + e * HEAD_DIM:PAIR + (e + 1) * HEAD_DIM] * (HEAD_DIM ** -0.5)
        kh = kf.astype(BF16)
        vh = proj_ref[t, :, 2 * PAIR + e * HEAD_DIM:2 * PAIR + (e + 1) * HEAD_DIM]
        oh = proj_ref[t, :, 3 * PAIR + e * HEAD_DIM:3 * PAIR + (e + 1) * HEAD_DIM]
        vaug = jnp.concatenate([vh, ones_col], axis=1)
        dlog = jnp.where(causal, b_c[:, h:h + 1] + r_r[h:h + 1, :], -jnp.inf)
        inter = inter_log[:, h:h + 1]
        m_t = jnp.maximum(inter, jnp.max(dlog, axis=1, keepdims=True))
        dw = jnp.exp(dlog - m_t)
        inter_w = jnp.exp(inter - m_t)
        s = lax.dot_general(qh, kh, (((1,), (1,)), ((), ())), preferred_element_type=F32) * dw
        ct = reset(ct_ref[h])
        lhs = jnp.concatenate([s.astype(BF16), (inter_w * qf).astype(BF16)], axis=1)
        rhs = jnp.concatenate([vaug.astype(BF16), ct.astype(BF16)], axis=0)
        tot = jnp.dot(lhs, rhs, preferred_element_type=F32)
        num = tot[:, :HEAD_DIM]
        den = tot[:, HEAD_DIM:HEAD_DIM + 1]
        cell = num / jnp.maximum(jnp.abs(den), jnp.exp(-m_t))
        cell = _rms(cell, gn_ref[:, sl])
        out_ref[:, sl] = (_sigmoid(oh) * cell).astype(out_ref.dtype)
        wv = (w_c[:, h:h + 1] * vaug).astype(BF16)
        upd = lax.dot_general(kh, wv, (((0,), (0,)), ((), ())), preferred_element_type=F32)
        ct_ref[h] = decay[:, h:h + 1] * ct + upd


def _proj_mlstm_kernel(x0_ref, xnext_ref, g_ref, *refs, tiles_per_seq):
    n_w = 5 * TILES_PER_STEP
    w_refs = refs[:n_w]
    (wgc_ref, bc_ref, br_ref, wqk_ref, gn_ref, u_ref, hml_ref, xn_a, xn_b, proj_a, proj_b,
     gcol_a, gcol_b, grow_a, grow_b, ct_ref, m_ref, xe_ref) = refs[n_w:]
    i = pl.program_id(0)
    j = pl.program_id(1)

    @pl.when((i == 0) & (j == 0))
    def _():
        xn_a[...] = _rms(x0_ref[...], g_ref[...]).astype(BF16)
        proj_b[...] = jnp.zeros_like(proj_b)
        gcol_b[...] = jnp.zeros_like(gcol_b)
        grow_b[...] = jnp.zeros_like(grow_b)
        ct_ref[...] = jnp.zeros_like(ct_ref)
        m_ref[...] = jnp.zeros_like(m_ref)
        xe_ref[...] = jnp.zeros_like(xe_ref)

    def step(xn_cur, xn_nxt, proj_w, gcol_w, grow_w, proj_r, gcol_r, grow_r):
        tile_first = (i + tiles_per_seq - 1) % tiles_per_seq == 0
        for t in range(TILES_PER_STEP):
            c = j * TILES_PER_STEP + t
            r0 = pl.multiple_of(c * CHUNK, CHUNK)

            def project_piece(k, t=t, c=c):
                cs = slice(k * PAIR, (k + 1) * PAIR)
                y = jnp.dot(xn_cur[...], w_refs[k * TILES_PER_STEP + t][...],
                            preferred_element_type=F32)
                if k < 4:
                    proj_w[c, :, cs] = y
                else:
                    u_ref[:, t * POOL_GROUP_WIDTH:(t + 1) * POOL_GROUP_WIDTH] = y

            first = (tile_first & (j == 0)) if t == 0 else None
            _mlstm_chunk(first, proj_r.at[:, pl.ds(r0, CHUNK), :], gcol_r[pl.ds(r0, CHUNK), :],
                         grow_r[c], bc_ref, br_ref, wqk_ref, gn_ref,
                         hml_ref.at[t * CHUNK:(t + 1) * CHUNK, :], ct_ref, m_ref, xe_ref,
                         project_piece)

            gc = jnp.dot(xn_cur[pl.ds(r0, CHUNK), :], wgc_ref[...], preferred_element_type=F32)
            gcol_w[pl.ds(r0, CHUNK), :] = gc
            grow_w[c] = jnp.concatenate(
                [gc[:, :LANES].T[0:SUBLANES, :], gc[:, LANES:].T[0:SUBLANES, :]], axis=0)

        rows = TILES_PER_STEP * CHUNK
        rn = pl.multiple_of(j * rows, rows)
        xn_nxt[pl.ds(rn, rows), :] = _rms(xnext_ref[...], g_ref[...]).astype(BF16)

    @pl.when(i % 2 == 0)
    def _():
        step(xn_a, xn_b, proj_a, gcol_a, grow_a, proj_b, gcol_b, grow_b)

    @pl.when(i % 2 == 1)
    def _():
        step(xn_b, xn_a, proj_b, gcol_b, grow_b, proj_a, gcol_a, grow_a)


def _proj_mlstm(x2, g, w_in16, w_u16, w_gcol, b_col, b_row, wqk, gn, seq):
    m, d = x2.shape
    nt = m // ROW_TILE
    steps = COL_TILES // TILES_PER_STEP
    rows = TILES_PER_STEP * CHUNK
    last_block = m // rows - 1
    const = lambda shape: pl.BlockSpec(shape, lambda i, j: (0,) * len(shape),
                                       pipeline_mode=pl.Buffered(1))
    slab = lambda s, t: pl.BlockSpec(
        (d, PAIR), lambda i, j: (0, s * COL_TILES + j * TILES_PER_STEP + t))
    u_slab = lambda t: pl.BlockSpec((d, POOL_GROUP_WIDTH), lambda i, j: (0, j * TILES_PER_STEP + t))
    tiles = range(TILES_PER_STEP)
    w_specs = [slab(s, t) for s in range(4) for t in tiles] + [u_slab(t) for t in tiles]
    w_args = [w_in16] * (4 * TILES_PER_STEP) + [w_u16] * TILES_PER_STEP
    return pl.pallas_call(
        functools.partial(_proj_mlstm_kernel, tiles_per_seq=seq // ROW_TILE),
        grid=(nt + 1, steps),
        in_specs=[
            const((ROW_TILE, d)),
            pl.BlockSpec((rows, d),
                         lambda i, j: (jnp.minimum((i + 1) * steps + j, last_block), 0)),
            const((1, d)),
            *w_specs,
            const((d, 2 * LANES)),
            const((1, 2 * LANES)),
            const((2 * SUBLANES, 1)),
            const((QK_CONV, 2 * MLSTM_WIDTH)),
            const((1, MLSTM_WIDTH)),
        ],
        out_specs=[
            pl.BlockSpec((ROW_TILE, TILES_PER_STEP * POOL_GROUP_WIDTH), lambda i, j: (i, j)),
            pl.BlockSpec((rows, MLSTM_WIDTH),
                         lambda i, j: (jnp.maximum((i - 1) * steps + j, 0), 0)),
        ],
        out_shape=[
            jax.ShapeDtypeStruct((m + ROW_TILE, POOL_WIDTH), F32),
            jax.ShapeDtypeStruct((m, MLSTM_WIDTH), BF16),
        ],
        scratch_shapes=[
            pltpu.VMEM((ROW_TILE, d), BF16),
            pltpu.VMEM((ROW_TILE, d), BF16),
            pltpu.VMEM((COL_TILES, ROW_TILE, TILE_COLS), F32),
            pltpu.VMEM((COL_TILES, ROW_TILE, TILE_COLS), F32),
            pltpu.VMEM((ROW_TILE, 2 * LANES), F32),
            pltpu.VMEM((ROW_TILE, 2 * LANES), F32),
            pltpu.VMEM((COL_TILES, 2 * SUBLANES, CHUNK), F32),
            pltpu.VMEM((COL_TILES, 2 * SUBLANES, CHUNK), F32),
            pltpu.VMEM((HEADS, HEAD_DIM, 2 * HEAD_DIM), F32),
            pltpu.VMEM((SUBLANES, LANES), F32),
            pltpu.VMEM((SUBLANES + CHUNK, 2 * MLSTM_WIDTH), F32),
        ],
        compiler_params=pltpu.CompilerParams(
            dimension_semantics=("arbitrary", "arbitrary"), vmem_limit_bytes=VMEM_LIMIT),
        name="proj_mlstm",
    )(x2, x2, g, *w_args, w_gcol, b_col, b_row, wqk, gn)


MIX_ROW_SPLIT = 2


def _mix_kernel(hml_ref, u_ref, x_ref, wpool_ref, pscale_ref, wo_ml_ref, wo_pool_ref, g_ref,
                gnext_ref, out_ref, hn_ref, ue_ref, *, tiles_per_seq):
    tm = u_ref.shape[0]
    t = pl.program_id(0) % tiles_per_seq

    @pl.when(t == 0)
    def _():
        ue_ref[0:POOL_HALO, :] = jnp.zeros((POOL_HALO, POOL_WIDTH), F32)

    u = u_ref[...]
    ue_ref[POOL_HALO:POOL_HALO + tm, :] = u
    pos = (lax.broadcasted_iota(jnp.int32, (tm, 1), 0) + (t * tm + 1)).astype(F32)
    mixed = []
    for g, window in enumerate(POOL_WINDOWS):
        sl = slice(g * POOL_GROUP_WIDTH, (g + 1) * POOL_GROUP_WIDTH)
        win = ue_ref[:, sl]
        span = 1
        while span < window:
            win = win + pltpu.roll(win, span, axis=0)
            span *= 2
        mean = win[POOL_HALO:, :] / jnp.minimum(pos, float(window))
        pooled = (mean - u[:, sl]).astype(BF16)
        mixed.append((jnp.dot(pooled, wpool_ref[g], preferred_element_type=F32)
                      * pscale_ref[:, sl]).astype(BF16))
    ue_ref[0:POOL_HALO, :] = u[tm - POOL_HALO:tm, :]
    hpool = jnp.concatenate(mixed, axis=1)
    rows = tm // MIX_ROW_SPLIT
    for r in range(MIX_ROW_SPLIT):
        rs = slice(r * rows, (r + 1) * rows)
        mix = (jnp.dot(hml_ref[rs, :], wo_ml_ref[...], preferred_element_type=F32)
               + jnp.dot(hpool[rs, :], wo_pool_ref[...], preferred_element_type=F32))
        h = x_ref[rs, :] + _rms(mix, g_ref[...])
        out_ref[rs, :] = h
        hn_ref[rs, :] = _rms(h, gnext_ref[...]).astype(BF16)


def _mix(hml, u, x2, w_pool, pool_scale, wo_ml, wo_pool, g_post, g_next, seq, tm):
    m, d = x2.shape
    const = lambda shape: pl.BlockSpec(shape, lambda i: (0,) * len(shape),
                                       pipeline_mode=pl.Buffered(1))
    return pl.pallas_call(
        functools.partial(_mix_kernel, tiles_per_seq=seq // tm),
        grid=(m // tm,),
        in_specs=[
            pl.BlockSpec((tm, MLSTM_WIDTH), lambda i: (i, 0)),
            pl.BlockSpec((tm, POOL_WIDTH), lambda i: (i, 0)),
            pl.BlockSpec((tm, d), lambda i: (i, 0)),
            const(w_pool.shape),
            const((1, POOL_WIDTH)),
            const(wo_ml.shape),
            const(wo_pool.shape),
            const((1, d)),
            const((1, d)),
        ],
        out_specs=[pl.BlockSpec((tm, d), lambda i: (i, 0)), pl.BlockSpec((tm, d), lambda i: (i, 0))],
        out_shape=[jax.ShapeDtypeStruct((m, d), F32), jax.ShapeDtypeStruct((m, d), BF16)],
        scratch_shapes=[pltpu.VMEM((POOL_HALO + tm, POOL_WIDTH), F32)],
        compiler_params=pltpu.CompilerParams(
            dimension_semantics=("arbitrary",), vmem_limit_bytes=VMEM_LIMIT),
        name="mix",
    )(hml, u, x2, w_pool, pool_scale, wo_ml, wo_pool, g_post, g_next)


FFN_ROW_SPLIT = 2


def _ffn_kernel(hn_ref, w1_ref, w2_ref, acc_ref):
    j = pl.program_id(1)
    rows = hn_ref.shape[0] // FFN_ROW_SPLIT
    groups = [slice(r * rows, (r + 1) * rows) for r in range(FFN_ROW_SPLIT)]
    w1 = w1_ref[...].astype(BF16)
    w2 = w2_ref[...].astype(BF16)
    acts = []
    for g in groups:
        a = jnp.dot(hn_ref[g, :], w1, preferred_element_type=F32)
        acts.append(jnp.square(jnp.maximum(a, 0.0)).astype(BF16))
    for g, a in zip(groups, acts):
        prev = jnp.where(j == 0, 0.0, acc_ref[g, :])
        acc_ref[g, :] = prev + jnp.dot(a, w2, preferred_element_type=F32)


def _ffn(hn, w1, w2, tm, tf):
    m, d = hn.shape
    f = w1.shape[1]
    return pl.pallas_call(
        _ffn_kernel,
        grid=(m // tm, f // tf),
        in_specs=[
            pl.BlockSpec((tm, d), lambda i, j: (i, 0)),
            pl.BlockSpec((d, tf), lambda i, j: (0, j)),
            pl.BlockSpec((tf, d), lambda i, j: (j, 0)),
        ],
        out_specs=pl.BlockSpec((tm, d), lambda i, j: (i, 0)),
        out_shape=jax.ShapeDtypeStruct((m, d), F32),
        compiler_params=pltpu.CompilerParams(
            dimension_semantics=("arbitrary", "arbitrary"), vmem_limit_bytes=VMEM_LIMIT),
        name="ffn",
    )(hn, w1, w2)


def _ple_kernel(h_ref, ff_ref, p_ref, gff_ref, ggate_ref, wgate_ref, wproj_ref, gpost_ref, out_ref,
                hn_ref):
    rows = h_ref.shape[0] // MIX_ROW_SPLIT
    groups = [slice(r * rows, (r + 1) * rows) for r in range(MIX_ROW_SPLIT)]
    for rs in groups:
        h = h_ref[rs, :] + _rms(ff_ref[rs, :], gff_ref[...])
        out_ref[rs, :] = h
        hn_ref[rs, :] = _rms(h, ggate_ref[...]).astype(BF16)
    for rs in groups:
        gate = _sigmoid(jnp.dot(hn_ref[rs, :], wgate_ref[...], preferred_element_type=F32))
        e = jnp.dot(p_ref[rs, :].astype(BF16), wproj_ref[...], preferred_element_type=F32)
        out_ref[rs, :] += _rms(e * gate, gpost_ref[...])


def _ple(h, ff, p2, g_ff_post, g_gate, w_gate, w_proj, g_post, tm):
    m, d = h.shape
    pd = p2.shape[1]
    const = lambda shape: pl.BlockSpec(shape, lambda i: (0,) * len(shape),
                                       pipeline_mode=pl.Buffered(1))
    return pl.pallas_call(
        _ple_kernel,
        grid=(m // tm,),
        in_specs=[
            pl.BlockSpec((tm, d), lambda i: (i, 0)),
            pl.BlockSpec((tm, d), lambda i: (i, 0)),
            pl.BlockSpec((tm, pd), lambda i: (i, 0)),
            const((1, d)),
            const((1, d)),
            const(w_gate.shape),
            const(w_proj.shape),
            const((1, d)),
        ],
        out_specs=pl.BlockSpec((tm, d), lambda i: (i, 0)),
        out_shape=jax.ShapeDtypeStruct((m, d), F32),
        scratch_shapes=[pltpu.VMEM((tm, d), BF16)],
        compiler_params=pltpu.CompilerParams(
            dimension_semantics=("arbitrary",), vmem_limit_bytes=VMEM_LIMIT),
        name="ple",
    )(h, ff, p2, g_ff_post, g_gate, w_gate, w_proj, g_post)


def _tile_qk_columns(w_q, w_k):
    r = w_q.shape[0]
    a = jnp.stack([w_q.reshape(r, COL_TILES, PAIR), w_k.reshape(r, COL_TILES, PAIR)], axis=2)
    return a.reshape(r, 2 * MLSTM_WIDTH)


def _layer(h, p2, seq, w_in, b_gates, w_qk_conv, g_mlstm, w_pool, pool_scale, w_out,
           g_mix_pre, g_mix_post, w_ff1, w_ff2, g_ff_pre, g_ff_post, w_ple_proj, w_ple_gate,
           g_ple_gate, g_ple_post):
    qkvo = 4 * MLSTM_WIDTH
    row = lambda v: v.reshape(1, -1).astype(F32)
    w_in16 = w_in.astype(BF16)
    w_u16 = w_in16[:, qkvo + 2 * HEADS:]
    w_gi = w_in[:, qkvo:qkvo + HEADS]
    w_gf = w_in[:, qkvo + HEADS:qkvo + 2 * HEADS]
    lane_pad = lambda w: jnp.pad(w, ((0, 0), (0, LANES - HEADS)))
    w_gcol = jnp.concatenate([lane_pad(w_gi), lane_pad(w_gf)], axis=1).astype(BF16)
    b_col = jnp.concatenate([lane_pad(b_gates[None, :HEADS]), lane_pad(b_gates[None, HEADS:])],
                            axis=1).astype(F32)
    b_row = b_gates.reshape(2 * HEADS, 1).astype(F32)
    wqk = _tile_qk_columns(w_qk_conv[:, :MLSTM_WIDTH], w_qk_conv[:, MLSTM_WIDTH:]).astype(F32)

    u, hml = _proj_mlstm(h, row(g_mix_pre), w_in16, w_u16, w_gcol, b_col, b_row, wqk,
                         row(g_mlstm), seq)
    w_out16 = w_out.astype(BF16)
    h, hn = _mix(hml, u, h, w_pool.astype(BF16), row(pool_scale), w_out16[:MLSTM_WIDTH],
                 w_out16[MLSTM_WIDTH:], row(g_mix_post), row(g_ff_pre), seq, tm=512)
    ff = _ffn(hn, w_ff1, w_ff2.astype(BF16), tm=1024, tf=1024)
    return _ple(h, ff, p2, row(g_ff_post), row(g_ple_gate), w_ple_gate.astype(BF16),
                w_ple_proj.astype(BF16), row(g_ple_post), tm=512)


def kernel(x, p, w_in, b_gates, w_qk_conv, g_mlstm, w_pool, pool_scale, w_out, g_mix_pre,
           g_mix_post, w_ff1, w_ff2, g_ff_pre, g_ff_post, w_ple_proj, w_ple_gate, g_ple_gate,
           g_ple_post):
    batch, seq, d = x.shape
    h = x.reshape(batch * seq, d)
    for i in range(p.shape[0]):
        h = _layer(h, p[i].reshape(batch * seq, -1), seq, w_in[i], b_gates[i],
                   w_qk_conv[i], g_mlstm[i], w_pool[i], pool_scale[i], w_out[i], g_mix_pre[i],
                   g_mix_post[i], w_ff1[i], w_ff2[i], g_ff_pre[i], g_ff_post[i], w_ple_proj[i],
                   w_ple_gate[i], g_ple_gate[i], g_ple_post[i])
    return h.reshape(batch, seq, d)
```

```python
import functools

import jax
import jax.numpy as jnp
from jax import lax
from jax.experimental import pallas as pl
from jax.experimental.pallas import tpu as pltpu

EPS = 1e-6
HEADS = 8
HEAD_DIM = 128
MLSTM_WIDTH = HEADS * HEAD_DIM
QK_CONV = 4
POOL_WINDOWS = (2, 4, 8, 16)
POOL_GROUP_WIDTH = 256
POOL_WIDTH = POOL_GROUP_WIDTH * len(POOL_WINDOWS)

LANES = 128
SUBLANES = 8
POOL_HALO = 16
CHUNK = 128
VMEM_LIMIT = 56 * 1024 * 1024

COL_TILES = 4
HEADS_PER_TILE = HEADS // COL_TILES
PAIR = HEADS_PER_TILE * HEAD_DIM
TILE_COLS = 4 * PAIR
ROW_TILE = COL_TILES * CHUNK
TILES_PER_STEP = 1

F32 = jnp.float32
BF16 = jnp.bfloat16


def _rms(x, g):
    return x * lax.rsqrt(jnp.mean(x * x, axis=-1, keepdims=True) + EPS) * g


def _log_sigmoid(x):
    return jnp.minimum(x, 0.0) - jnp.log1p(jnp.exp(-jnp.abs(x)))


def _sigmoid(x):
    return 1.0 / (1.0 + jnp.exp(-x))


def _gate_prep(gc, bc_ref):
    L = CHUNK
    gc = gc + bc_ref[...]
    li_c = gc[:, :LANES]
    lf_c = _log_sigmoid(gc[:, LANES:])
    row = lax.broadcasted_iota(jnp.int32, (L, L), 0)
    col = lax.broadcasted_iota(jnp.int32, (L, L), 1)
    tril = jnp.where(row >= col, 1.0, 0.0).astype(F32)
    b_c = jnp.dot(tril, lf_c, preferred_element_type=F32, precision=lax.Precision.HIGHEST)
    a_c = b_c[L - 1:L, :] - b_c + li_c
    r_r = (li_c - b_c).T[0:SUBLANES, :]
    return jnp.concatenate([b_c, a_c], axis=1), r_r


def _mlstm_chunk(first, proj_ref, gates_c, r_r, wqk_ref, gn_ref, out_ref, ct_ref, m_ref,
                 xe_ref, between):
    L = CHUNK
    reset = (lambda v: v) if first is None else (lambda v: jnp.where(first, 0.0, v))
    between(0)

    if first is not None:
        xe_ref[0:SUBLANES, :] = reset(xe_ref[0:SUBLANES, :])
    qk = []
    for t in range(COL_TILES):
        cs = slice(t * 2 * PAIR, (t + 1) * 2 * PAIR)
        x = proj_ref[t, :, 0:2 * PAIR]
        xe_ref[SUBLANES:SUBLANES + L, cs] = x
        acc = x * wqk_ref[QK_CONV - 1:QK_CONV, cs]
        for back in range(1, QK_CONV):
            tap = QK_CONV - 1 - back
            acc = acc + xe_ref[SUBLANES - back:SUBLANES - back + L, cs] * wqk_ref[tap:tap + 1, cs]
        xe_ref[0:SUBLANES, cs] = x[L - SUBLANES:L, :]
        qk.append(acc * _sigmoid(acc))

    b_c = gates_c[:, :LANES]
    a_c = gates_c[:, LANES:]
    row = lax.broadcasted_iota(jnp.int32, (L, L), 0)
    col = lax.broadcasted_iota(jnp.int32, (L, L), 1)
    causal = row >= col
    btot = b_c[L - 1:L, :]
    m_prev = reset(m_ref[0:1, :])
    m_new = jnp.maximum(btot + m_prev, jnp.max(a_c, axis=0, keepdims=True))
    decay = jnp.exp(btot + m_prev - m_new)
    w_c = jnp.exp(a_c - m_new)
    inter_log = b_c + m_prev
    m_ref[0:1, :] = m_new

    ones_col = jnp.where(lax.broadcasted_iota(jnp.int32, (L, LANES), 1) == 0, 1.0, 0.0).astype(F32)

    for h in range(HEADS):
        t, e = divmod(h, HEADS_PER_TILE)
        if e == 0:
            between(t + 1)
        es = slice(e * HEAD_DIM, (e + 1) * HEAD_DIM)
        sl = slice(h * HEAD_DIM, (h + 1) * HEAD_DIM)
        qf = qk[t][:, es]
        qh = qf.astype(BF16)
        kf = qk[t][:, PAIR + e * HEAD_DIM:PAIR + (e + 1) * HEAD_DIM] * (HEAD_DIM ** -0.5)
        kh = kf.astype(BF16)
        vh = proj_ref[t, :, 2 * PAIR + e * HEAD_DIM:2 * PAIR + (e + 1) * HEAD_DIM]
        oh = proj_ref[t, :, 3 * PAIR + e * HEAD_DIM:3 * PAIR + (e + 1) * HEAD_DIM]
        vaug = jnp.concatenate([vh, ones_col], axis=1)
        dlog = jnp.where(causal, b_c[:, h:h + 1] + r_r[h:h + 1, :], -jnp.inf)
        inter = inter_log[:, h:h + 1]
        m_t = jnp.maximum(inter, jnp.max(dlog, axis=1, keepdims=True))
        dw = jnp.exp(dlog - m_t)
        inter_w = jnp.exp(inter - m_t)
        s = lax.dot_general(qh, kh, (((1,), (1,)), ((), ())), preferred_element_type=F32) * dw
        ct = reset(ct_ref[h])
        lhs = jnp.concatenate([s.astype(BF16), (inter_w * qf).astype(BF16)], axis=1)
        rhs = jnp.concatenate([vaug.astype(BF16), ct.astype(BF16)], axis=0)
        tot = jnp.dot(lhs, rhs, preferred_element_type=F32)
        num = tot[:, :HEAD_DIM]
        den = tot[:, HEAD_DIM:HEAD_DIM + 1]
        cell = num / jnp.maximum(jnp.abs(den), jnp.exp(-m_t))
        cell = _rms(cell, gn_ref[:, sl])
        out_ref[:, sl] = (_sigmoid(oh) * cell).astype(out_ref.dtype)
        wv = (w_c[:, h:h + 1] * vaug).astype(BF16)
        upd = lax.dot_general(kh, wv, (((0,), (0,)), ((), ())), preferred_element_type=F32)
        ct_ref[h] = decay[:, h:h + 1] * ct + upd


def _proj_mlstm_kernel(x0_ref, xnext_ref, g_ref, *refs, tiles_per_seq):
    n_w = 5 * TILES_PER_STEP
    w_refs = refs[:n_w]
    (wgc_ref, bc_ref, wqk_ref, gn_ref, u_ref, hml_ref, xn_a, xn_b, proj_a, proj_b,
     gcol_a, gcol_b, grow_a, grow_b, ct_ref, m_ref, xe_ref) = refs[n_w:]
    i = pl.program_id(0)
    j = pl.program_id(1)

    @pl.when((i == 0) & (j == 0))
    def _():
        xn_a[...] = _rms(x0_ref[...], g_ref[...]).astype(BF16)
        proj_b[...] = jnp.zeros_like(proj_b)
        gcol_b[...] = jnp.zeros_like(gcol_b)
        grow_b[...] = jnp.zeros_like(grow_b)
        ct_ref[...] = jnp.zeros_like(ct_ref)
        m_ref[...] = jnp.zeros_like(m_ref)
        xe_ref[...] = jnp.zeros_like(xe_ref)

    def step(xn_cur, xn_nxt, proj_w, gcol_w, grow_w, proj_r, gcol_r, grow_r):
        tile_first = (i + tiles_per_seq - 1) % tiles_per_seq == 0
        for t in range(TILES_PER_STEP):
            c = j * TILES_PER_STEP + t
            r0 = pl.multiple_of(c * CHUNK, CHUNK)

            def project_piece(k, t=t, c=c):
                cs = slice(k * PAIR, (k + 1) * PAIR)
                y = jnp.dot(xn_cur[...], w_refs[k * TILES_PER_STEP + t][...],
                            preferred_element_type=F32)
                if k < 4:
                    proj_w[c, :, cs] = y
                else:
                    u_ref[:, t * POOL_GROUP_WIDTH:(t + 1) * POOL_GROUP_WIDTH] = y

            first = (tile_first & (j == 0)) if t == 0 else None
            _mlstm_chunk(first, proj_r.at[:, pl.ds(r0, CHUNK), :], gcol_r[pl.ds(r0, CHUNK), :],
                         grow_r[c], wqk_ref, gn_ref,
                         hml_ref.at[t * CHUNK:(t + 1) * CHUNK, :], ct_ref, m_ref, xe_ref,
                         project_piece)

            gc = jnp.dot(xn_cur[pl.ds(r0, CHUNK), :], wgc_ref[...], preferred_element_type=F32)
            gcol_w[pl.ds(r0, CHUNK), :], grow_w[c] = _gate_prep(gc, bc_ref)

        rows = TILES_PER_STEP * CHUNK
        rn = pl.multiple_of(j * rows, rows)
        xn_nxt[pl.ds(rn, rows), :] = _rms(xnext_ref[...], g_ref[...]).astype(BF16)

    @pl.when(i % 2 == 0)
    def _():
        step(xn_a, xn_b, proj_a, gcol_a, grow_a, proj_b, gcol_b, grow_b)

    @pl.when(i % 2 == 1)
    def _():
        step(xn_b, xn_a, proj_b, gcol_b, grow_b, proj_a, gcol_a, grow_a)


def _proj_mlstm(x2, g, w_in16, w_u16, w_gcol, b_col, wqk, gn, seq):
    m, d = x2.shape
    nt = m // ROW_TILE
    steps = COL_TILES // TILES_PER_STEP
    rows = TILES_PER_STEP * CHUNK
    last_block = m // rows - 1
    const = lambda shape: pl.BlockSpec(shape, lambda i, j: (0,) * len(shape),
                                       pipeline_mode=pl.Buffered(1))
    slab = lambda s, t: pl.BlockSpec(
        (d, PAIR), lambda i, j: (0, s * COL_TILES + j * TILES_PER_STEP + t))
    u_slab = lambda t: pl.BlockSpec((d, POOL_GROUP_WIDTH), lambda i, j: (0, j * TILES_PER_STEP + t))
    tiles = range(TILES_PER_STEP)
    w_specs = [slab(s, t) for s in range(4) for t in tiles] + [u_slab(t) for t in tiles]
    w_args = [w_in16] * (4 * TILES_PER_STEP) + [w_u16] * TILES_PER_STEP
    return pl.pallas_call(
        functools.partial(_proj_mlstm_kernel, tiles_per_seq=seq // ROW_TILE),
        grid=(nt + 1, steps),
        in_specs=[
            const((ROW_TILE, d)),
            pl.BlockSpec((rows, d),
                         lambda i, j: (jnp.minimum((i + 1) * steps + j, last_block), 0)),
            const((1, d)),
            *w_specs,
            const((d, 2 * LANES)),
            const((1, 2 * LANES)),
            const((QK_CONV, 2 * MLSTM_WIDTH)),
            const((1, MLSTM_WIDTH)),
        ],
        out_specs=[
            pl.BlockSpec((ROW_TILE, TILES_PER_STEP * POOL_GROUP_WIDTH), lambda i, j: (i, j)),
            pl.BlockSpec((rows, MLSTM_WIDTH),
                         lambda i, j: (jnp.maximum((i - 1) * steps + j, 0), 0)),
        ],
        out_shape=[
            jax.ShapeDtypeStruct((m + ROW_TILE, POOL_WIDTH), F32),
            jax.ShapeDtypeStruct((m, MLSTM_WIDTH), BF16),
        ],
        scratch_shapes=[
            pltpu.VMEM((ROW_TILE, d), BF16),
            pltpu.VMEM((ROW_TILE, d), BF16),
            pltpu.VMEM((COL_TILES, ROW_TILE, TILE_COLS), F32),
            pltpu.VMEM((COL_TILES, ROW_TILE, TILE_COLS), F32),
            pltpu.VMEM((ROW_TILE, 2 * LANES), F32),
            pltpu.VMEM((ROW_TILE, 2 * LANES), F32),
            pltpu.VMEM((COL_TILES, SUBLANES, CHUNK), F32),
            pltpu.VMEM((COL_TILES, SUBLANES, CHUNK), F32),
            pltpu.VMEM((HEADS, HEAD_DIM, 2 * HEAD_DIM), F32),
            pltpu.VMEM((SUBLANES, LANES), F32),
            pltpu.VMEM((SUBLANES + CHUNK, 2 * MLSTM_WIDTH), F32),
        ],
        compiler_params=pltpu.CompilerParams(
            dimension_semantics=("arbitrary", "arbitrary"), vmem_limit_bytes=VMEM_LIMIT),
        name="proj_mlstm",
    )(x2, x2, g, *w_args, w_gcol, b_col, wqk, gn)


MIX_ROW_SPLIT = 2


def _mix_kernel(hml_ref, u_ref, x_ref, wpool_ref, pscale_ref, wo_ml_ref, wo_pool_ref, g_ref,
                gnext_ref, out_ref, hn_ref, ue_ref, *, tiles_per_seq):
    tm = u_ref.shape[0]
    t = pl.program_id(0) % tiles_per_seq

    @pl.when(t == 0)
    def _():
        ue_ref[0:POOL_HALO, :] = jnp.zeros((POOL_HALO, POOL_WIDTH), F32)

    u = u_ref[...]
    ue_ref[POOL_HALO:POOL_HALO + tm, :] = u
    pos = (lax.broadcasted_iota(jnp.int32, (tm, 1), 0) + (t * tm + 1)).astype(F32)
    mixed = []
    for g, window in enumerate(POOL_WINDOWS):
        sl = slice(g * POOL_GROUP_WIDTH, (g + 1) * POOL_GROUP_WIDTH)
        win = ue_ref[:, sl]
        span = 1
        while span < window:
            win = win + pltpu.roll(win, span, axis=0)
            span *= 2
        mean = win[POOL_HALO:, :] / jnp.minimum(pos, float(window))
        pooled = (mean - u[:, sl]).astype(BF16)
        mixed.append((jnp.dot(pooled, wpool_ref[g], preferred_element_type=F32)
                      * pscale_ref[:, sl]).astype(BF16))
    ue_ref[0:POOL_HALO, :] = u[tm - POOL_HALO:tm, :]
    hpool = jnp.concatenate(mixed, axis=1)
    rows = tm // MIX_ROW_SPLIT
    for r in range(MIX_ROW_SPLIT):
        rs = slice(r * rows, (r + 1) * rows)
        mix = (jnp.dot(hml_ref[rs, :], wo_ml_ref[...], preferred_element_type=F32)
               + jnp.dot(hpool[rs, :], wo_pool_ref[...], preferred_element_type=F32))
        h = x_ref[rs, :] + _rms(mix, g_ref[...])
        out_ref[rs, :] = h
        hn_ref[rs, :] = _rms(h, gnext_ref[...]).astype(BF16)


def _mix(hml, u, x2, w_pool, pool_scale, wo_ml, wo_pool, g_post, g_next, seq, tm):
    m, d = x2.shape
    const = lambda shape: pl.BlockSpec(shape, lambda i: (0,) * len(shape),
                                       pipeline_mode=pl.Buffered(1))
    return pl.pallas_call(
        functools.partial(_mix_kernel, tiles_per_seq=seq // tm),
        grid=(m // tm,),
        in_specs=[
            pl.BlockSpec((tm, MLSTM_WIDTH), lambda i: (i, 0)),
            pl.BlockSpec((tm, POOL_WIDTH), lambda i: (i, 0)),
            pl.BlockSpec((tm, d), lambda i: (i, 0)),
            const(w_pool.shape),
            const((1, POOL_WIDTH)),
            const(wo_ml.shape),
            const(wo_pool.shape),
            const((1, d)),
            const((1, d)),
        ],
        out_specs=[pl.BlockSpec((tm, d), lambda i: (i, 0)), pl.BlockSpec((tm, d), lambda i: (i, 0))],
        out_shape=[jax.ShapeDtypeStruct((m, d), F32), jax.ShapeDtypeStruct((m, d), BF16)],
        scratch_shapes=[pltpu.VMEM((POOL_HALO + tm, POOL_WIDTH), F32)],
        compiler_params=pltpu.CompilerParams(
            dimension_semantics=("arbitrary",), vmem_limit_bytes=VMEM_LIMIT),
        name="mix",
    )(hml, u, x2, w_pool, pool_scale, wo_ml, wo_pool, g_post, g_next)


FFN_ROW_SPLIT = 2


def _ffn_kernel(hn_ref, w1_ref, w2_ref, acc_ref):
    j = pl.program_id(1)
    rows = hn_ref.shape[0] // FFN_ROW_SPLIT
    groups = [slice(r * rows, (r + 1) * rows) for r in range(FFN_ROW_SPLIT)]
    w1 = w1_ref[...].astype(BF16)
    w2 = w2_ref[...].astype(BF16)
    acts = []
    for g in groups:
        a = jnp.dot(hn_ref[g, :], w1, preferred_element_type=F32)
        acts.append(jnp.square(jnp.maximum(a, 0.0)).astype(BF16))
    for g, a in zip(groups, acts):
        prev = jnp.where(j == 0, 0.0, acc_ref[g, :])
        acc_ref[g, :] = prev + jnp.dot(a, w2, preferred_element_type=F32)


def _ffn(hn, w1, w2, tm, tf):
    m, d = hn.shape
    f = w1.shape[1]
    return pl.pallas_call(
        _ffn_kernel,
        grid=(m // tm, f // tf),
        in_specs=[
            pl.BlockSpec((tm, d), lambda i, j: (i, 0)),
            pl.BlockSpec((d, tf), lambda i, j: (0, j)),
            pl.BlockSpec((tf, d), lambda i, j: (j, 0)),
        ],
        out_specs=pl.BlockSpec((tm, d), lambda i, j: (i, 0)),
        out_shape=jax.ShapeDtypeStruct((m, d), F32),
        compiler_params=pltpu.CompilerParams(
            dimension_semantics=("arbitrary", "arbitrary"), vmem_limit_bytes=VMEM_LIMIT),
        name="ffn",
    )(hn, w1, w2)


def _ple_kernel(h_ref, ff_ref, p_ref, gff_ref, ggate_ref, wgate_ref, wproj_ref, gpost_ref, out_ref,
                hn_ref):
    rows = h_ref.shape[0] // MIX_ROW_SPLIT
    groups = [slice(r * rows, (r + 1) * rows) for r in range(MIX_ROW_SPLIT)]
    for rs in groups:
        h = h_ref[rs, :] + _rms(ff_ref[rs, :], gff_ref[...])
        out_ref[rs, :] = h
        hn_ref[rs, :] = _rms(h, ggate_ref[...]).astype(BF16)
    for rs in groups:
        gate = _sigmoid(jnp.dot(hn_ref[rs, :], wgate_ref[...], preferred_element_type=F32))
        e = jnp.dot(p_ref[rs, :].astype(BF16), wproj_ref[...], preferred_element_type=F32)
        out_ref[rs, :] += _rms(e * gate, gpost_ref[...])


def _ple(h, ff, p2, g_ff_post, g_gate, w_gate, w_proj, g_post, tm):
    m, d = h.shape
    pd = p2.shape[1]
    const = lambda shape: pl.BlockSpec(shape, lambda i: (0,) * len(shape),
                                       pipeline_mode=pl.Buffered(1))
    return pl.pallas_call(
        _ple_kernel,
        grid=(m // tm,),
        in_specs=[
            pl.BlockSpec((tm, d), lambda i: (i, 0)),
            pl.BlockSpec((tm, d), lambda i: (i, 0)),
            pl.BlockSpec((tm, pd), lambda i: (i, 0)),
            const((1, d)),
            const((1, d)),
            const(w_gate.shape),
            const(w_proj.shape),
            const((1, d)),
        ],
        out_specs=pl.BlockSpec((tm, d), lambda i: (i, 0)),
        out_shape=jax.ShapeDtypeStruct((m, d), F32),
        scratch_shapes=[pltpu.VMEM((tm, d), BF16)],
        compiler_params=pltpu.CompilerParams(
            dimension_semantics=("arbitrary",), vmem_limit_bytes=VMEM_LIMIT),
        name="ple",
    )(h, ff, p2, g_ff_post, g_gate, w_gate, w_proj, g_post)


def _tile_qk_columns(w_q, w_k):
    r = w_q.shape[0]
    a = jnp.stack([w_q.reshape(r, COL_TILES, PAIR), w_k.reshape(r, COL_TILES, PAIR)], axis=2)
    return a.reshape(r, 2 * MLSTM_WIDTH)


def _layer(h, p2, seq, w_in, b_gates, w_qk_conv, g_mlstm, w_pool, pool_scale, w_out,
           g_mix_pre, g_mix_post, w_ff1, w_ff2, g_ff_pre, g_ff_post, w_ple_proj, w_ple_gate,
           g_ple_gate, g_ple_post):
    qkvo = 4 * MLSTM_WIDTH
    row = lambda v: v.reshape(1, -1).astype(F32)
    w_in16 = w_in.astype(BF16)
    w_u16 = w_in16[:, qkvo + 2 * HEADS:]
    w_gi = w_in[:, qkvo:qkvo + HEADS]
    w_gf = w_in[:, qkvo + HEADS:qkvo + 2 * HEADS]
    lane_pad = lambda w: jnp.pad(w, ((0, 0), (0, LANES - HEADS)))
    w_gcol = jnp.concatenate([lane_pad(w_gi), lane_pad(w_gf)], axis=1).astype(BF16)
    b_col = jnp.concatenate([lane_pad(b_gates[None, :HEADS]), lane_pad(b_gates[None, HEADS:])],
                            axis=1).astype(F32)
    wqk = _tile_qk_columns(w_qk_conv[:, :MLSTM_WIDTH], w_qk_conv[:, MLSTM_WIDTH:]).astype(F32)

    u, hml = _proj_mlstm(h, row(g_mix_pre), w_in16, w_u16, w_gcol, b_col, wqk,
                         row(g_mlstm), seq)
    w_out16 = w_out.astype(BF16)
    h, hn = _mix(hml, u, h, w_pool.astype(BF16), row(pool_scale), w_out16[:MLSTM_WIDTH],
                 w_out16[MLSTM_WIDTH:], row(g_mix_post), row(g_ff_pre), seq, tm=512)
    ff = _ffn(hn, w_ff1, w_ff2.astype(BF16), tm=1024, tf=1024)
    return _ple(h, ff, p2, row(g_ff_post), row(g_ple_gate), w_ple_gate.astype(BF16),
                w_ple_proj.astype(BF16), row(g_ple_post), tm=512)


def kernel(x, p, w_in, b_gates, w_qk_conv, g_mlstm, w_pool, pool_scale, w_out, g_mix_pre,
           g_mix_post, w_ff1, w_ff2, g_ff_pre, g_ff_post, w_ple_proj, w_ple_gate, g_ple_gate,
           g_ple_post):
    batch, seq, d = x.shape
    h = x.reshape(batch * seq, d)
    for i in range(p.shape[0]):
        h = _layer(h, p[i].reshape(batch * seq, -1), seq, w_in[i], b_gates[i],
                   w_qk_conv[i], g_mlstm[i], w_pool[i], pool_scale[i], w_out[i], g_mix_pre[i],
                   g_mix_post[i], w_ff1[i], w_ff2[i], g_ff_pre[i], g_ff_post[i], w_ple_proj[i],
                   w_ple_gate[i], g_ple_gate[i], g_ple_post[i])
    return h.reshape(batch, seq, d)
```

```python
import functools

import jax
import jax.numpy as jnp
from jax import lax
from jax.experimental import pallas as pl
from jax.experimental.pallas import tpu as pltpu

EPS = 1e-6
HEADS = 8
HEAD_DIM = 128
MLSTM_WIDTH = HEADS * HEAD_DIM
QK_CONV = 4
POOL_WINDOWS = (2, 4, 8, 16)
POOL_GROUP_WIDTH = 256
POOL_WIDTH = POOL_GROUP_WIDTH * len(POOL_WINDOWS)

LANES = 128
SUBLANES = 8
POOL_HALO = 16
CHUNK = 128
VMEM_LIMIT = 56 * 1024 * 1024

COL_TILES = 4
HEADS_PER_TILE = HEADS // COL_TILES
PAIR = HEADS_PER_TILE * HEAD_DIM
TILE_COLS = 4 * PAIR
ROW_TILE = COL_TILES * CHUNK

MIX_TM = 512
PLE_TM = 512
FFN_TM = 1024
FFN_TF = 1024
ROW_SPLIT = 2

F32 = jnp.float32
BF16 = jnp.bfloat16


def _rms(x, g):
    return x * lax.rsqrt(jnp.mean(x * x, axis=-1, keepdims=True) + EPS) * g


def _log_sigmoid(x):
    return jnp.minimum(x, 0.0) - jnp.log1p(jnp.exp(-jnp.abs(x)))


def _sigmoid(x):
    return 1.0 / (1.0 + jnp.exp(-x))


def _row_groups(n):
    rows = n // ROW_SPLIT
    return [slice(r * rows, (r + 1) * rows) for r in range(ROW_SPLIT)]


def _mlstm_chunk(first, proj_ref, gc, gr, bc_ref, br_ref, wqk_ref, gn_ref, out_ref, ct_ref, m_ref,
                 xe_ref, between):
    L = CHUNK
    reset = lambda v: jnp.where(first, 0.0, v)
    between(0)

    xe_ref[0:SUBLANES, :] = reset(xe_ref[0:SUBLANES, :])
    qk = []
    for t in range(COL_TILES):
        cs = slice(t * 2 * PAIR, (t + 1) * 2 * PAIR)
        x = proj_ref[t, :, 0:2 * PAIR]
        xe_ref[SUBLANES:SUBLANES + L, cs] = x
        acc = x * wqk_ref[QK_CONV - 1:QK_CONV, cs]
        for back in range(1, QK_CONV):
            tap = QK_CONV - 1 - back
            acc = acc + xe_ref[SUBLANES - back:SUBLANES - back + L, cs] * wqk_ref[tap:tap + 1, cs]
        xe_ref[0:SUBLANES, cs] = x[L - SUBLANES:L, :]
        qk.append(acc * _sigmoid(acc))

    gc = gc + bc_ref[...]
    li_c = gc[:, :LANES]
    lf_c = _log_sigmoid(gc[:, LANES:])
    gr = gr + br_ref[...]
    li_r = gr[0:SUBLANES, :]
    lf_r = _log_sigmoid(gr[SUBLANES:, :])
    row = lax.broadcasted_iota(jnp.int32, (L, L), 0)
    col = lax.broadcasted_iota(jnp.int32, (L, L), 1)
    causal = row >= col
    tril = jnp.where(causal, 1.0, 0.0).astype(F32)
    triu = jnp.where(row <= col, 1.0, 0.0).astype(F32)
    b_c = jnp.dot(tril, lf_c, preferred_element_type=F32, precision=lax.Precision.HIGHEST)
    b_r = jnp.dot(lf_r, triu, preferred_element_type=F32, precision=lax.Precision.HIGHEST)
    r_r = li_r - b_r
    btot = b_c[L - 1:L, :]
    a_c = btot - b_c + li_c
    m_prev = reset(m_ref[0:1, :])
    m_new = jnp.maximum(btot + m_prev, jnp.max(a_c, axis=0, keepdims=True))
    decay = jnp.exp(btot + m_prev - m_new)
    w_c = jnp.exp(a_c - m_new)
    inter_log = b_c + m_prev
    m_ref[0:1, :] = m_new

    ones_col = jnp.where(lax.broadcasted_iota(jnp.int32, (L, LANES), 1) == 0, 1.0, 0.0).astype(F32)

    for h in range(HEADS):
        t, e = divmod(h, HEADS_PER_TILE)
        if e == 0:
            between(t + 1)
        es = slice(e * HEAD_DIM, (e + 1) * HEAD_DIM)
        sl = slice(h * HEAD_DIM, (h + 1) * HEAD_DIM)
        qf = qk[t][:, es]
        qh = qf.astype(BF16)
        kh = (qk[t][:, PAIR + e * HEAD_DIM:PAIR + (e + 1) * HEAD_DIM] * (HEAD_DIM ** -0.5)).astype(BF16)
        vh = proj_ref[t, :, 2 * PAIR + e * HEAD_DIM:2 * PAIR + (e + 1) * HEAD_DIM]
        oh = proj_ref[t, :, 3 * PAIR + e * HEAD_DIM:3 * PAIR + (e + 1) * HEAD_DIM]
        vaug = jnp.concatenate([vh, ones_col], axis=1)
        dlog = jnp.where(causal, b_c[:, h:h + 1] + r_r[h:h + 1, :], -jnp.inf)
        inter = inter_log[:, h:h + 1]
        m_t = jnp.maximum(inter, jnp.max(dlog, axis=1, keepdims=True))
        dw = jnp.exp(dlog - m_t)
        inter_w = jnp.exp(inter - m_t)
        s = lax.dot_general(qh, kh, (((1,), (1,)), ((), ())), preferred_element_type=F32) * dw
        ct = reset(ct_ref[h])
        lhs = jnp.concatenate([s.astype(BF16), (inter_w * qf).astype(BF16)], axis=1)
        rhs = jnp.concatenate([vaug.astype(BF16), ct.astype(BF16)], axis=0)
        tot = jnp.dot(lhs, rhs, preferred_element_type=F32)
        num = tot[:, :HEAD_DIM]
        den = tot[:, HEAD_DIM:HEAD_DIM + 1]
        cell = num / jnp.maximum(jnp.abs(den), jnp.exp(-m_t))
        cell = _rms(cell, gn_ref[:, sl])
        out_ref[:, sl] = (_sigmoid(oh) * cell).astype(out_ref.dtype)
        wv = (w_c[:, h:h + 1] * vaug).astype(BF16)
        upd = lax.dot_general(kh, wv, (((0,), (0,)), ((), ())), preferred_element_type=F32)
        ct_ref[h] = decay[:, h:h + 1] * ct + upd


def _proj_mlstm_kernel(x0_ref, xnext_ref, g_ref, wq_ref, wk_ref, wv_ref, wo_ref, wu_ref, wgc_ref,
                       bc_ref, br_ref, wqk_ref, gn_ref,
                       u_ref, hml_ref, xn_a, xn_b, proj_a, proj_b, gcol_a, gcol_b, grow_a, grow_b,
                       ct_ref, m_ref, xe_ref, *, tiles_per_seq):
    w_refs = (wq_ref, wk_ref, wv_ref, wo_ref, wu_ref)
    i = pl.program_id(0)
    j = pl.program_id(1)

    @pl.when((i == 0) & (j == 0))
    def _():
        xn_a[...] = _rms(x0_ref[...], g_ref[...]).astype(BF16)
        proj_b[...] = jnp.zeros_like(proj_b)
        gcol_b[...] = jnp.zeros_like(gcol_b)
        grow_b[...] = jnp.zeros_like(grow_b)
        ct_ref[...] = jnp.zeros_like(ct_ref)
        m_ref[...] = jnp.zeros_like(m_ref)
        xe_ref[...] = jnp.zeros_like(xe_ref)

    def step(xn_cur, xn_nxt, proj_w, gcol_w, grow_w, proj_r, gcol_r, grow_r):
        r0 = pl.multiple_of(j * CHUNK, CHUNK)

        def project_piece(k):
            y = jnp.dot(xn_cur[...], w_refs[k][...], preferred_element_type=F32)
            if k < 4:
                proj_w[j, :, k * PAIR:(k + 1) * PAIR] = y
            else:
                u_ref[...] = y

        first = ((i + tiles_per_seq - 1) % tiles_per_seq == 0) & (j == 0)
        _mlstm_chunk(first, proj_r.at[:, pl.ds(r0, CHUNK), :], gcol_r[pl.ds(r0, CHUNK), :],
                     grow_r[j], bc_ref, br_ref, wqk_ref, gn_ref, hml_ref, ct_ref, m_ref, xe_ref,
                     project_piece)

        gc = jnp.dot(xn_cur[pl.ds(r0, CHUNK), :], wgc_ref[...], preferred_element_type=F32)
        gcol_w[pl.ds(r0, CHUNK), :] = gc
        grow_w[j] = jnp.concatenate(
            [gc[:, :LANES].T[0:SUBLANES, :], gc[:, LANES:].T[0:SUBLANES, :]], axis=0)

        xn_nxt[pl.ds(r0, CHUNK), :] = _rms(xnext_ref[...], g_ref[...]).astype(BF16)

    @pl.when(i % 2 == 0)
    def _():
        step(xn_a, xn_b, proj_a, gcol_a, grow_a, proj_b, gcol_b, grow_b)

    @pl.when(i % 2 == 1)
    def _():
        step(xn_b, xn_a, proj_b, gcol_b, grow_b, proj_a, gcol_a, grow_a)


def _proj_mlstm(x2, g, w_in16, w_u16, w_gcol, b_col, b_row, wqk, gn, seq):
    m, d = x2.shape
    nt = m // ROW_TILE
    last_chunk = m // CHUNK - 1
    const = lambda shape: pl.BlockSpec(shape, lambda i, j: (0,) * len(shape),
                                       pipeline_mode=pl.Buffered(1))
    slab = lambda s: pl.BlockSpec((d, PAIR), lambda i, j: (0, s * COL_TILES + j))
    return pl.pallas_call(
        functools.partial(_proj_mlstm_kernel, tiles_per_seq=seq // ROW_TILE),
        grid=(nt + 1, COL_TILES),
        in_specs=[
            const((ROW_TILE, d)),
            pl.BlockSpec((CHUNK, d),
                         lambda i, j: (jnp.minimum((i + 1) * COL_TILES + j, last_chunk), 0)),
            const((1, d)),
            slab(0), slab(1), slab(2), slab(3),
            pl.BlockSpec((d, POOL_GROUP_WIDTH), lambda i, j: (0, j)),
            const((d, 2 * LANES)),
            const((1, 2 * LANES)),
            const((2 * SUBLANES, 1)),
            const((QK_CONV, 2 * MLSTM_WIDTH)),
            const((1, MLSTM_WIDTH)),
        ],
        out_specs=[
            pl.BlockSpec((ROW_TILE, POOL_GROUP_WIDTH), lambda i, j: (i, j)),
            pl.BlockSpec((CHUNK, MLSTM_WIDTH),
                         lambda i, j: (jnp.maximum((i - 1) * COL_TILES + j, 0), 0)),
        ],
        out_shape=[
            jax.ShapeDtypeStruct((m + ROW_TILE, POOL_WIDTH), F32),
            jax.ShapeDtypeStruct((m, MLSTM_WIDTH), BF16),
        ],
        scratch_shapes=[
            pltpu.VMEM((ROW_TILE, d), BF16),
            pltpu.VMEM((ROW_TILE, d), BF16),
            pltpu.VMEM((COL_TILES, ROW_TILE, TILE_COLS), F32),
            pltpu.VMEM((COL_TILES, ROW_TILE, TILE_COLS), F32),
            pltpu.VMEM((ROW_TILE, 2 * LANES), F32),
            pltpu.VMEM((ROW_TILE, 2 * LANES), F32),
            pltpu.VMEM((COL_TILES, 2 * SUBLANES, CHUNK), F32),
            pltpu.VMEM((COL_TILES, 2 * SUBLANES, CHUNK), F32),
            pltpu.VMEM((HEADS, HEAD_DIM, 2 * HEAD_DIM), F32),
            pltpu.VMEM((SUBLANES, LANES), F32),
            pltpu.VMEM((SUBLANES + CHUNK, 2 * MLSTM_WIDTH), F32),
        ],
        compiler_params=pltpu.CompilerParams(
            dimension_semantics=("arbitrary", "arbitrary"), vmem_limit_bytes=VMEM_LIMIT),
        name="proj_mlstm",
    )(x2, x2, g, w_in16, w_in16, w_in16, w_in16, w_u16, w_gcol, b_col, b_row, wqk, gn)


def _mix_kernel(hml_ref, u_ref, x_ref, wpool_ref, pscale_ref, wo_ml_ref, wo_pool_ref, g_ref,
                gnext_ref, out_ref, hn_ref, ue_ref, *, tiles_per_seq):
    tm = u_ref.shape[0]
    t = pl.program_id(0) % tiles_per_seq

    @pl.when(t == 0)
    def _():
        ue_ref[0:POOL_HALO, :] = jnp.zeros((POOL_HALO, POOL_WIDTH), F32)

    u = u_ref[...]
    ue_ref[POOL_HALO:POOL_HALO + tm, :] = u
    pos = (lax.broadcasted_iota(jnp.int32, (tm, 1), 0) + (t * tm + 1)).astype(F32)
    mixed = []
    for g, window in enumerate(POOL_WINDOWS):
        sl = slice(g * POOL_GROUP_WIDTH, (g + 1) * POOL_GROUP_WIDTH)
        win = ue_ref[:, sl]
        span = 1
        while span < window:
            win = win + pltpu.roll(win, span, axis=0)
            span *= 2
        mean = win[POOL_HALO:, :] / jnp.minimum(pos, float(window))
        pooled = (mean - u[:, sl]).astype(BF16)
        mixed.append((jnp.dot(pooled, wpool_ref[g], preferred_element_type=F32)
                      * pscale_ref[:, sl]).astype(BF16))
    ue_ref[0:POOL_HALO, :] = u[tm - POOL_HALO:tm, :]
    hpool = jnp.concatenate(mixed, axis=1)
    for rs in _row_groups(tm):
        mix = (jnp.dot(hml_ref[rs, :], wo_ml_ref[...], preferred_element_type=F32)
               + jnp.dot(hpool[rs, :], wo_pool_ref[...], preferred_element_type=F32))
        h = x_ref[rs, :] + _rms(mix, g_ref[...])
        out_ref[rs, :] = h
        hn_ref[rs, :] = _rms(h, gnext_ref[...]).astype(BF16)


def _mix(hml, u, x2, w_pool, pool_scale, wo_ml, wo_pool, g_post, g_next, seq):
    m, d = x2.shape
    tm = MIX_TM
    const = lambda shape: pl.BlockSpec(shape, lambda i: (0,) * len(shape),
                                       pipeline_mode=pl.Buffered(1))
    return pl.pallas_call(
        functools.partial(_mix_kernel, tiles_per_seq=seq // tm),
        grid=(m // tm,),
        in_specs=[
            pl.BlockSpec((tm, MLSTM_WIDTH), lambda i: (i, 0)),
            pl.BlockSpec((tm, POOL_WIDTH), lambda i: (i, 0)),
            pl.BlockSpec((tm, d), lambda i: (i, 0)),
            const(w_pool.shape),
            const((1, POOL_WIDTH)),
            const(wo_ml.shape),
            const(wo_pool.shape),
            const((1, d)),
            const((1, d)),
        ],
        out_specs=[pl.BlockSpec((tm, d), lambda i: (i, 0)), pl.BlockSpec((tm, d), lambda i: (i, 0))],
        out_shape=[jax.ShapeDtypeStruct((m, d), F32), jax.ShapeDtypeStruct((m, d), BF16)],
        scratch_shapes=[pltpu.VMEM((POOL_HALO + tm, POOL_WIDTH), F32)],
        compiler_params=pltpu.CompilerParams(
            dimension_semantics=("arbitrary",), vmem_limit_bytes=VMEM_LIMIT),
        name="mix",
    )(hml, u, x2, w_pool, pool_scale, wo_ml, wo_pool, g_post, g_next)


def _ffn_kernel(hn_ref, w1_ref, w2_ref, acc_ref):
    j = pl.program_id(1)
    groups = _row_groups(hn_ref.shape[0])
    w1 = w1_ref[...].astype(BF16)
    acts = []
    for g in groups:
        a = jnp.dot(hn_ref[g, :], w1, preferred_element_type=F32)
        acts.append(jnp.square(jnp.maximum(a, 0.0)).astype(BF16))
    for g, a in zip(groups, acts):
        prev = jnp.where(j == 0, 0.0, acc_ref[g, :])
        acc_ref[g, :] = prev + jnp.dot(a, w2_ref[...], preferred_element_type=F32)


def _ffn(hn, w1, w2):
    m, d = hn.shape
    f = w1.shape[1]
    return pl.pallas_call(
        _ffn_kernel,
        grid=(m // FFN_TM, f // FFN_TF),
        in_specs=[
            pl.BlockSpec((FFN_TM, d), lambda i, j: (i, 0)),
            pl.BlockSpec((d, FFN_TF), lambda i, j: (0, j)),
            pl.BlockSpec((FFN_TF, d), lambda i, j: (j, 0)),
        ],
        out_specs=pl.BlockSpec((FFN_TM, d), lambda i, j: (i, 0)),
        out_shape=jax.ShapeDtypeStruct((m, d), F32),
        compiler_params=pltpu.CompilerParams(
            dimension_semantics=("arbitrary", "arbitrary"), vmem_limit_bytes=VMEM_LIMIT),
        name="ffn",
    )(hn, w1, w2)


def _ple_kernel(h_ref, ff_ref, p_ref, gff_ref, ggate_ref, wgate_ref, wproj_ref, gpost_ref, out_ref,
                hn_ref):
    groups = _row_groups(h_ref.shape[0])
    for rs in groups:
        h = h_ref[rs, :] + _rms(ff_ref[rs, :], gff_ref[...])
        out_ref[rs, :] = h
        hn_ref[rs, :] = _rms(h, ggate_ref[...]).astype(BF16)
    for rs in groups:
        gate = _sigmoid(jnp.dot(hn_ref[rs, :], wgate_ref[...], preferred_element_type=F32))
        e = jnp.dot(p_ref[rs, :].astype(BF16), wproj_ref[...], preferred_element_type=F32)
        out_ref[rs, :] += _rms(e * gate, gpost_ref[...])


def _ple(h, ff, p2, g_ff_post, g_gate, w_gate, w_proj, g_post):
    m, d = h.shape
    pd = p2.shape[1]
    tm = PLE_TM
    const = lambda shape: pl.BlockSpec(shape, lambda i: (0,) * len(shape),
                                       pipeline_mode=pl.Buffered(1))
    return pl.pallas_call(
        _ple_kernel,
        grid=(m // tm,),
        in_specs=[
            pl.BlockSpec((tm, d), lambda i: (i, 0)),
            pl.BlockSpec((tm, d), lambda i: (i, 0)),
            pl.BlockSpec((tm, pd), lambda i: (i, 0)),
            const((1, d)),
            const((1, d)),
            const(w_gate.shape),
            const(w_proj.shape),
            const((1, d)),
        ],
        out_specs=pl.BlockSpec((tm, d), lambda i: (i, 0)),
        out_shape=jax.ShapeDtypeStruct((m, d), F32),
        scratch_shapes=[pltpu.VMEM((tm, d), BF16)],
        compiler_params=pltpu.CompilerParams(
            dimension_semantics=("arbitrary",), vmem_limit_bytes=VMEM_LIMIT),
        name="ple",
    )(h, ff, p2, g_ff_post, g_gate, w_gate, w_proj, g_post)


def _tile_qk_columns(w_q, w_k):
    r = w_q.shape[0]
    a = jnp.stack([w_q.reshape(r, COL_TILES, PAIR), w_k.reshape(r, COL_TILES, PAIR)], axis=2)
    return a.reshape(r, 2 * MLSTM_WIDTH)


def _layer(h, p2, seq, w_in, b_gates, w_qk_conv, g_mlstm, w_pool, pool_scale, w_out,
           g_mix_pre, g_mix_post, w_ff1, w_ff2, g_ff_pre, g_ff_post, w_ple_proj, w_ple_gate,
           g_ple_gate, g_ple_post):
    m, d = h.shape
    assert seq % ROW_TILE == 0 and seq % MIX_TM == 0 and m % FFN_TM == 0 and m % PLE_TM == 0
    assert w_in.shape == (d, 4 * MLSTM_WIDTH + 2 * HEADS + POOL_WIDTH) and w_ff1.shape[1] % FFN_TF == 0
    qkvo = 4 * MLSTM_WIDTH
    row = lambda v: v.reshape(1, -1).astype(F32)
    w_in16 = w_in.astype(BF16)
    w_u16 = w_in16[:, qkvo + 2 * HEADS:]
    w_gi = w_in[:, qkvo:qkvo + HEADS]
    w_gf = w_in[:, qkvo + HEADS:qkvo + 2 * HEADS]
    lane_pad = lambda w: jnp.pad(w, ((0, 0), (0, LANES - HEADS)))
    w_gcol = jnp.concatenate([lane_pad(w_gi), lane_pad(w_gf)], axis=1).astype(BF16)
    b_col = jnp.concatenate([lane_pad(b_gates[None, :HEADS]), lane_pad(b_gates[None, HEADS:])],
                            axis=1).astype(F32)
    b_row = b_gates.reshape(2 * HEADS, 1).astype(F32)
    wqk = _tile_qk_columns(w_qk_conv[:, :MLSTM_WIDTH], w_qk_conv[:, MLSTM_WIDTH:]).astype(F32)

    u, hml = _proj_mlstm(h, row(g_mix_pre), w_in16, w_u16, w_gcol, b_col, b_row, wqk,
                         row(g_mlstm), seq)
    w_out16 = w_out.astype(BF16)
    h, hn = _mix(hml, u, h, w_pool.astype(BF16), row(pool_scale), w_out16[:MLSTM_WIDTH],
                 w_out16[MLSTM_WIDTH:], row(g_mix_post), row(g_ff_pre), seq)
    ff = _ffn(hn, w_ff1, w_ff2.astype(BF16))
    return _ple(h, ff, p2, row(g_ff_post), row(g_ple_gate), w_ple_gate.astype(BF16),
                w_ple_proj.astype(BF16), row(g_ple_post))


def kernel(x, p, w_in, b_gates, w_qk_conv, g_mlstm, w_pool, pool_scale, w_out, g_mix_pre,
           g_mix_post, w_ff1, w_ff2, g_ff_pre, g_ff_post, w_ple_proj, w_ple_gate, g_ple_gate,
           g_ple_post):
    batch, seq, d = x.shape
    h = x.reshape(batch * seq, d)
    for i in range(p.shape[0]):
        h = _layer(h, p[i].reshape(batch * seq, -1), seq, w_in[i], b_gates[i],
                   w_qk_conv[i], g_mlstm[i], w_pool[i], pool_scale[i], w_out[i], g_mix_pre[i],
                   g_mix_post[i], w_ff1[i], w_ff2[i], g_ff_pre[i], g_ff_post[i], w_ple_proj[i],
                   w_ple_gate[i], g_ple_gate[i], g_ple_post[i])
    return h.reshape(batch, seq, d)
```

```python
import functools

import jax
import jax.numpy as jnp
from jax import lax
from jax.experimental import pallas as pl
from jax.experimental.pallas import tpu as pltpu

EPS = 1e-6
HEADS = 8
HEAD_DIM = 128
MLSTM_WIDTH = HEADS * HEAD_DIM
QK_CONV = 4
POOL_WINDOWS = (2, 4, 8, 16)
POOL_GROUP_WIDTH = 256
POOL_WIDTH = POOL_GROUP_WIDTH * len(POOL_WINDOWS)

LANES = 128
SUBLANES = 8
POOL_HALO = 16
CHUNK = 128
VMEM_LIMIT = 56 * 1024 * 1024

COL_TILES = 4
HEADS_PER_TILE = HEADS // COL_TILES
PAIR = HEADS_PER_TILE * HEAD_DIM
TILE_COLS = 4 * PAIR
ROW_TILE = COL_TILES * CHUNK

MIX_TM = 512
PLE_TM = 512
FFN_TM = 1024
FFN_TF = 1024
ROW_SPLIT = 2

F32 = jnp.float32
BF16 = jnp.bfloat16


def _rms(x, g):
    return x * lax.rsqrt(jnp.mean(x * x, axis=-1, keepdims=True) + EPS) * g


def _log_sigmoid(x):
    return jnp.minimum(x, 0.0) - jnp.log1p(jnp.exp(-jnp.abs(x)))


def _sigmoid(x):
    return 1.0 / (1.0 + jnp.exp(-x))


def _row_groups(n):
    rows = n // ROW_SPLIT
    return [slice(r * rows, (r + 1) * rows) for r in range(ROW_SPLIT)]


def _mlstm_chunk(first, proj_ref, gc, gr, bc_ref, br_ref, wqk_ref, gn_ref, out_ref, ct_ref, m_ref,
                 xe_ref, between):
    L = CHUNK
    reset = lambda v: jnp.where(first, 0.0, v)
    between(0)

    xe_ref[0:SUBLANES, :] = reset(xe_ref[0:SUBLANES, :])
    qk = []
    for t in range(COL_TILES):
        cs = slice(t * 2 * PAIR, (t + 1) * 2 * PAIR)
        x = proj_ref[t, :, 0:2 * PAIR]
        xe_ref[SUBLANES:SUBLANES + L, cs] = x
        acc = x * wqk_ref[QK_CONV - 1:QK_CONV, cs]
        for back in range(1, QK_CONV):
            tap = QK_CONV - 1 - back
            acc = acc + xe_ref[SUBLANES - back:SUBLANES - back + L, cs] * wqk_ref[tap:tap + 1, cs]
        xe_ref[0:SUBLANES, cs] = x[L - SUBLANES:L, :]
        qk.append(acc * _sigmoid(acc))

    gc = gc + bc_ref[...]
    li_c = gc[:, :LANES]
    lf_c = _log_sigmoid(gc[:, LANES:])
    gr = gr + br_ref[...]
    li_r = gr[0:SUBLANES, :]
    lf_r = _log_sigmoid(gr[SUBLANES:, :])
    row = lax.broadcasted_iota(jnp.int32, (L, L), 0)
    col = lax.broadcasted_iota(jnp.int32, (L, L), 1)
    causal = row >= col
    tril = jnp.where(causal, 1.0, 0.0).astype(F32)
    triu = jnp.where(row <= col, 1.0, 0.0).astype(F32)
    b_c = jnp.dot(tril, lf_c, preferred_element_type=F32, precision=lax.Precision.HIGHEST)
    b_r = jnp.dot(lf_r, triu, preferred_element_type=F32, precision=lax.Precision.HIGHEST)
    r_r = li_r - b_r
    btot = b_c[L - 1:L, :]
    a_c = btot - b_c + li_c
    m_prev = reset(m_ref[0:1, :])
    m_new = jnp.maximum(btot + m_prev, jnp.max(a_c, axis=0, keepdims=True))
    decay = jnp.exp(btot + m_prev - m_new)
    w_c = jnp.exp(a_c - m_new)
    inter_log = b_c + m_prev
    m_ref[0:1, :] = m_new

    ones_col = jnp.where(lax.broadcasted_iota(jnp.int32, (L, LANES), 1) == 0, 1.0, 0.0).astype(F32)

    for h in range(HEADS):
        t, e = divmod(h, HEADS_PER_TILE)
        if e == 0:
            between(t + 1)
        es = slice(e * HEAD_DIM, (e + 1) * HEAD_DIM)
        sl = slice(h * HEAD_DIM, (h + 1) * HEAD_DIM)
        qf = qk[t][:, es]
        qh = qf.astype(BF16)
        kh = (qk[t][:, PAIR + e * HEAD_DIM:PAIR + (e + 1) * HEAD_DIM] * (HEAD_DIM ** -0.5)).astype(BF16)
        vh = proj_ref[t, :, 2 * PAIR + e * HEAD_DIM:2 * PAIR + (e + 1) * HEAD_DIM]
        oh = proj_ref[t, :, 3 * PAIR + e * HEAD_DIM:3 * PAIR + (e + 1) * HEAD_DIM]
        vaug = jnp.concatenate([vh, ones_col], axis=1)
        dlog = jnp.where(causal, b_c[:, h:h + 1] + r_r[h:h + 1, :], -jnp.inf)
        inter = inter_log[:, h:h + 1]
        m_t = jnp.maximum(inter, jnp.max(dlog, axis=1, keepdims=True))
        dw = jnp.exp(dlog - m_t)
        inter_w = jnp.exp(inter - m_t)
        s = lax.dot_general(qh, kh, (((1,), (1,)), ((), ())), preferred_element_type=F32) * dw
        ct = reset(ct_ref[h])
        lhs = jnp.concatenate([s.astype(BF16), (inter_w * qf).astype(BF16)], axis=1)
        rhs = jnp.concatenate([vaug.astype(BF16), ct.astype(BF16)], axis=0)
        tot = jnp.dot(lhs, rhs, preferred_element_type=F32)
        num = tot[:, :HEAD_DIM]
        den = tot[:, HEAD_DIM:HEAD_DIM + 1]
        cell = num / jnp.maximum(jnp.abs(den), jnp.exp(-m_t))
        cell = _rms(cell, gn_ref[:, sl])
        out_ref[:, sl] = (_sigmoid(oh) * cell).astype(out_ref.dtype)
        wv = (w_c[:, h:h + 1] * vaug).astype(BF16)
        upd = lax.dot_general(kh, wv, (((0,), (0,)), ((), ())), preferred_element_type=F32)
        ct_ref[h] = decay[:, h:h + 1] * ct + upd


def _proj_mlstm_kernel(x0_ref, xnext_ref, g_ref, wq_ref, wk_ref, wv_ref, wo_ref, wu_ref, wgc_ref,
                       bc_ref, br_ref, wqk_ref, gn_ref,
                       u_ref, hml_ref, xn_a, xn_b, proj_a, proj_b, gcol_a, gcol_b, grow_a, grow_b,
                       ct_ref, m_ref, xe_ref, *, tiles_per_seq):
    w_refs = (wq_ref, wk_ref, wv_ref, wo_ref, wu_ref)
    i = pl.program_id(0)
    j = pl.program_id(1)

    @pl.when((i == 0) & (j == 0))
    def _():
        xn_a[...] = _rms(x0_ref[...], g_ref[...]).astype(BF16)
        proj_b[...] = jnp.zeros_like(proj_b)
        gcol_b[...] = jnp.zeros_like(gcol_b)
        grow_b[...] = jnp.zeros_like(grow_b)
        ct_ref[...] = jnp.zeros_like(ct_ref)
        m_ref[...] = jnp.zeros_like(m_ref)
        xe_ref[...] = jnp.zeros_like(xe_ref)

    def step(xn_cur, xn_nxt, proj_w, gcol_w, grow_w, proj_r, gcol_r, grow_r):
        r0 = pl.multiple_of(j * CHUNK, CHUNK)

        def project_piece(k):
            y = jnp.dot(xn_cur[...], w_refs[k][...], preferred_element_type=F32)
            if k < 4:
                proj_w[j, :, k * PAIR:(k + 1) * PAIR] = y
            else:
                u_ref[...] = y

        first = ((i + tiles_per_seq - 1) % tiles_per_seq == 0) & (j == 0)
        _mlstm_chunk(first, proj_r.at[:, pl.ds(r0, CHUNK), :], gcol_r[pl.ds(r0, CHUNK), :],
                     grow_r[j], bc_ref, br_ref, wqk_ref, gn_ref, hml_ref, ct_ref, m_ref, xe_ref,
                     project_piece)

        gc = jnp.dot(xn_cur[pl.ds(r0, CHUNK), :], wgc_ref[...], preferred_element_type=F32)
        gcol_w[pl.ds(r0, CHUNK), :] = gc
        grow_w[j] = jnp.concatenate(
            [gc[:, :LANES].T[0:SUBLANES, :], gc[:, LANES:].T[0:SUBLANES, :]], axis=0)

        xn_nxt[pl.ds(r0, CHUNK), :] = _rms(xnext_ref[...], g_ref[...]).astype(BF16)

    @pl.when(i % 2 == 0)
    def _():
        step(xn_a, xn_b, proj_a, gcol_a, grow_a, proj_b, gcol_b, grow_b)

    @pl.when(i % 2 == 1)
    def _():
        step(xn_b, xn_a, proj_b, gcol_b, grow_b, proj_a, gcol_a, grow_a)


def _proj_mlstm(x2, g, w_in16, w_u16, w_gcol, b_col, b_row, wqk, gn, seq):
    m, d = x2.shape
    nt = m // ROW_TILE
    last_chunk = m // CHUNK - 1
    const = lambda shape: pl.BlockSpec(shape, lambda i, j: (0,) * len(shape),
                                       pipeline_mode=pl.Buffered(1))
    slab = lambda s: pl.BlockSpec((d, PAIR), lambda i, j: (0, s * COL_TILES + j))
    return pl.pallas_call(
        functools.partial(_proj_mlstm_kernel, tiles_per_seq=seq // ROW_TILE),
        grid=(nt + 1, COL_TILES),
        in_specs=[
            const((ROW_TILE, d)),
            pl.BlockSpec((CHUNK, d),
                         lambda i, j: (jnp.minimum((i + 1) * COL_TILES + j, last_chunk), 0)),
            const((1, d)),
            slab(0), slab(1), slab(2), slab(3),
            pl.BlockSpec((d, POOL_GROUP_WIDTH), lambda i, j: (0, j)),
            const((d, 2 * LANES)),
            const((1, 2 * LANES)),
            const((2 * SUBLANES, 1)),
            const((QK_CONV, 2 * MLSTM_WIDTH)),
            const((1, MLSTM_WIDTH)),
        ],
        out_specs=[
            pl.BlockSpec((ROW_TILE, POOL_GROUP_WIDTH), lambda i, j: (i, j)),
            pl.BlockSpec((CHUNK, MLSTM_WIDTH),
                         lambda i, j: (jnp.maximum((i - 1) * COL_TILES + j, 0), 0)),
        ],
        out_shape=[
            jax.ShapeDtypeStruct((m + ROW_TILE, POOL_WIDTH), F32),
            jax.ShapeDtypeStruct((m, MLSTM_WIDTH), BF16),
        ],
        scratch_shapes=[
            pltpu.VMEM((ROW_TILE, d), BF16),
            pltpu.VMEM((ROW_TILE, d), BF16),
            pltpu.VMEM((COL_TILES, ROW_TILE, TILE_COLS), F32),
            pltpu.VMEM((COL_TILES, ROW_TILE, TILE_COLS), F32),
            pltpu.VMEM((ROW_TILE, 2 * LANES), F32),
            pltpu.VMEM((ROW_TILE, 2 * LANES), F32),
            pltpu.VMEM((COL_TILES, 2 * SUBLANES, CHUNK), F32),
            pltpu.VMEM((COL_TILES, 2 * SUBLANES, CHUNK), F32),
            pltpu.VMEM((HEADS, HEAD_DIM, 2 * HEAD_DIM), F32),
            pltpu.VMEM((SUBLANES, LANES), F32),
            pltpu.VMEM((SUBLANES + CHUNK, 2 * MLSTM_WIDTH), F32),
        ],
        compiler_params=pltpu.CompilerParams(
            dimension_semantics=("arbitrary", "arbitrary"), vmem_limit_bytes=VMEM_LIMIT),
        name="proj_mlstm",
    )(x2, x2, g, w_in16, w_in16, w_in16, w_in16, w_u16, w_gcol, b_col, b_row, wqk, gn)


def _mix_kernel(hml_ref, u_ref, x_ref, wpool_ref, pscale_ref, wo_ml_ref, wo_pool_ref, g_ref,
                gnext_ref, wnext_ref, out_ref, hn_ref, wnext16_ref, ue_ref, *, tiles_per_seq):
    tm = u_ref.shape[0]
    t = pl.program_id(0) % tiles_per_seq
    wnext16_ref[...] = wnext_ref[...].astype(BF16)

    @pl.when(t == 0)
    def _():
        ue_ref[0:POOL_HALO, :] = jnp.zeros((POOL_HALO, POOL_WIDTH), F32)

    u = u_ref[...]
    ue_ref[POOL_HALO:POOL_HALO + tm, :] = u
    pos = (lax.broadcasted_iota(jnp.int32, (tm, 1), 0) + (t * tm + 1)).astype(F32)
    mixed = []
    for g, window in enumerate(POOL_WINDOWS):
        sl = slice(g * POOL_GROUP_WIDTH, (g + 1) * POOL_GROUP_WIDTH)
        win = ue_ref[:, sl]
        span = 1
        while span < window:
            win = win + pltpu.roll(win, span, axis=0)
            span *= 2
        mean = win[POOL_HALO:, :] / jnp.minimum(pos, float(window))
        pooled = (mean - u[:, sl]).astype(BF16)
        mixed.append((jnp.dot(pooled, wpool_ref[g], preferred_element_type=F32)
                      * pscale_ref[:, sl]).astype(BF16))
    ue_ref[0:POOL_HALO, :] = u[tm - POOL_HALO:tm, :]
    hpool = jnp.concatenate(mixed, axis=1)
    for rs in _row_groups(tm):
        mix = (jnp.dot(hml_ref[rs, :], wo_ml_ref[...], preferred_element_type=F32)
               + jnp.dot(hpool[rs, :], wo_pool_ref[...], preferred_element_type=F32))
        h = x_ref[rs, :] + _rms(mix, g_ref[...])
        out_ref[rs, :] = h
        hn_ref[rs, :] = _rms(h, gnext_ref[...]).astype(BF16)


def _mix(hml, u, x2, w_pool, pool_scale, wo_ml, wo_pool, g_post, g_next, w_next, seq):
    m, d = x2.shape
    tm = MIX_TM
    wr = w_next.shape[0] // (m // tm)
    const = lambda shape: pl.BlockSpec(shape, lambda i: (0,) * len(shape),
                                       pipeline_mode=pl.Buffered(1))
    return pl.pallas_call(
        functools.partial(_mix_kernel, tiles_per_seq=seq // tm),
        grid=(m // tm,),
        in_specs=[
            pl.BlockSpec((tm, MLSTM_WIDTH), lambda i: (i, 0)),
            pl.BlockSpec((tm, POOL_WIDTH), lambda i: (i, 0)),
            pl.BlockSpec((tm, d), lambda i: (i, 0)),
            const(w_pool.shape),
            const((1, POOL_WIDTH)),
            const(wo_ml.shape),
            const(wo_pool.shape),
            const((1, d)),
            const((1, d)),
            pl.BlockSpec((wr, w_next.shape[1]), lambda i: (i, 0)),
        ],
        out_specs=[pl.BlockSpec((tm, d), lambda i: (i, 0)), pl.BlockSpec((tm, d), lambda i: (i, 0)),
                   pl.BlockSpec((wr, w_next.shape[1]), lambda i: (i, 0))],
        out_shape=[jax.ShapeDtypeStruct((m, d), F32), jax.ShapeDtypeStruct((m, d), BF16),
                   jax.ShapeDtypeStruct(w_next.shape, BF16)],
        scratch_shapes=[pltpu.VMEM((POOL_HALO + tm, POOL_WIDTH), F32)],
        compiler_params=pltpu.CompilerParams(
            dimension_semantics=("arbitrary",), vmem_limit_bytes=VMEM_LIMIT),
        name="mix",
    )(hml, u, x2, w_pool, pool_scale, wo_ml, wo_pool, g_post, g_next, w_next)


def _ffn_kernel(hn_ref, w1_ref, w2_ref, acc_ref):
    j = pl.program_id(1)
    groups = _row_groups(hn_ref.shape[0])
    w1 = w1_ref[...].astype(BF16)
    acts = []
    for g in groups:
        a = jnp.dot(hn_ref[g, :], w1, preferred_element_type=F32)
        acts.append(jnp.square(jnp.maximum(a, 0.0)).astype(BF16))
    for g, a in zip(groups, acts):
        prev = jnp.where(j == 0, 0.0, acc_ref[g, :])
        acc_ref[g, :] = prev + jnp.dot(a, w2_ref[...], preferred_element_type=F32)


def _ffn(hn, w1, w2):
    m, d = hn.shape
    f = w1.shape[1]
    return pl.pallas_call(
        _ffn_kernel,
        grid=(m // FFN_TM, f // FFN_TF),
        in_specs=[
            pl.BlockSpec((FFN_TM, d), lambda i, j: (i, 0)),
            pl.BlockSpec((d, FFN_TF), lambda i, j: (0, j)),
            pl.BlockSpec((FFN_TF, d), lambda i, j: (j, 0)),
        ],
        out_specs=pl.BlockSpec((FFN_TM, d), lambda i, j: (i, 0)),
        out_shape=jax.ShapeDtypeStruct((m, d), F32),
        compiler_params=pltpu.CompilerParams(
            dimension_semantics=("arbitrary", "arbitrary"), vmem_limit_bytes=VMEM_LIMIT),
        name="ffn",
    )(hn, w1, w2)


def _ple_kernel(h_ref, ff_ref, p_ref, gff_ref, ggate_ref, wgate_ref, wproj_ref, gpost_ref, out_ref,
                hn_ref):
    groups = _row_groups(h_ref.shape[0])
    for rs in groups:
        h = h_ref[rs, :] + _rms(ff_ref[rs, :], gff_ref[...])
        out_ref[rs, :] = h
        hn_ref[rs, :] = _rms(h, ggate_ref[...]).astype(BF16)
    for rs in groups:
        gate = _sigmoid(jnp.dot(hn_ref[rs, :], wgate_ref[...], preferred_element_type=F32))
        e = jnp.dot(p_ref[rs, :].astype(BF16), wproj_ref[...], preferred_element_type=F32)
        out_ref[rs, :] += _rms(e * gate, gpost_ref[...])


def _ple(h, ff, p2, g_ff_post, g_gate, w_gate, w_proj, g_post):
    m, d = h.shape
    pd = p2.shape[1]
    tm = PLE_TM
    const = lambda shape: pl.BlockSpec(shape, lambda i: (0,) * len(shape),
                                       pipeline_mode=pl.Buffered(1))
    return pl.pallas_call(
        _ple_kernel,
        grid=(m // tm,),
        in_specs=[
            pl.BlockSpec((tm, d), lambda i: (i, 0)),
            pl.BlockSpec((tm, d), lambda i: (i, 0)),
            pl.BlockSpec((tm, pd), lambda i: (i, 0)),
            const((1, d)),
            const((1, d)),
            const(w_gate.shape),
            const(w_proj.shape),
            const((1, d)),
        ],
        out_specs=pl.BlockSpec((tm, d), lambda i: (i, 0)),
        out_shape=jax.ShapeDtypeStruct((m, d), F32),
        scratch_shapes=[pltpu.VMEM((tm, d), BF16)],
        compiler_params=pltpu.CompilerParams(
            dimension_semantics=("arbitrary",), vmem_limit_bytes=VMEM_LIMIT),
        name="ple",
    )(h, ff, p2, g_ff_post, g_gate, w_gate, w_proj, g_post)


def _tile_qk_columns(w_q, w_k):
    r = w_q.shape[0]
    a = jnp.stack([w_q.reshape(r, COL_TILES, PAIR), w_k.reshape(r, COL_TILES, PAIR)], axis=2)
    return a.reshape(r, 2 * MLSTM_WIDTH)


def _layer(h, p2, seq, w_in, b_gates, w_qk_conv, g_mlstm, w_pool, pool_scale, w_out,
           g_mix_pre, g_mix_post, w_ff1, w_ff2, g_ff_pre, g_ff_post, w_ple_proj, w_ple_gate,
           g_ple_gate, g_ple_post):
    m, d = h.shape
    assert seq % ROW_TILE == 0 and seq % MIX_TM == 0 and m % FFN_TM == 0 and m % PLE_TM == 0
    assert w_in.shape == (d, 4 * MLSTM_WIDTH + 2 * HEADS + POOL_WIDTH) and w_ff1.shape[1] % FFN_TF == 0
    qkvo = 4 * MLSTM_WIDTH
    row = lambda v: v.reshape(1, -1).astype(F32)
    w_in16 = w_in.astype(BF16)
    w_u16 = w_in16[:, qkvo + 2 * HEADS:]
    w_gi = w_in[:, qkvo:qkvo + HEADS]
    w_gf = w_in[:, qkvo + HEADS:qkvo + 2 * HEADS]
    lane_pad = lambda w: jnp.pad(w, ((0, 0), (0, LANES - HEADS)))
    w_gcol = jnp.concatenate([lane_pad(w_gi), lane_pad(w_gf)], axis=1).astype(BF16)
    b_col = jnp.concatenate([lane_pad(b_gates[None, :HEADS]), lane_pad(b_gates[None, HEADS:])],
                            axis=1).astype(F32)
    b_row = b_gates.reshape(2 * HEADS, 1).astype(F32)
    wqk = _tile_qk_columns(w_qk_conv[:, :MLSTM_WIDTH], w_qk_conv[:, MLSTM_WIDTH:]).astype(F32)

    u, hml = _proj_mlstm(h, row(g_mix_pre), w_in16, w_u16, w_gcol, b_col, b_row, wqk,
                         row(g_mlstm), seq)
    w_out16 = w_out.astype(BF16)
    h, hn, w_ff2_16 = _mix(hml, u, h, w_pool.astype(BF16), row(pool_scale), w_out16[:MLSTM_WIDTH],
                           w_out16[MLSTM_WIDTH:], row(g_mix_post), row(g_ff_pre), w_ff2, seq)
    ff = _ffn(hn, w_ff1, w_ff2_16)
    return _ple(h, ff, p2, row(g_ff_post), row(g_ple_gate), w_ple_gate.astype(BF16),
                w_ple_proj.astype(BF16), row(g_ple_post))


def kernel(x, p, w_in, b_gates, w_qk_conv, g_mlstm, w_pool, pool_scale, w_out, g_mix_pre,
           g_mix_post, w_ff1, w_ff2, g_ff_pre, g_ff_post, w_ple_proj, w_ple_gate, g_ple_gate,
           g_ple_post):
    batch, seq, d = x.shape
    h = x.reshape(batch * seq, d)
    for i in range(p.shape[0]):
        h = _layer(h, p[i].reshape(batch * seq, -1), seq, w_in[i], b_gates[i],
                   w_qk_conv[i], g_mlstm[i], w_pool[i], pool_scale[i], w_out[i], g_mix_pre[i],
                   g_mix_post[i], w_ff1[i], w_ff2[i], g_ff_pre[i], g_ff_post[i], w_ple_proj[i],
                   w_ple_gate[i], g_ple_gate[i], g_ple_post[i])
    return h.reshape(batch, seq, d)
```

```python
import functools

import jax
import jax.numpy as jnp
from jax import lax
from jax.experimental import pallas as pl
from jax.experimental.pallas import tpu as pltpu

EPS = 1e-6
HEADS = 8
HEAD_DIM = 128
MLSTM_WIDTH = HEADS * HEAD_DIM
QK_CONV = 4
POOL_WINDOWS = (2, 4, 8, 16)
POOL_GROUP_WIDTH = 256
POOL_WIDTH = POOL_GROUP_WIDTH * len(POOL_WINDOWS)

LANES = 128
SUBLANES = 8
POOL_HALO = 16
CHUNK = 128
VMEM_LIMIT = 56 * 1024 * 1024

COL_TILES = 4
HEADS_PER_TILE = HEADS // COL_TILES
PAIR = HEADS_PER_TILE * HEAD_DIM
TILE_COLS = 4 * PAIR
ROW_TILE = COL_TILES * CHUNK

MIX_TM = 512
PLE_TM = 512
FFN_TM = 1024
FFN_TF = 1024
ROW_SPLIT = 2

F32 = jnp.float32
BF16 = jnp.bfloat16


def _rms(x, g):
    return x * lax.rsqrt(jnp.mean(x * x, axis=-1, keepdims=True) + EPS) * g


def _log_sigmoid(x):
    return jnp.minimum(x, 0.0) - jnp.log1p(jnp.exp(-jnp.abs(x)))


def _sigmoid(x):
    return 1.0 / (1.0 + jnp.exp(-x))


def _row_groups(n):
    rows = n // ROW_SPLIT
    return [slice(r * rows, (r + 1) * rows) for r in range(ROW_SPLIT)]


def _mlstm_chunk(first, proj_ref, gc, gr, bc_ref, br_ref, wqk_ref, gn_ref, out_ref, ct_ref, m_ref,
                 xe_ref, between):
    L = CHUNK
    reset = lambda v: jnp.where(first, 0.0, v)
    between(0)

    xe_ref[0:SUBLANES, :] = reset(xe_ref[0:SUBLANES, :])
    qk = []
    for t in range(COL_TILES):
        cs = slice(t * 2 * PAIR, (t + 1) * 2 * PAIR)
        x = proj_ref[t, :, 0:2 * PAIR]
        xe_ref[SUBLANES:SUBLANES + L, cs] = x
        acc = x * wqk_ref[QK_CONV - 1:QK_CONV, cs]
        for back in range(1, QK_CONV):
            tap = QK_CONV - 1 - back
            acc = acc + xe_ref[SUBLANES - back:SUBLANES - back + L, cs] * wqk_ref[tap:tap + 1, cs]
        xe_ref[0:SUBLANES, cs] = x[L - SUBLANES:L, :]
        qk.append(acc * _sigmoid(acc))

    gc = gc + bc_ref[...]
    li_c = gc[:, :LANES]
    lf_c = _log_sigmoid(gc[:, LANES:])
    gr = gr + br_ref[...]
    li_r = gr[0:SUBLANES, :]
    lf_r = _log_sigmoid(gr[SUBLANES:, :])
    row = lax.broadcasted_iota(jnp.int32, (L, L), 0)
    col = lax.broadcasted_iota(jnp.int32, (L, L), 1)
    causal = row >= col
    tril = jnp.where(causal, 1.0, 0.0).astype(F32)
    triu = jnp.where(row <= col, 1.0, 0.0).astype(F32)
    b_c = jnp.dot(tril, lf_c, preferred_element_type=F32, precision=lax.Precision.HIGHEST)
    b_r = jnp.dot(lf_r, triu, preferred_element_type=F32, precision=lax.Precision.HIGHEST)
    r_r = li_r - b_r
    btot = b_c[L - 1:L, :]
    a_c = btot - b_c + li_c
    m_prev = reset(m_ref[0:1, :])
    m_new = jnp.maximum(btot + m_prev, jnp.max(a_c, axis=0, keepdims=True))
    decay = jnp.exp(btot + m_prev - m_new)
    w_c = jnp.exp(a_c - m_new)
    inter_log = b_c + m_prev
    m_ref[0:1, :] = m_new

    ones_col = jnp.where(lax.broadcasted_iota(jnp.int32, (L, LANES), 1) == 0, 1.0, 0.0).astype(F32)

    for h in range(HEADS):
        t, e = divmod(h, HEADS_PER_TILE)
        if e == 0:
            between(t + 1)
        es = slice(e * HEAD_DIM, (e + 1) * HEAD_DIM)
        sl = slice(h * HEAD_DIM, (h + 1) * HEAD_DIM)
        qf = qk[t][:, es]
        qh = qf.astype(BF16)
        kh = (qk[t][:, PAIR + e * HEAD_DIM:PAIR + (e + 1) * HEAD_DIM] * (HEAD_DIM ** -0.5)).astype(BF16)
        vh = proj_ref[t, :, 2 * PAIR + e * HEAD_DIM:2 * PAIR + (e + 1) * HEAD_DIM]
        oh = proj_ref[t, :, 3 * PAIR + e * HEAD_DIM:3 * PAIR + (e + 1) * HEAD_DIM]
        vaug = jnp.concatenate([vh, ones_col], axis=1)
        dlog = jnp.where(causal, b_c[:, h:h + 1] + r_r[h:h + 1, :], -jnp.inf)
        inter = inter_log[:, h:h + 1]
        m_t = jnp.maximum(inter, jnp.max(dlog, axis=1, keepdims=True))
        dw = jnp.exp(dlog - m_t)
        inter_w = jnp.exp(inter - m_t)
        s = lax.dot_general(qh, kh, (((1,), (1,)), ((), ())), preferred_element_type=F32) * dw
        ct = reset(ct_ref[h])
        lhs = jnp.concatenate([s.astype(BF16), (inter_w * qf).astype(BF16)], axis=1)
        rhs = jnp.concatenate([vaug.astype(BF16), ct.astype(BF16)], axis=0)
        tot = jnp.dot(lhs, rhs, preferred_element_type=F32)
        num = tot[:, :HEAD_DIM]
        den = tot[:, HEAD_DIM:HEAD_DIM + 1]
        cell = num / jnp.maximum(jnp.abs(den), jnp.exp(-m_t))
        cell = _rms(cell, gn_ref[:, sl])
        out_ref[:, sl] = (_sigmoid(oh) * cell).astype(out_ref.dtype)
        wv = (w_c[:, h:h + 1] * vaug).astype(BF16)
        upd = lax.dot_general(kh, wv, (((0,), (0,)), ((), ())), preferred_element_type=F32)
        ct_ref[h] = decay[:, h:h + 1] * ct + upd


def _proj_mlstm_kernel(x0_ref, xnext_ref, g_ref, wq_ref, wk_ref, wv_ref, wo_ref, wu_ref, wgc_ref,
                       bc_ref, br_ref, wqk_ref, gn_ref,
                       u_ref, hml_ref, xn_a, xn_b, proj_a, proj_b, gcol_a, gcol_b, grow_a, grow_b,
                       ct_ref, m_ref, xe_ref, *, tiles_per_seq):
    w_refs = (wq_ref, wk_ref, wv_ref, wo_ref, wu_ref)
    i = pl.program_id(0)
    j = pl.program_id(1)

    @pl.when((i == 0) & (j == 0))
    def _():
        xn_a[...] = _rms(x0_ref[...], g_ref[...]).astype(BF16)
        proj_b[...] = jnp.zeros_like(proj_b)
        gcol_b[...] = jnp.zeros_like(gcol_b)
        grow_b[...] = jnp.zeros_like(grow_b)
        ct_ref[...] = jnp.zeros_like(ct_ref)
        m_ref[...] = jnp.zeros_like(m_ref)
        xe_ref[...] = jnp.zeros_like(xe_ref)

    def step(xn_cur, xn_nxt, proj_w, gcol_w, grow_w, proj_r, gcol_r, grow_r):
        r0 = pl.multiple_of(j * CHUNK, CHUNK)

        def project_piece(k):
            y = jnp.dot(xn_cur[...], w_refs[k][...], preferred_element_type=F32)
            if k < 4:
                proj_w[j, :, k * PAIR:(k + 1) * PAIR] = y
            else:
                u_ref[...] = y

        first = ((i + tiles_per_seq - 1) % tiles_per_seq == 0) & (j == 0)
        _mlstm_chunk(first, proj_r.at[:, pl.ds(r0, CHUNK), :], gcol_r[pl.ds(r0, CHUNK), :],
                     grow_r[j], bc_ref, br_ref, wqk_ref, gn_ref, hml_ref, ct_ref, m_ref, xe_ref,
                     project_piece)

        gc = jnp.dot(xn_cur[pl.ds(r0, CHUNK), :], wgc_ref[...], preferred_element_type=F32)
        gcol_w[pl.ds(r0, CHUNK), :] = gc
        grow_w[j] = jnp.concatenate(
            [gc[:, :LANES].T[0:SUBLANES, :], gc[:, LANES:].T[0:SUBLANES, :]], axis=0)

        xn_nxt[pl.ds(r0, CHUNK), :] = _rms(xnext_ref[...], g_ref[...]).astype(BF16)

    @pl.when(i % 2 == 0)
    def _():
        step(xn_a, xn_b, proj_a, gcol_a, grow_a, proj_b, gcol_b, grow_b)

    @pl.when(i % 2 == 1)
    def _():
        step(xn_b, xn_a, proj_b, gcol_b, grow_b, proj_a, gcol_a, grow_a)


def _proj_mlstm(x2, g, w_in16, w_u16, w_gcol, b_col, b_row, wqk, gn, seq):
    m, d = x2.shape
    nt = m // ROW_TILE
    last_chunk = m // CHUNK - 1
    const = lambda shape: pl.BlockSpec(shape, lambda i, j: (0,) * len(shape),
                                       pipeline_mode=pl.Buffered(1))
    slab = lambda s: pl.BlockSpec((d, PAIR), lambda i, j: (0, s * COL_TILES + j))
    return pl.pallas_call(
        functools.partial(_proj_mlstm_kernel, tiles_per_seq=seq // ROW_TILE),
        grid=(nt + 1, COL_TILES),
        in_specs=[
            const((ROW_TILE, d)),
            pl.BlockSpec((CHUNK, d),
                         lambda i, j: (jnp.minimum((i + 1) * COL_TILES + j, last_chunk), 0)),
            const((1, d)),
            slab(0), slab(1), slab(2), slab(3),
            pl.BlockSpec((d, POOL_GROUP_WIDTH), lambda i, j: (0, j)),
            const((d, 2 * LANES)),
            const((1, 2 * LANES)),
            const((2 * SUBLANES, 1)),
            const((QK_CONV, 2 * MLSTM_WIDTH)),
            const((1, MLSTM_WIDTH)),
        ],
        out_specs=[
            pl.BlockSpec((ROW_TILE, POOL_GROUP_WIDTH), lambda i, j: (i, j)),
            pl.BlockSpec((CHUNK, MLSTM_WIDTH),
                         lambda i, j: (jnp.maximum((i - 1) * COL_TILES + j, 0), 0)),
        ],
        out_shape=[
            jax.ShapeDtypeStruct((m + ROW_TILE, POOL_WIDTH), F32),
            jax.ShapeDtypeStruct((m, MLSTM_WIDTH), BF16),
        ],
        scratch_shapes=[
            pltpu.VMEM((ROW_TILE, d), BF16),
            pltpu.VMEM((ROW_TILE, d), BF16),
            pltpu.VMEM((COL_TILES, ROW_TILE, TILE_COLS), F32),
            pltpu.VMEM((COL_TILES, ROW_TILE, TILE_COLS), F32),
            pltpu.VMEM((ROW_TILE, 2 * LANES), F32),
            pltpu.VMEM((ROW_TILE, 2 * LANES), F32),
            pltpu.VMEM((COL_TILES, 2 * SUBLANES, CHUNK), F32),
            pltpu.VMEM((COL_TILES, 2 * SUBLANES, CHUNK), F32),
            pltpu.VMEM((HEADS, HEAD_DIM, 2 * HEAD_DIM), F32),
            pltpu.VMEM((SUBLANES, LANES), F32),
            pltpu.VMEM((SUBLANES + CHUNK, 2 * MLSTM_WIDTH), F32),
        ],
        compiler_params=pltpu.CompilerParams(
            dimension_semantics=("arbitrary", "arbitrary"), vmem_limit_bytes=VMEM_LIMIT),
        name="proj_mlstm",
    )(x2, x2, g, w_in16, w_in16, w_in16, w_in16, w_u16, w_gcol, b_col, b_row, wqk, gn)


def _mix_kernel(hml_ref, u_ref, x_ref, wpool_ref, pscale_ref, wo_ml_ref, wo_pool_ref, g_ref,
                gnext_ref, wa_ref, wb_ref, out_ref, hn_ref, wa16_ref, wb16_ref, ue_ref, *,
                tiles_per_seq):
    tm = u_ref.shape[0]
    t = pl.program_id(0) % tiles_per_seq
    wa16_ref[...] = wa_ref[...].astype(BF16)
    wb16_ref[...] = wb_ref[...].astype(BF16)

    @pl.when(t == 0)
    def _():
        ue_ref[0:POOL_HALO, :] = jnp.zeros((POOL_HALO, POOL_WIDTH), F32)

    u = u_ref[...]
    ue_ref[POOL_HALO:POOL_HALO + tm, :] = u
    pos = (lax.broadcasted_iota(jnp.int32, (tm, 1), 0) + (t * tm + 1)).astype(F32)
    mixed = []
    for g, window in enumerate(POOL_WINDOWS):
        sl = slice(g * POOL_GROUP_WIDTH, (g + 1) * POOL_GROUP_WIDTH)
        win = ue_ref[:, sl]
        span = 1
        while span < window:
            win = win + pltpu.roll(win, span, axis=0)
            span *= 2
        mean = win[POOL_HALO:, :] / jnp.minimum(pos, float(window))
        pooled = (mean - u[:, sl]).astype(BF16)
        mixed.append((jnp.dot(pooled, wpool_ref[g], preferred_element_type=F32)
                      * pscale_ref[:, sl]).astype(BF16))
    ue_ref[0:POOL_HALO, :] = u[tm - POOL_HALO:tm, :]
    hpool = jnp.concatenate(mixed, axis=1)
    for rs in _row_groups(tm):
        mix = (jnp.dot(hml_ref[rs, :], wo_ml_ref[...], preferred_element_type=F32)
               + jnp.dot(hpool[rs, :], wo_pool_ref[...], preferred_element_type=F32))
        h = x_ref[rs, :] + _rms(mix, g_ref[...])
        out_ref[rs, :] = h
        hn_ref[rs, :] = _rms(h, gnext_ref[...]).astype(BF16)


def _mix(hml, u, x2, w_pool, pool_scale, wo_ml, wo_pool, g_post, g_next, w_later, seq):
    m, d = x2.shape
    tm = MIX_TM
    share = lambda w: pl.BlockSpec((w.shape[0] // (m // tm), w.shape[1]), lambda i: (i, 0))
    const = lambda shape: pl.BlockSpec(shape, lambda i: (0,) * len(shape),
                                       pipeline_mode=pl.Buffered(1))
    return pl.pallas_call(
        functools.partial(_mix_kernel, tiles_per_seq=seq // tm),
        grid=(m // tm,),
        in_specs=[
            pl.BlockSpec((tm, MLSTM_WIDTH), lambda i: (i, 0)),
            pl.BlockSpec((tm, POOL_WIDTH), lambda i: (i, 0)),
            pl.BlockSpec((tm, d), lambda i: (i, 0)),
            const(w_pool.shape),
            const((1, POOL_WIDTH)),
            const(wo_ml.shape),
            const(wo_pool.shape),
            const((1, d)),
            const((1, d)),
            share(w_later[0]), share(w_later[1]),
        ],
        out_specs=[pl.BlockSpec((tm, d), lambda i: (i, 0)), pl.BlockSpec((tm, d), lambda i: (i, 0)),
                   share(w_later[0]), share(w_later[1])],
        out_shape=[jax.ShapeDtypeStruct((m, d), F32), jax.ShapeDtypeStruct((m, d), BF16),
                   jax.ShapeDtypeStruct(w_later[0].shape, BF16),
                   jax.ShapeDtypeStruct(w_later[1].shape, BF16)],
        scratch_shapes=[pltpu.VMEM((POOL_HALO + tm, POOL_WIDTH), F32)],
        compiler_params=pltpu.CompilerParams(
            dimension_semantics=("arbitrary",), vmem_limit_bytes=VMEM_LIMIT),
        name="mix",
    )(hml, u, x2, w_pool, pool_scale, wo_ml, wo_pool, g_post, g_next, *w_later)


def _ffn_kernel(hn_ref, w1_ref, w2_ref, acc_ref):
    j = pl.program_id(1)
    groups = _row_groups(hn_ref.shape[0])
    w1 = w1_ref[...].astype(BF16)
    acts = []
    for g in groups:
        a = jnp.dot(hn_ref[g, :], w1, preferred_element_type=F32)
        acts.append(jnp.square(jnp.maximum(a, 0.0)).astype(BF16))
    for g, a in zip(groups, acts):
        prev = jnp.where(j == 0, 0.0, acc_ref[g, :])
        acc_ref[g, :] = prev + jnp.dot(a, w2_ref[...], preferred_element_type=F32)


def _ffn(hn, w1, w2):
    m, d = hn.shape
    f = w1.shape[1]
    return pl.pallas_call(
        _ffn_kernel,
        grid=(m // FFN_TM, f // FFN_TF),
        in_specs=[
            pl.BlockSpec((FFN_TM, d), lambda i, j: (i, 0)),
            pl.BlockSpec((d, FFN_TF), lambda i, j: (0, j)),
            pl.BlockSpec((FFN_TF, d), lambda i, j: (j, 0)),
        ],
        out_specs=pl.BlockSpec((FFN_TM, d), lambda i, j: (i, 0)),
        out_shape=jax.ShapeDtypeStruct((m, d), F32),
        compiler_params=pltpu.CompilerParams(
            dimension_semantics=("arbitrary", "arbitrary"), vmem_limit_bytes=VMEM_LIMIT),
        name="ffn",
    )(hn, w1, w2)


def _ple_kernel(h_ref, ff_ref, p_ref, gff_ref, ggate_ref, wgate_ref, wproj_ref, gpost_ref, out_ref,
                hn_ref):
    groups = _row_groups(h_ref.shape[0])
    for rs in groups:
        h = h_ref[rs, :] + _rms(ff_ref[rs, :], gff_ref[...])
        out_ref[rs, :] = h
        hn_ref[rs, :] = _rms(h, ggate_ref[...]).astype(BF16)
    for rs in groups:
        gate = _sigmoid(jnp.dot(hn_ref[rs, :], wgate_ref[...], preferred_element_type=F32))
        e = jnp.dot(p_ref[rs, :].astype(BF16), wproj_ref[...], preferred_element_type=F32)
        out_ref[rs, :] += _rms(e * gate, gpost_ref[...])


def _ple(h, ff, p2, g_ff_post, g_gate, w_gate, w_proj, g_post):
    m, d = h.shape
    pd = p2.shape[1]
    tm = PLE_TM
    const = lambda shape: pl.BlockSpec(shape, lambda i: (0,) * len(shape),
                                       pipeline_mode=pl.Buffered(1))
    return pl.pallas_call(
        _ple_kernel,
        grid=(m // tm,),
        in_specs=[
            pl.BlockSpec((tm, d), lambda i: (i, 0)),
            pl.BlockSpec((tm, d), lambda i: (i, 0)),
            pl.BlockSpec((tm, pd), lambda i: (i, 0)),
            const((1, d)),
            const((1, d)),
            const(w_gate.shape),
            const(w_proj.shape),
            const((1, d)),
        ],
        out_specs=pl.BlockSpec((tm, d), lambda i: (i, 0)),
        out_shape=jax.ShapeDtypeStruct((m, d), F32),
        scratch_shapes=[pltpu.VMEM((tm, d), BF16)],
        compiler_params=pltpu.CompilerParams(
            dimension_semantics=("arbitrary",), vmem_limit_bytes=VMEM_LIMIT),
        name="ple",
    )(h, ff, p2, g_ff_post, g_gate, w_gate, w_proj, g_post)


def _tile_qk_columns(w_q, w_k):
    r = w_q.shape[0]
    a = jnp.stack([w_q.reshape(r, COL_TILES, PAIR), w_k.reshape(r, COL_TILES, PAIR)], axis=2)
    return a.reshape(r, 2 * MLSTM_WIDTH)


def _layer(h, p2, seq, w_in, b_gates, w_qk_conv, g_mlstm, w_pool, pool_scale, w_out,
           g_mix_pre, g_mix_post, w_ff1, w_ff2, g_ff_pre, g_ff_post, w_ple_proj, w_ple_gate,
           g_ple_gate, g_ple_post):
    m, d = h.shape
    assert seq % ROW_TILE == 0 and seq % MIX_TM == 0 and m % FFN_TM == 0 and m % PLE_TM == 0
    assert w_in.shape == (d, 4 * MLSTM_WIDTH + 2 * HEADS + POOL_WIDTH) and w_ff1.shape[1] % FFN_TF == 0
    qkvo = 4 * MLSTM_WIDTH
    row = lambda v: v.reshape(1, -1).astype(F32)
    w_in16 = w_in.astype(BF16)
    w_u16 = w_in16[:, qkvo + 2 * HEADS:]
    w_gi = w_in[:, qkvo:qkvo + HEADS]
    w_gf = w_in[:, qkvo + HEADS:qkvo + 2 * HEADS]
    lane_pad = lambda w: jnp.pad(w, ((0, 0), (0, LANES - HEADS)))
    w_gcol = jnp.concatenate([lane_pad(w_gi), lane_pad(w_gf)], axis=1).astype(BF16)
    b_col = jnp.concatenate([lane_pad(b_gates[None, :HEADS]), lane_pad(b_gates[None, HEADS:])],
                            axis=1).astype(F32)
    b_row = b_gates.reshape(2 * HEADS, 1).astype(F32)
    wqk = _tile_qk_columns(w_qk_conv[:, :MLSTM_WIDTH], w_qk_conv[:, MLSTM_WIDTH:]).astype(F32)

    u, hml = _proj_mlstm(h, row(g_mix_pre), w_in16, w_u16, w_gcol, b_col, b_row, wqk,
                         row(g_mlstm), seq)
    w_out16 = w_out.astype(BF16)
    h, hn, w_ff2_16, w_gate16 = _mix(
        hml, u, h, w_pool.astype(BF16), row(pool_scale), w_out16[:MLSTM_WIDTH],
        w_out16[MLSTM_WIDTH:], row(g_mix_post), row(g_ff_pre), (w_ff2, w_ple_gate), seq)
    ff = _ffn(hn, w_ff1, w_ff2_16)
    return _ple(h, ff, p2, row(g_ff_post), row(g_ple_gate), w_gate16,
                w_ple_proj.astype(BF16), row(g_ple_post))


def kernel(x, p, w_in, b_gates, w_qk_conv, g_mlstm, w_pool, pool_scale, w_out, g_mix_pre,
           g_mix_post, w_ff1, w_ff2, g_ff_pre, g_ff_post, w_ple_proj, w_ple_gate, g_ple_gate,
           g_ple_post):
    batch, seq, d = x.shape
    h = x.reshape(batch * seq, d)
    for i in range(p.shape[0]):
        h = _layer(h, p[i].reshape(batch * seq, -1), seq, w_in[i], b_gates[i],
                   w_qk_conv[i], g_mlstm[i], w_pool[i], pool_scale[i], w_out[i], g_mix_pre[i],
                   g_mix_post[i], w_ff1[i], w_ff2[i], g_ff_pre[i], g_ff_post[i], w_ple_proj[i],
                   w_ple_gate[i], g_ple_gate[i], g_ple_post[i])
    return h.reshape(batch, seq, d)
```

```python
import functools

import jax
import jax.numpy as jnp
from jax import lax
from jax.experimental import pallas as pl
from jax.experimental.pallas import tpu as pltpu

EPS = 1e-6
HEADS = 8
HEAD_DIM = 128
MLSTM_WIDTH = HEADS * HEAD_DIM
QK_CONV = 4
POOL_WINDOWS = (2, 4, 8, 16)
POOL_GROUP_WIDTH = 256
POOL_WIDTH = POOL_GROUP_WIDTH * len(POOL_WINDOWS)

LANES = 128
SUBLANES = 8
POOL_HALO = 16
CHUNK = 128
VMEM_LIMIT = 56 * 1024 * 1024

COL_TILES = 4
HEADS_PER_TILE = HEADS // COL_TILES
PAIR = HEADS_PER_TILE * HEAD_DIM
TILE_COLS = 4 * PAIR
ROW_TILE = COL_TILES * CHUNK

MIX_TM = 512
PLE_TM = 512
FFN_TM = 1024
FFN_TF = 1024
ROW_SPLIT = 2

F32 = jnp.float32
BF16 = jnp.bfloat16


def _rms(x, g):
    return x * lax.rsqrt(jnp.mean(x * x, axis=-1, keepdims=True) + EPS) * g


def _log_sigmoid(x):
    return jnp.minimum(x, 0.0) - jnp.log1p(jnp.exp(-jnp.abs(x)))


def _sigmoid(x):
    return 1.0 / (1.0 + jnp.exp(-x))


def _row_groups(n):
    rows = n // ROW_SPLIT
    return [slice(r * rows, (r + 1) * rows) for r in range(ROW_SPLIT)]


def _mlstm_chunk(first, proj_ref, gc, gr, bc_ref, br_ref, wqk_ref, gn_ref, out_ref, ct_ref, m_ref,
                 xe_ref, between):
    L = CHUNK
    reset = lambda v: jnp.where(first, 0.0, v)
    between(0)

    xe_ref[0:SUBLANES, :] = reset(xe_ref[0:SUBLANES, :])
    qk = []
    for t in range(COL_TILES):
        cs = slice(t * 2 * PAIR, (t + 1) * 2 * PAIR)
        x = proj_ref[t, :, 0:2 * PAIR]
        xe_ref[SUBLANES:SUBLANES + L, cs] = x
        acc = x * wqk_ref[QK_CONV - 1:QK_CONV, cs]
        for back in range(1, QK_CONV):
            tap = QK_CONV - 1 - back
            acc = acc + xe_ref[SUBLANES - back:SUBLANES - back + L, cs] * wqk_ref[tap:tap + 1, cs]
        xe_ref[0:SUBLANES, cs] = x[L - SUBLANES:L, :]
        qk.append(acc * _sigmoid(acc))

    gc = gc + bc_ref[...]
    li_c = gc[:, :LANES]
    lf_c = _log_sigmoid(gc[:, LANES:])
    gr = gr + br_ref[...]
    li_r = gr[0:SUBLANES, :]
    lf_r = _log_sigmoid(gr[SUBLANES:, :])
    row = lax.broadcasted_iota(jnp.int32, (L, L), 0)
    col = lax.broadcasted_iota(jnp.int32, (L, L), 1)
    causal = row >= col
    tril = jnp.where(causal, 1.0, 0.0).astype(F32)
    triu = jnp.where(row <= col, 1.0, 0.0).astype(F32)
    b_c = jnp.dot(tril, lf_c, preferred_element_type=F32, precision=lax.Precision.HIGHEST)
    b_r = jnp.dot(lf_r, triu, preferred_element_type=F32, precision=lax.Precision.HIGHEST)
    r_r = li_r - b_r
    btot = b_c[L - 1:L, :]
    a_c = btot - b_c + li_c
    m_prev = reset(m_ref[0:1, :])
    m_new = jnp.maximum(btot + m_prev, jnp.max(a_c, axis=0, keepdims=True))
    decay = jnp.exp(btot + m_prev - m_new)
    w_c = jnp.exp(a_c - m_new)
    inter_log = b_c + m_prev
    m_ref[0:1, :] = m_new

    ones_col = jnp.where(lax.broadcasted_iota(jnp.int32, (L, LANES), 1) == 0, 1.0, 0.0).astype(F32)

    for h in range(HEADS):
        t, e = divmod(h, HEADS_PER_TILE)
        if e == 0:
            between(t + 1)
        es = slice(e * HEAD_DIM, (e + 1) * HEAD_DIM)
        sl = slice(h * HEAD_DIM, (h + 1) * HEAD_DIM)
        qf = qk[t][:, es]
        qh = qf.astype(BF16)
        kh = (qk[t][:, PAIR + e * HEAD_DIM:PAIR + (e + 1) * HEAD_DIM] * (HEAD_DIM ** -0.5)).astype(BF16)
        vh = proj_ref[t, :, 2 * PAIR + e * HEAD_DIM:2 * PAIR + (e + 1) * HEAD_DIM]
        oh = proj_ref[t, :, 3 * PAIR + e * HEAD_DIM:3 * PAIR + (e + 1) * HEAD_DIM]
        vaug = jnp.concatenate([vh, ones_col], axis=1)
        dlog = jnp.where(causal, b_c[:, h:h + 1] + r_r[h:h + 1, :], -jnp.inf)
        inter = inter_log[:, h:h + 1]
        m_t = jnp.maximum(inter, jnp.max(dlog, axis=1, keepdims=True))
        dw = jnp.exp(dlog - m_t)
        inter_w = jnp.exp(inter - m_t)
        s = lax.dot_general(qh, kh, (((1,), (1,)), ((), ())), preferred_element_type=F32) * dw
        ct = reset(ct_ref[h])
        lhs = jnp.concatenate([s.astype(BF16), (inter_w * qf).astype(BF16)], axis=1)
        rhs = jnp.concatenate([vaug.astype(BF16), ct.astype(BF16)], axis=0)
        tot = jnp.dot(lhs, rhs, preferred_element_type=F32)
        num = tot[:, :HEAD_DIM]
        den = tot[:, HEAD_DIM:HEAD_DIM + 1]
        cell = num / jnp.maximum(jnp.abs(den), jnp.exp(-m_t))
        cell = _rms(cell, gn_ref[:, sl])
        out_ref[:, sl] = (_sigmoid(oh) * cell).astype(out_ref.dtype)
        wv = (w_c[:, h:h + 1] * vaug).astype(BF16)
        upd = lax.dot_general(kh, wv, (((0,), (0,)), ((), ())), preferred_element_type=F32)
        ct_ref[h] = decay[:, h:h + 1] * ct + upd


def _proj_mlstm_kernel(x0_ref, xnext_ref, g_ref, wq_ref, wk_ref, wv_ref, wo_ref, wu_ref, wgc_ref,
                       bc_ref, br_ref, wqk_ref, gn_ref,
                       u_ref, hml_ref, xn_a, xn_b, proj_a, proj_b, gcol_a, gcol_b, grow_a, grow_b,
                       ct_ref, m_ref, xe_ref, *, tiles_per_seq):
    w_refs = (wq_ref, wk_ref, wv_ref, wo_ref, wu_ref)
    i = pl.program_id(0)
    j = pl.program_id(1)

    @pl.when((i == 0) & (j == 0))
    def _():
        xn_a[...] = _rms(x0_ref[...], g_ref[...]).astype(BF16)
        proj_b[...] = jnp.zeros_like(proj_b)
        gcol_b[...] = jnp.zeros_like(gcol_b)
        grow_b[...] = jnp.zeros_like(grow_b)
        ct_ref[...] = jnp.zeros_like(ct_ref)
        m_ref[...] = jnp.zeros_like(m_ref)
        xe_ref[...] = jnp.zeros_like(xe_ref)

    def step(xn_cur, xn_nxt, proj_w, gcol_w, grow_w, proj_r, gcol_r, grow_r):
        r0 = pl.multiple_of(j * CHUNK, CHUNK)

        def project_piece(k):
            y = jnp.dot(xn_cur[...], w_refs[k][...], preferred_element_type=F32)
            if k < 4:
                proj_w[j, :, k * PAIR:(k + 1) * PAIR] = y
            else:
                u_ref[...] = y

        first = ((i + tiles_per_seq - 1) % tiles_per_seq == 0) & (j == 0)
        _mlstm_chunk(first, proj_r.at[:, pl.ds(r0, CHUNK), :], gcol_r[pl.ds(r0, CHUNK), :],
                     grow_r[j], bc_ref, br_ref, wqk_ref, gn_ref, hml_ref, ct_ref, m_ref, xe_ref,
                     project_piece)

        gc = jnp.dot(xn_cur[pl.ds(r0, CHUNK), :], wgc_ref[...], preferred_element_type=F32)
        gcol_w[pl.ds(r0, CHUNK), :] = gc
        grow_w[j] = jnp.concatenate(
            [gc[:, :LANES].T[0:SUBLANES, :], gc[:, LANES:].T[0:SUBLANES, :]], axis=0)

        xn_nxt[pl.ds(r0, CHUNK), :] = _rms(xnext_ref[...], g_ref[...]).astype(BF16)

    @pl.when(i % 2 == 0)
    def _():
        step(xn_a, xn_b, proj_a, gcol_a, grow_a, proj_b, gcol_b, grow_b)

    @pl.when(i % 2 == 1)
    def _():
        step(xn_b, xn_a, proj_b, gcol_b, grow_b, proj_a, gcol_a, grow_a)


def _proj_mlstm(x2, g, w_in16, w_u16, w_gcol, b_col, b_row, wqk, gn, seq):
    m, d = x2.shape
    nt = m // ROW_TILE
    last_chunk = m // CHUNK - 1
    const = lambda shape: pl.BlockSpec(shape, lambda i, j: (0,) * len(shape),
                                       pipeline_mode=pl.Buffered(1))
    slab = lambda s: pl.BlockSpec((d, PAIR), lambda i, j: (0, s * COL_TILES + j))
    return pl.pallas_call(
        functools.partial(_proj_mlstm_kernel, tiles_per_seq=seq // ROW_TILE),
        grid=(nt + 1, COL_TILES),
        in_specs=[
            const((ROW_TILE, d)),
            pl.BlockSpec((CHUNK, d),
                         lambda i, j: (jnp.minimum((i + 1) * COL_TILES + j, last_chunk), 0)),
            const((1, d)),
            slab(0), slab(1), slab(2), slab(3),
            pl.BlockSpec((d, POOL_GROUP_WIDTH), lambda i, j: (0, j)),
            const((d, 2 * LANES)),
            const((1, 2 * LANES)),
            const((2 * SUBLANES, 1)),
            const((QK_CONV, 2 * MLSTM_WIDTH)),
            const((1, MLSTM_WIDTH)),
        ],
        out_specs=[
            pl.BlockSpec((ROW_TILE, POOL_GROUP_WIDTH), lambda i, j: (i, j)),
            pl.BlockSpec((CHUNK, MLSTM_WIDTH),
                         lambda i, j: (jnp.maximum((i - 1) * COL_TILES + j, 0), 0)),
        ],
        out_shape=[
            jax.ShapeDtypeStruct((m + ROW_TILE, POOL_WIDTH), F32),
            jax.ShapeDtypeStruct((m, MLSTM_WIDTH), BF16),
        ],
        scratch_shapes=[
            pltpu.VMEM((ROW_TILE, d), BF16),
            pltpu.VMEM((ROW_TILE, d), BF16),
            pltpu.VMEM((COL_TILES, ROW_TILE, TILE_COLS), F32),
            pltpu.VMEM((COL_TILES, ROW_TILE, TILE_COLS), F32),
            pltpu.VMEM((ROW_TILE, 2 * LANES), F32),
            pltpu.VMEM((ROW_TILE, 2 * LANES), F32),
            pltpu.VMEM((COL_TILES, 2 * SUBLANES, CHUNK), F32),
            pltpu.VMEM((COL_TILES, 2 * SUBLANES, CHUNK), F32),
            pltpu.VMEM((HEADS, HEAD_DIM, 2 * HEAD_DIM), F32),
            pltpu.VMEM((SUBLANES, LANES), F32),
            pltpu.VMEM((SUBLANES + CHUNK, 2 * MLSTM_WIDTH), F32),
        ],
        compiler_params=pltpu.CompilerParams(
            dimension_semantics=("arbitrary", "arbitrary"), vmem_limit_bytes=VMEM_LIMIT),
        name="proj_mlstm",
    )(x2, x2, g, w_in16, w_in16, w_in16, w_in16, w_u16, w_gcol, b_col, b_row, wqk, gn)


def _mix_kernel(hml_ref, u_ref, x_ref, wpool_ref, pscale_ref, wo_ml_ref, wo_pool_ref, g_ref,
                gnext_ref, wnext_ref, out_ref, hn_ref, wnext16_ref, ue_ref, *, tiles_per_seq):
    tm = u_ref.shape[0]
    t = pl.program_id(0) % tiles_per_seq
    wnext16_ref[...] = wnext_ref[...].astype(BF16)

    @pl.when(t == 0)
    def _():
        ue_ref[0:POOL_HALO, :] = jnp.zeros((POOL_HALO, POOL_WIDTH), F32)

    u = u_ref[...]
    ue_ref[POOL_HALO:POOL_HALO + tm, :] = u
    pos = (lax.broadcasted_iota(jnp.int32, (tm, 1), 0) + (t * tm + 1)).astype(F32)
    mixed = []
    for g, window in enumerate(POOL_WINDOWS):
        sl = slice(g * POOL_GROUP_WIDTH, (g + 1) * POOL_GROUP_WIDTH)
        win = ue_ref[:, sl]
        span = 1
        while span < window:
            win = win + pltpu.roll(win, span, axis=0)
            span *= 2
        mean = win[POOL_HALO:, :] / jnp.minimum(pos, float(window))
        pooled = (mean - u[:, sl]).astype(BF16)
        mixed.append((jnp.dot(pooled, wpool_ref[g], preferred_element_type=F32)
                      * pscale_ref[:, sl]).astype(BF16))
    ue_ref[0:POOL_HALO, :] = u[tm - POOL_HALO:tm, :]
    hpool = jnp.concatenate(mixed, axis=1)
    for rs in _row_groups(tm):
        mix = (jnp.dot(hml_ref[rs, :], wo_ml_ref[...], preferred_element_type=F32)
               + jnp.dot(hpool[rs, :], wo_pool_ref[...], preferred_element_type=F32))
        h = x_ref[rs, :] + _rms(mix, g_ref[...])
        out_ref[rs, :] = h
        hn_ref[rs, :] = _rms(h, gnext_ref[...]).astype(BF16)


def _mix(hml, u, x2, w_pool, pool_scale, wo_ml, wo_pool, g_post, g_next, w_next, seq):
    m, d = x2.shape
    tm = MIX_TM
    wr = w_next.shape[0] // (m // tm)
    const = lambda shape: pl.BlockSpec(shape, lambda i: (0,) * len(shape),
                                       pipeline_mode=pl.Buffered(1))
    return pl.pallas_call(
        functools.partial(_mix_kernel, tiles_per_seq=seq // tm),
        grid=(m // tm,),
        in_specs=[
            pl.BlockSpec((tm, MLSTM_WIDTH), lambda i: (i, 0)),
            pl.BlockSpec((tm, POOL_WIDTH), lambda i: (i, 0)),
            pl.BlockSpec((tm, d), lambda i: (i, 0)),
            const(w_pool.shape),
            const((1, POOL_WIDTH)),
            const(wo_ml.shape),
            const(wo_pool.shape),
            const((1, d)),
            const((1, d)),
            pl.BlockSpec((wr, w_next.shape[1]), lambda i: (i, 0)),
        ],
        out_specs=[pl.BlockSpec((tm, d), lambda i: (i, 0)), pl.BlockSpec((tm, d), lambda i: (i, 0)),
                   pl.BlockSpec((wr, w_next.shape[1]), lambda i: (i, 0))],
        out_shape=[jax.ShapeDtypeStruct((m, d), F32), jax.ShapeDtypeStruct((m, d), BF16),
                   jax.ShapeDtypeStruct(w_next.shape, BF16)],
        scratch_shapes=[pltpu.VMEM((POOL_HALO + tm, POOL_WIDTH), F32)],
        compiler_params=pltpu.CompilerParams(
            dimension_semantics=("arbitrary",), vmem_limit_bytes=VMEM_LIMIT),
        name="mix",
    )(hml, u, x2, w_pool, pool_scale, wo_ml, wo_pool, g_post, g_next, w_next)


def _ffn_kernel(hn_ref, w1_ref, w2_ref, acc_ref):
    j = pl.program_id(1)
    groups = _row_groups(hn_ref.shape[0])
    w1 = w1_ref[...].astype(BF16)
    acts = []
    for g in groups:
        a = jnp.dot(hn_ref[g, :], w1, preferred_element_type=F32)
        acts.append(jnp.square(jnp.maximum(a, 0.0)).astype(BF16))
    for g, a in zip(groups, acts):
        prev = jnp.where(j == 0, 0.0, acc_ref[g, :])
        acc_ref[g, :] = prev + jnp.dot(a, w2_ref[...], preferred_element_type=F32)


def _ffn(hn, w1, w2):
    m, d = hn.shape
    f = w1.shape[1]
    return pl.pallas_call(
        _ffn_kernel,
        grid=(m // FFN_TM, f // FFN_TF),
        in_specs=[
            pl.BlockSpec((FFN_TM, d), lambda i, j: (i, 0)),
            pl.BlockSpec((d, FFN_TF), lambda i, j: (0, j)),
            pl.BlockSpec((FFN_TF, d), lambda i, j: (j, 0)),
        ],
        out_specs=pl.BlockSpec((FFN_TM, d), lambda i, j: (i, 0)),
        out_shape=jax.ShapeDtypeStruct((m, d), F32),
        compiler_params=pltpu.CompilerParams(
            dimension_semantics=("arbitrary", "arbitrary"), vmem_limit_bytes=VMEM_LIMIT),
        name="ffn",
    )(hn, w1, w2)


def _ple_kernel(h_ref, ff_ref, p_ref, gff_ref, ggate_ref, wgate_ref, wproj_ref, gpost_ref, out_ref,
                hn_ref):
    groups = _row_groups(h_ref.shape[0])
    for rs in groups:
        h = h_ref[rs, :] + _rms(ff_ref[rs, :], gff_ref[...])
        out_ref[rs, :] = h
        hn_ref[rs, :] = _rms(h, ggate_ref[...]).astype(BF16)
    for rs in groups:
        gate = _sigmoid(jnp.dot(hn_ref[rs, :], wgate_ref[...], preferred_element_type=F32))
        e = jnp.dot(p_ref[rs, :].astype(BF16), wproj_ref[...], preferred_element_type=F32)
        out_ref[rs, :] += _rms(e * gate, gpost_ref[...])


def _ple(h, ff, p2, g_ff_post, g_gate, w_gate, w_proj, g_post):
    m, d = h.shape
    pd = p2.shape[1]
    tm = PLE_TM
    const = lambda shape: pl.BlockSpec(shape, lambda i: (0,) * len(shape),
                                       pipeline_mode=pl.Buffered(1))
    return pl.pallas_call(
        _ple_kernel,
        grid=(m // tm,),
        in_specs=[
            pl.BlockSpec((tm, d), lambda i: (i, 0)),
            pl.BlockSpec((tm, d), lambda i: (i, 0)),
            pl.BlockSpec((tm, pd), lambda i: (i, 0)),
            const((1, d)),
            const((1, d)),
            const(w_gate.shape),
            const(w_proj.shape),
            const((1, d)),
        ],
        out_specs=pl.BlockSpec((tm, d), lambda i: (i, 0)),
        out_shape=jax.ShapeDtypeStruct((m, d), F32),
        scratch_shapes=[pltpu.VMEM((tm, d), BF16)],
        compiler_params=pltpu.CompilerParams(
            dimension_semantics=("arbitrary",), vmem_limit_bytes=VMEM_LIMIT),
        name="ple",
    )(h, ff, p2, g_ff_post, g_gate, w_gate, w_proj, g_post)


def _tile_qk_columns(w_q, w_k):
    r = w_q.shape[0]
    a = jnp.stack([w_q.reshape(r, COL_TILES, PAIR), w_k.reshape(r, COL_TILES, PAIR)], axis=2)
    return a.reshape(r, 2 * MLSTM_WIDTH)


def _layer(h, p2, seq, w_in, b_gates, w_qk_conv, g_mlstm, w_pool, pool_scale, w_out,
           g_mix_pre, g_mix_post, w_ff1, w_ff2, g_ff_pre, g_ff_post, w_ple_proj, w_ple_gate,
           g_ple_gate, g_ple_post):
    m, d = h.shape
    assert seq % ROW_TILE == 0 and seq % MIX_TM == 0 and m % FFN_TM == 0 and m % PLE_TM == 0
    assert w_in.shape == (d, 4 * MLSTM_WIDTH + 2 * HEADS + POOL_WIDTH) and w_ff1.shape[1] % FFN_TF == 0
    qkvo = 4 * MLSTM_WIDTH
    row = lambda v: v.reshape(1, -1).astype(F32)
    w_in16 = w_in.astype(BF16)
    w_u16 = w_in16[:, qkvo + 2 * HEADS:]
    w_gi = w_in[:, qkvo:qkvo + HEADS]
    w_gf = w_in[:, qkvo + HEADS:qkvo + 2 * HEADS]
    lane_pad = lambda w: jnp.pad(w, ((0, 0), (0, LANES - HEADS)))
    w_gcol = jnp.concatenate([lane_pad(w_gi), lane_pad(w_gf)], axis=1).astype(BF16)
    b_col = jnp.concatenate([lane_pad(b_gates[None, :HEADS]), lane_pad(b_gates[None, HEADS:])],
                            axis=1).astype(F32)
    b_row = b_gates.reshape(2 * HEADS, 1).astype(F32)
    wqk = _tile_qk_columns(w_qk_conv[:, :MLSTM_WIDTH], w_qk_conv[:, MLSTM_WIDTH:]).astype(F32)

    u, hml = _proj_mlstm(h, row(g_mix_pre), w_in16, w_u16, w_gcol, b_col, b_row, wqk,
                         row(g_mlstm), seq)
    w_out16 = w_out.astype(BF16)
    h, hn, w_ff2_16 = _mix(hml, u, h, w_pool.astype(BF16), row(pool_scale), w_out16[:MLSTM_WIDTH],
                           w_out16[MLSTM_WIDTH:], row(g_mix_post), row(g_ff_pre), w_ff2, seq)
    ff = _ffn(hn, w_ff1, w_ff2_16)
    return _ple(h, ff, p2, row(g_ff_post), row(g_ple_gate), w_ple_gate.astype(BF16),
                w_ple_proj.astype(BF16), row(g_ple_post))


def kernel(x, p, w_in, b_gates, w_qk_conv, g_mlstm, w_pool, pool_scale, w_out, g_mix_pre,
           g_mix_post, w_ff1, w_ff2, g_ff_pre, g_ff_post, w_ple_proj, w_ple_gate, g_ple_gate,
           g_ple_post):
    batch, seq, d = x.shape
    h = x.reshape(batch * seq, d)
    for i in range(p.shape[0]):
        h = _layer(h, p[i].reshape(batch * seq, -1), seq, w_in[i], b_gates[i],
                   w_qk_conv[i], g_mlstm[i], w_pool[i], pool_scale[i], w_out[i], g_mix_pre[i],
                   g_mix_post[i], w_ff1[i], w_ff2[i], g_ff_pre[i], g_ff_post[i], w_ple_proj[i],
                   w_ple_gate[i], g_ple_gate[i], g_ple_post[i])
    return h.reshape(batch, seq, d)
```

```python
import functools

import jax
import jax.numpy as jnp
from jax import lax
from jax.experimental import pallas as pl
from jax.experimental.pallas import tpu as pltpu

EPS = 1e-6
HEADS = 8
HEAD_DIM = 128
MLSTM_WIDTH = HEADS * HEAD_DIM
QK_CONV = 4
POOL_WINDOWS = (2, 4, 8, 16)
POOL_GROUP_WIDTH = 256
POOL_WIDTH = POOL_GROUP_WIDTH * len(POOL_WINDOWS)

LANES = 128
SUBLANES = 8
POOL_HALO = 16
CHUNK = 128
VMEM_LIMIT = 56 * 1024 * 1024

COL_TILES = 4
HEADS_PER_TILE = HEADS // COL_TILES
PAIR = HEADS_PER_TILE * HEAD_DIM
TILE_COLS = 4 * PAIR
ROW_TILE = COL_TILES * CHUNK

MIX_TM = 512
PLE_TM = 512
FFN_TM = 1024
FFN_TF = 1024
ROW_SPLIT = 2

F32 = jnp.float32
BF16 = jnp.bfloat16


def _rms(x, g):
    return x * lax.rsqrt(jnp.mean(x * x, axis=-1, keepdims=True) + EPS) * g


def _log_sigmoid(x):
    return jnp.minimum(x, 0.0) - jnp.log1p(jnp.exp(-jnp.abs(x)))


def _sigmoid(x):
    return 1.0 / (1.0 + jnp.exp(-x))


def _row_groups(n):
    rows = n // ROW_SPLIT
    return [slice(r * rows, (r + 1) * rows) for r in range(ROW_SPLIT)]


def _mlstm_chunk(first, proj_ref, gc, gr, bc_ref, br_ref, wqk_ref, gn_ref, out_ref, ct_ref, m_ref,
                 xe_ref, between):
    L = CHUNK
    reset = lambda v: jnp.where(first, 0.0, v)
    between(0)

    xe_ref[0:SUBLANES, :] = reset(xe_ref[0:SUBLANES, :])
    qk = []
    for t in range(COL_TILES):
        cs = slice(t * 2 * PAIR, (t + 1) * 2 * PAIR)
        x = proj_ref[t, :, 0:2 * PAIR]
        xe_ref[SUBLANES:SUBLANES + L, cs] = x
        acc = x * wqk_ref[QK_CONV - 1:QK_CONV, cs]
        for back in range(1, QK_CONV):
            tap = QK_CONV - 1 - back
            acc = acc + xe_ref[SUBLANES - back:SUBLANES - back + L, cs] * wqk_ref[tap:tap + 1, cs]
        xe_ref[0:SUBLANES, cs] = x[L - SUBLANES:L, :]
        qk.append(acc * _sigmoid(acc))

    gc = gc + bc_ref[...]
    li_c = gc[:, :LANES]
    lf_c = _log_sigmoid(gc[:, LANES:])
    gr = gr + br_ref[...]
    li_r = gr[0:SUBLANES, :]
    lf_r = _log_sigmoid(gr[SUBLANES:, :])
    row = lax.broadcasted_iota(jnp.int32, (L, L), 0)
    col = lax.broadcasted_iota(jnp.int32, (L, L), 1)
    causal = row >= col
    tril = jnp.where(causal, 1.0, 0.0).astype(F32)
    triu = jnp.where(row <= col, 1.0, 0.0).astype(F32)
    b_c = jnp.dot(tril, lf_c, preferred_element_type=F32, precision=lax.Precision.HIGHEST)
    b_r = jnp.dot(lf_r, triu, preferred_element_type=F32, precision=lax.Precision.HIGHEST)
    r_r = li_r - b_r
    btot = b_c[L - 1:L, :]
    a_c = btot - b_c + li_c
    m_prev = reset(m_ref[0:1, :])
    m_new = jnp.maximum(btot + m_prev, jnp.max(a_c, axis=0, keepdims=True))
    decay = jnp.exp(btot + m_prev - m_new)
    w_c = jnp.exp(a_c - m_new)
    inter_log = b_c + m_prev
    m_ref[0:1, :] = m_new

    ones_col = jnp.where(lax.broadcasted_iota(jnp.int32, (L, LANES), 1) == 0, 1.0, 0.0).astype(F32)

    for h in range(HEADS):
        t, e = divmod(h, HEADS_PER_TILE)
        if e == 0:
            between(t + 1)
        es = slice(e * HEAD_DIM, (e + 1) * HEAD_DIM)
        sl = slice(h * HEAD_DIM, (h + 1) * HEAD_DIM)
        qf = qk[t][:, es]
        qh = qf.astype(BF16)
        kh = (qk[t][:, PAIR + e * HEAD_DIM:PAIR + (e + 1) * HEAD_DIM] * (HEAD_DIM ** -0.5)).astype(BF16)
        vh = proj_ref[t, :, 2 * PAIR + e * HEAD_DIM:2 * PAIR + (e + 1) * HEAD_DIM]
        oh = proj_ref[t, :, 3 * PAIR + e * HEAD_DIM:3 * PAIR + (e + 1) * HEAD_DIM]
        vaug = jnp.concatenate([vh, ones_col], axis=1)
        dlog = jnp.where(causal, b_c[:, h:h + 1] + r_r[h:h + 1, :], -jnp.inf)
        inter = inter_log[:, h:h + 1]
        m_t = jnp.maximum(inter, jnp.max(dlog, axis=1, keepdims=True))
        dw = jnp.exp(dlog - m_t)
        inter_w = jnp.exp(inter - m_t)
        s = lax.dot_general(qh, kh, (((1,), (1,)), ((), ())), preferred_element_type=F32) * dw
        ct = reset(ct_ref[h])
        lhs = jnp.concatenate([s.astype(BF16), (inter_w * qf).astype(BF16)], axis=1)
        rhs = jnp.concatenate([vaug.astype(BF16), ct.astype(BF16)], axis=0)
        tot = jnp.dot(lhs, rhs, preferred_element_type=F32)
        num = tot[:, :HEAD_DIM]
        den = tot[:, HEAD_DIM:HEAD_DIM + 1]
        cell = num / jnp.maximum(jnp.abs(den), jnp.exp(-m_t))
        cell = _rms(cell, gn_ref[:, sl])
        out_ref[:, sl] = (_sigmoid(oh) * cell).astype(out_ref.dtype)
        wv = (w_c[:, h:h + 1] * vaug).astype(BF16)
        upd = lax.dot_general(kh, wv, (((0,), (0,)), ((), ())), preferred_element_type=F32)
        ct_ref[h] = decay[:, h:h + 1] * ct + upd


def _proj_mlstm_kernel(x0_ref, xnext_ref, g_ref, wq_ref, wk_ref, wv_ref, wo_ref, wu_ref, wgc_ref,
                       bc_ref, br_ref, wqk_ref, gn_ref,
                       u_ref, hml_ref, xn_a, xn_b, proj_a, proj_b, gcol_a, gcol_b, grow_a, grow_b,
                       ct_ref, m_ref, xe_ref, *, tiles_per_seq):
    w_refs = (wq_ref, wk_ref, wv_ref, wo_ref, wu_ref)
    i = pl.program_id(0)
    j = pl.program_id(1)

    @pl.when((i == 0) & (j == 0))
    def _():
        xn_a[...] = _rms(x0_ref[...], g_ref[...]).astype(BF16)
        proj_b[...] = jnp.zeros_like(proj_b)
        gcol_b[...] = jnp.zeros_like(gcol_b)
        grow_b[...] = jnp.zeros_like(grow_b)
        ct_ref[...] = jnp.zeros_like(ct_ref)
        m_ref[...] = jnp.zeros_like(m_ref)
        xe_ref[...] = jnp.zeros_like(xe_ref)

    def step(xn_cur, xn_nxt, proj_w, gcol_w, grow_w, proj_r, gcol_r, grow_r):
        r0 = pl.multiple_of(j * CHUNK, CHUNK)

        def project_piece(k):
            y = jnp.dot(xn_cur[...], w_refs[k][...], preferred_element_type=F32)
            if k < 4:
                proj_w[j, :, k * PAIR:(k + 1) * PAIR] = y
            else:
                u_ref[...] = y

        first = ((i + tiles_per_seq - 1) % tiles_per_seq == 0) & (j == 0)
        _mlstm_chunk(first, proj_r.at[:, pl.ds(r0, CHUNK), :], gcol_r[pl.ds(r0, CHUNK), :],
                     grow_r[j], bc_ref, br_ref, wqk_ref, gn_ref, hml_ref, ct_ref, m_ref, xe_ref,
                     project_piece)

        gc = jnp.dot(xn_cur[pl.ds(r0, CHUNK), :], wgc_ref[...], preferred_element_type=F32)
        gcol_w[pl.ds(r0, CHUNK), :] = gc
        grow_w[j] = jnp.concatenate(
            [gc[:, :LANES].T[0:SUBLANES, :], gc[:, LANES:].T[0:SUBLANES, :]], axis=0)

        xn_nxt[pl.ds(r0, CHUNK), :] = _rms(xnext_ref[...], g_ref[...]).astype(BF16)

    @pl.when(i % 2 == 0)
    def _():
        step(xn_a, xn_b, proj_a, gcol_a, grow_a, proj_b, gcol_b, grow_b)

    @pl.when(i % 2 == 1)
    def _():
        step(xn_b, xn_a, proj_b, gcol_b, grow_b, proj_a, gcol_a, grow_a)


def _proj_mlstm(x2, g, w_in16, w_u16, w_gcol, b_col, b_row, wqk, gn, seq):
    m, d = x2.shape
    nt = m // ROW_TILE
    last_chunk = m // CHUNK - 1
    const = lambda shape: pl.BlockSpec(shape, lambda i, j: (0,) * len(shape),
                                       pipeline_mode=pl.Buffered(1))
    slab = lambda s: pl.BlockSpec((d, PAIR), lambda i, j: (0, s * COL_TILES + j))
    return pl.pallas_call(
        functools.partial(_proj_mlstm_kernel, tiles_per_seq=seq // ROW_TILE),
        grid=(nt + 1, COL_TILES),
        in_specs=[
            const((ROW_TILE, d)),
            pl.BlockSpec((CHUNK, d),
                         lambda i, j: (jnp.minimum((i + 1) * COL_TILES + j, last_chunk), 0)),
            const((1, d)),
            slab(0), slab(1), slab(2), slab(3),
            pl.BlockSpec((d, POOL_GROUP_WIDTH), lambda i, j: (0, j)),
            const((d, 2 * LANES)),
            const((1, 2 * LANES)),
            const((2 * SUBLANES, 1)),
            const((QK_CONV, 2 * MLSTM_WIDTH)),
            const((1, MLSTM_WIDTH)),
        ],
        out_specs=[
            pl.BlockSpec((ROW_TILE, POOL_GROUP_WIDTH), lambda i, j: (i, j)),
            pl.BlockSpec((CHUNK, MLSTM_WIDTH),
                         lambda i, j: (jnp.maximum((i - 1) * COL_TILES + j, 0), 0)),
        ],
        out_shape=[
            jax.ShapeDtypeStruct((m + ROW_TILE, POOL_WIDTH), F32),
            jax.ShapeDtypeStruct((m, MLSTM_WIDTH), BF16),
        ],
        scratch_shapes=[
            pltpu.VMEM((ROW_TILE, d), BF16),
            pltpu.VMEM((ROW_TILE, d), BF16),
            pltpu.VMEM((COL_TILES, ROW_TILE, TILE_COLS), F32),
            pltpu.VMEM((COL_TILES, ROW_TILE, TILE_COLS), F32),
            pltpu.VMEM((ROW_TILE, 2 * LANES), F32),
            pltpu.VMEM((ROW_TILE, 2 * LANES), F32),
            pltpu.VMEM((COL_TILES, 2 * SUBLANES, CHUNK), F32),
            pltpu.VMEM((COL_TILES, 2 * SUBLANES, CHUNK), F32),
            pltpu.VMEM((HEADS, HEAD_DIM, 2 * HEAD_DIM), F32),
            pltpu.VMEM((SUBLANES, LANES), F32),
            pltpu.VMEM((SUBLANES + CHUNK, 2 * MLSTM_WIDTH), F32),
        ],
        compiler_params=pltpu.CompilerParams(
            dimension_semantics=("arbitrary", "arbitrary"), vmem_limit_bytes=VMEM_LIMIT),
        name="proj_mlstm",
    )(x2, x2, g, w_in16, w_in16, w_in16, w_in16, w_u16, w_gcol, b_col, b_row, wqk, gn)


def _mix_kernel(hml_ref, u_ref, x_ref, wpool_ref, pscale_ref, wo_ml_ref, wo_pool_ref, g_ref,
                gnext_ref, wa_ref, wb_ref, out_ref, hn_ref, wa16_ref, wb16_ref, ue_ref, *,
                tiles_per_seq):
    tm = u_ref.shape[0]
    t = pl.program_id(0) % tiles_per_seq
    wa16_ref[...] = wa_ref[...].astype(BF16)
    wb16_ref[...] = wb_ref[...].astype(BF16)

    @pl.when(t == 0)
    def _():
        ue_ref[0:POOL_HALO, :] = jnp.zeros((POOL_HALO, POOL_WIDTH), F32)

    u = u_ref[...]
    ue_ref[POOL_HALO:POOL_HALO + tm, :] = u
    pos = (lax.broadcasted_iota(jnp.int32, (tm, 1), 0) + (t * tm + 1)).astype(F32)
    mixed = []
    for g, window in enumerate(POOL_WINDOWS):
        sl = slice(g * POOL_GROUP_WIDTH, (g + 1) * POOL_GROUP_WIDTH)
        win = ue_ref[:, sl]
        span = 1
        while span < window:
            win = win + pltpu.roll(win, span, axis=0)
            span *= 2
        mean = win[POOL_HALO:, :] / jnp.minimum(pos, float(window))
        pooled = (mean - u[:, sl]).astype(BF16)
        mixed.append((jnp.dot(pooled, wpool_ref[g], preferred_element_type=F32)
                      * pscale_ref[:, sl]).astype(BF16))
    ue_ref[0:POOL_HALO, :] = u[tm - POOL_HALO:tm, :]
    hpool = jnp.concatenate(mixed, axis=1)
    for rs in _row_groups(tm):
        mix = (jnp.dot(hml_ref[rs, :], wo_ml_ref[...], preferred_element_type=F32)
               + jnp.dot(hpool[rs, :], wo_pool_ref[...], preferred_element_type=F32))
        h = x_ref[rs, :] + _rms(mix, g_ref[...])
        out_ref[rs, :] = h
        hn_ref[rs, :] = _rms(h, gnext_ref[...]).astype(BF16)


def _mix(hml, u, x2, w_pool, pool_scale, wo_ml, wo_pool, g_post, g_next, w_next, seq):
    m, d = x2.shape
    tm = MIX_TM
    share = lambda w: pl.BlockSpec((w.shape[0] // (m // tm), w.shape[1]), lambda i: (i, 0))
    const = lambda shape: pl.BlockSpec(shape, lambda i: (0,) * len(shape),
                                       pipeline_mode=pl.Buffered(1))
    return pl.pallas_call(
        functools.partial(_mix_kernel, tiles_per_seq=seq // tm),
        grid=(m // tm,),
        in_specs=[
            pl.BlockSpec((tm, MLSTM_WIDTH), lambda i: (i, 0)),
            pl.BlockSpec((tm, POOL_WIDTH), lambda i: (i, 0)),
            pl.BlockSpec((tm, d), lambda i: (i, 0)),
            const(w_pool.shape),
            const((1, POOL_WIDTH)),
            const(wo_ml.shape),
            const(wo_pool.shape),
            const((1, d)),
            const((1, d)),
            share(w_next[0]), share(w_next[1]),
        ],
        out_specs=[pl.BlockSpec((tm, d), lambda i: (i, 0)), pl.BlockSpec((tm, d), lambda i: (i, 0)),
                   share(w_next[0]), share(w_next[1])],
        out_shape=[jax.ShapeDtypeStruct((m, d), F32), jax.ShapeDtypeStruct((m, d), BF16),
                   jax.ShapeDtypeStruct(w_next[0].shape, BF16),
                   jax.ShapeDtypeStruct(w_next[1].shape, BF16)],
        scratch_shapes=[pltpu.VMEM((POOL_HALO + tm, POOL_WIDTH), F32)],
        compiler_params=pltpu.CompilerParams(
            dimension_semantics=("arbitrary",), vmem_limit_bytes=VMEM_LIMIT),
        name="mix",
    )(hml, u, x2, w_pool, pool_scale, wo_ml, wo_pool, g_post, g_next, *w_next)


def _ffn_kernel(hn_ref, w1_ref, w2_ref, acc_ref):
    j = pl.program_id(1)
    groups = _row_groups(hn_ref.shape[0])
    acts = []
    for g in groups:
        a = jnp.dot(hn_ref[g, :], w1_ref[...], preferred_element_type=F32)
        acts.append(jnp.square(jnp.maximum(a, 0.0)).astype(BF16))
    for g, a in zip(groups, acts):
        prev = jnp.where(j == 0, 0.0, acc_ref[g, :])
        acc_ref[g, :] = prev + jnp.dot(a, w2_ref[...], preferred_element_type=F32)


def _ffn(hn, w1, w2):
    m, d = hn.shape
    f = w1.shape[1]
    return pl.pallas_call(
        _ffn_kernel,
        grid=(m // FFN_TM, f // FFN_TF),
        in_specs=[
            pl.BlockSpec((FFN_TM, d), lambda i, j: (i, 0)),
            pl.BlockSpec((d, FFN_TF), lambda i, j: (0, j)),
            pl.BlockSpec((FFN_TF, d), lambda i, j: (j, 0)),
        ],
        out_specs=pl.BlockSpec((FFN_TM, d), lambda i, j: (i, 0)),
        out_shape=jax.ShapeDtypeStruct((m, d), F32),
        compiler_params=pltpu.CompilerParams(
            dimension_semantics=("arbitrary", "arbitrary"), vmem_limit_bytes=VMEM_LIMIT),
        name="ffn",
    )(hn, w1, w2)


def _ple_kernel(h_ref, ff_ref, p_ref, gff_ref, ggate_ref, wgate_ref, wproj_ref, gpost_ref, out_ref,
                hn_ref):
    groups = _row_groups(h_ref.shape[0])
    for rs in groups:
        h = h_ref[rs, :] + _rms(ff_ref[rs, :], gff_ref[...])
        out_ref[rs, :] = h
        hn_ref[rs, :] = _rms(h, ggate_ref[...]).astype(BF16)
    for rs in groups:
        gate = _sigmoid(jnp.dot(hn_ref[rs, :], wgate_ref[...], preferred_element_type=F32))
        e = jnp.dot(p_ref[rs, :].astype(BF16), wproj_ref[...], preferred_element_type=F32)
        out_ref[rs, :] += _rms(e * gate, gpost_ref[...])


def _ple(h, ff, p2, g_ff_post, g_gate, w_gate, w_proj, g_post):
    m, d = h.shape
    pd = p2.shape[1]
    tm = PLE_TM
    const = lambda shape: pl.BlockSpec(shape, lambda i: (0,) * len(shape),
                                       pipeline_mode=pl.Buffered(1))
    return pl.pallas_call(
        _ple_kernel,
        grid=(m // tm,),
        in_specs=[
            pl.BlockSpec((tm, d), lambda i: (i, 0)),
            pl.BlockSpec((tm, d), lambda i: (i, 0)),
            pl.BlockSpec((tm, pd), lambda i: (i, 0)),
            const((1, d)),
            const((1, d)),
            const(w_gate.shape),
            const(w_proj.shape),
            const((1, d)),
        ],
        out_specs=pl.BlockSpec((tm, d), lambda i: (i, 0)),
        out_shape=jax.ShapeDtypeStruct((m, d), F32),
        scratch_shapes=[pltpu.VMEM((tm, d), BF16)],
        compiler_params=pltpu.CompilerParams(
            dimension_semantics=("arbitrary",), vmem_limit_bytes=VMEM_LIMIT),
        name="ple",
    )(h, ff, p2, g_ff_post, g_gate, w_gate, w_proj, g_post)


def _tile_qk_columns(w_q, w_k):
    r = w_q.shape[0]
    a = jnp.stack([w_q.reshape(r, COL_TILES, PAIR), w_k.reshape(r, COL_TILES, PAIR)], axis=2)
    return a.reshape(r, 2 * MLSTM_WIDTH)


def _layer(h, p2, seq, w_in, b_gates, w_qk_conv, g_mlstm, w_pool, pool_scale, w_out,
           g_mix_pre, g_mix_post, w_ff1, w_ff2, g_ff_pre, g_ff_post, w_ple_proj, w_ple_gate,
           g_ple_gate, g_ple_post):
    m, d = h.shape
    assert seq % ROW_TILE == 0 and seq % MIX_TM == 0 and m % FFN_TM == 0 and m % PLE_TM == 0
    assert w_in.shape == (d, 4 * MLSTM_WIDTH + 2 * HEADS + POOL_WIDTH) and w_ff1.shape[1] % FFN_TF == 0
    qkvo = 4 * MLSTM_WIDTH
    row = lambda v: v.reshape(1, -1).astype(F32)
    w_in16 = w_in.astype(BF16)
    w_u16 = w_in16[:, qkvo + 2 * HEADS:]
    w_gi = w_in[:, qkvo:qkvo + HEADS]
    w_gf = w_in[:, qkvo + HEADS:qkvo + 2 * HEADS]
    lane_pad = lambda w: jnp.pad(w, ((0, 0), (0, LANES - HEADS)))
    w_gcol = jnp.concatenate([lane_pad(w_gi), lane_pad(w_gf)], axis=1).astype(BF16)
    b_col = jnp.concatenate([lane_pad(b_gates[None, :HEADS]), lane_pad(b_gates[None, HEADS:])],
                            axis=1).astype(F32)
    b_row = b_gates.reshape(2 * HEADS, 1).astype(F32)
    wqk = _tile_qk_columns(w_qk_conv[:, :MLSTM_WIDTH], w_qk_conv[:, MLSTM_WIDTH:]).astype(F32)

    u, hml = _proj_mlstm(h, row(g_mix_pre), w_in16, w_u16, w_gcol, b_col, b_row, wqk,
                         row(g_mlstm), seq)
    w_out16 = w_out.astype(BF16)
    h, hn, w_ff1_16, w_ff2_16 = _mix(
        hml, u, h, w_pool.astype(BF16), row(pool_scale), w_out16[:MLSTM_WIDTH],
        w_out16[MLSTM_WIDTH:], row(g_mix_post), row(g_ff_pre), (w_ff1, w_ff2), seq)
    ff = _ffn(hn, w_ff1_16, w_ff2_16)
    return _ple(h, ff, p2, row(g_ff_post), row(g_ple_gate), w_ple_gate.astype(BF16),
                w_ple_proj.astype(BF16), row(g_ple_post))


def kernel(x, p, w_in, b_gates, w_qk_conv, g_mlstm, w_pool, pool_scale, w_out, g_mix_pre,
           g_mix_post, w_ff1, w_ff2, g_ff_pre, g_ff_post, w_ple_proj, w_ple_gate, g_ple_gate,
           g_ple_post):
    batch, seq, d = x.shape
    h = x.reshape(batch * seq, d)
    for i in range(p.shape[0]):
        h = _layer(h, p[i].reshape(batch * seq, -1), seq, w_in[i], b_gates[i],
                   w_qk_conv[i], g_mlstm[i], w_pool[i], pool_scale[i], w_out[i], g_mix_pre[i],
                   g_mix_post[i], w_ff1[i], w_ff2[i], g_ff_pre[i], g_ff_post[i], w_ple_proj[i],
                   w_ple_gate[i], g_ple_gate[i], g_ple_post[i])
    return h.reshape(batch, seq, d)
```

```python
import functools

import jax
import jax.numpy as jnp
from jax import lax
from jax.experimental import pallas as pl
from jax.experimental.pallas import tpu as pltpu

EPS = 1e-6
HEADS = 8
HEAD_DIM = 128
MLSTM_WIDTH = HEADS * HEAD_DIM
QK_CONV = 4
POOL_WINDOWS = (2, 4, 8, 16)
POOL_GROUP_WIDTH = 256
POOL_WIDTH = POOL_GROUP_WIDTH * len(POOL_WINDOWS)

LANES = 128
SUBLANES = 8
POOL_HALO = 16
CHUNK = 128
VMEM_LIMIT = 56 * 1024 * 1024

COL_TILES = 4
HEADS_PER_TILE = HEADS // COL_TILES
PAIR = HEADS_PER_TILE * HEAD_DIM
TILE_COLS = 4 * PAIR
ROW_TILE = COL_TILES * CHUNK

MIX_TM = 512
PLE_TM = 512
FFN_TM = 1024
FFN_TF = 1024
MXU_COLS = 256
ROW_SPLIT = 2

F32 = jnp.float32
BF16 = jnp.bfloat16


def _rms(x, g):
    return x * lax.rsqrt(jnp.mean(x * x, axis=-1, keepdims=True) + EPS) * g


def _log_sigmoid(x):
    return jnp.minimum(x, 0.0) - jnp.log1p(jnp.exp(-jnp.abs(x)))


def _sigmoid(x):
    return 1.0 / (1.0 + jnp.exp(-x))


def _row_groups(n):
    rows = n // ROW_SPLIT
    return [slice(r * rows, (r + 1) * rows) for r in range(ROW_SPLIT)]


def _mlstm_chunk(first, proj_ref, gc, gr, bc_ref, br_ref, wqk_ref, gn_ref, out_ref, ct_ref, m_ref,
                 xe_ref, between):
    L = CHUNK
    reset = lambda v: jnp.where(first, 0.0, v)
    between(0)

    xe_ref[0:SUBLANES, :] = reset(xe_ref[0:SUBLANES, :])
    qk = []
    for t in range(COL_TILES):
        cs = slice(t * 2 * PAIR, (t + 1) * 2 * PAIR)
        x = proj_ref[t, :, 0:2 * PAIR]
        xe_ref[SUBLANES:SUBLANES + L, cs] = x
        acc = x * wqk_ref[QK_CONV - 1:QK_CONV, cs]
        for back in range(1, QK_CONV):
            tap = QK_CONV - 1 - back
            acc = acc + xe_ref[SUBLANES - back:SUBLANES - back + L, cs] * wqk_ref[tap:tap + 1, cs]
        xe_ref[0:SUBLANES, cs] = x[L - SUBLANES:L, :]
        qk.append(acc * _sigmoid(acc))

    gc = gc + bc_ref[...]
    li_c = gc[:, :LANES]
    lf_c = _log_sigmoid(gc[:, LANES:])
    gr = gr + br_ref[...]
    li_r = gr[0:SUBLANES, :]
    lf_r = _log_sigmoid(gr[SUBLANES:, :])
    row = lax.broadcasted_iota(jnp.int32, (L, L), 0)
    col = lax.broadcasted_iota(jnp.int32, (L, L), 1)
    causal = row >= col
    tril = jnp.where(causal, 1.0, 0.0).astype(F32)
    triu = jnp.where(row <= col, 1.0, 0.0).astype(F32)
    b_c = jnp.dot(tril, lf_c, preferred_element_type=F32, precision=lax.Precision.HIGHEST)
    b_r = jnp.dot(lf_r, triu, preferred_element_type=F32, precision=lax.Precision.HIGHEST)
    r_r = li_r - b_r
    btot = b_c[L - 1:L, :]
    a_c = btot - b_c + li_c
    m_prev = reset(m_ref[0:1, :])
    m_new = jnp.maximum(btot + m_prev, jnp.max(a_c, axis=0, keepdims=True))
    decay = jnp.exp(btot + m_prev - m_new)
    w_c = jnp.exp(a_c - m_new)
    inter_log = b_c + m_prev
    m_ref[0:1, :] = m_new

    ones_col = jnp.where(lax.broadcasted_iota(jnp.int32, (L, LANES), 1) == 0, 1.0, 0.0).astype(F32)

    for h in range(HEADS):
        t, e = divmod(h, HEADS_PER_TILE)
        if e == 0:
            between(t + 1)
        es = slice(e * HEAD_DIM, (e + 1) * HEAD_DIM)
        sl = slice(h * HEAD_DIM, (h + 1) * HEAD_DIM)
        qf = qk[t][:, es]
        qh = qf.astype(BF16)
        kh = (qk[t][:, PAIR + e * HEAD_DIM:PAIR + (e + 1) * HEAD_DIM] * (HEAD_DIM ** -0.5)).astype(BF16)
        vh = proj_ref[t, :, 2 * PAIR + e * HEAD_DIM:2 * PAIR + (e + 1) * HEAD_DIM]
        oh = proj_ref[t, :, 3 * PAIR + e * HEAD_DIM:3 * PAIR + (e + 1) * HEAD_DIM]
        vaug = jnp.concatenate([vh, ones_col], axis=1)
        dlog = jnp.where(causal, b_c[:, h:h + 1] + r_r[h:h + 1, :], -jnp.inf)
        inter = inter_log[:, h:h + 1]
        m_t = jnp.maximum(inter, jnp.max(dlog, axis=1, keepdims=True))
        dw = jnp.exp(dlog - m_t)
        inter_w = jnp.exp(inter - m_t)
        s = lax.dot_general(qh, kh, (((1,), (1,)), ((), ())), preferred_element_type=F32) * dw
        ct = reset(ct_ref[h])
        lhs = jnp.concatenate([s.astype(BF16), (inter_w * qf).astype(BF16)], axis=1)
        rhs = jnp.concatenate([vaug.astype(BF16), ct.astype(BF16)], axis=0)
        tot = jnp.dot(lhs, rhs, preferred_element_type=F32)
        num = tot[:, :HEAD_DIM]
        den = tot[:, HEAD_DIM:HEAD_DIM + 1]
        cell = num / jnp.maximum(jnp.abs(den), jnp.exp(-m_t))
        cell = _rms(cell, gn_ref[:, sl])
        out_ref[:, sl] = (_sigmoid(oh) * cell).astype(out_ref.dtype)
        wv = (w_c[:, h:h + 1] * vaug).astype(BF16)
        upd = lax.dot_general(kh, wv, (((0,), (0,)), ((), ())), preferred_element_type=F32)
        ct_ref[h] = decay[:, h:h + 1] * ct + upd


def _proj_mlstm_kernel(x0_ref, xnext_ref, g_ref, wq_ref, wk_ref, wv_ref, wo_ref, wu_ref, wgc_ref,
                       bc_ref, br_ref, wqk_ref, gn_ref,
                       u_ref, hml_ref, xn_a, xn_b, proj_a, proj_b, gcol_a, gcol_b, grow_a, grow_b,
                       ct_ref, m_ref, xe_ref, *, tiles_per_seq):
    w_refs = (wq_ref, wk_ref, wv_ref, wo_ref, wu_ref)
    i = pl.program_id(0)
    j = pl.program_id(1)

    @pl.when((i == 0) & (j == 0))
    def _():
        xn_a[...] = _rms(x0_ref[...], g_ref[...]).astype(BF16)
        proj_b[...] = jnp.zeros_like(proj_b)
        gcol_b[...] = jnp.zeros_like(gcol_b)
        grow_b[...] = jnp.zeros_like(grow_b)
        ct_ref[...] = jnp.zeros_like(ct_ref)
        m_ref[...] = jnp.zeros_like(m_ref)
        xe_ref[...] = jnp.zeros_like(xe_ref)

    def step(xn_cur, xn_nxt, proj_w, gcol_w, grow_w, proj_r, gcol_r, grow_r):
        r0 = pl.multiple_of(j * CHUNK, CHUNK)

        def project_piece(k):
            y = jnp.dot(xn_cur[...], w_refs[k][...], preferred_element_type=F32)
            if k < 4:
                proj_w[j, :, k * PAIR:(k + 1) * PAIR] = y
            else:
                u_ref[...] = y

        first = ((i + tiles_per_seq - 1) % tiles_per_seq == 0) & (j == 0)
        _mlstm_chunk(first, proj_r.at[:, pl.ds(r0, CHUNK), :], gcol_r[pl.ds(r0, CHUNK), :],
                     grow_r[j], bc_ref, br_ref, wqk_ref, gn_ref, hml_ref, ct_ref, m_ref, xe_ref,
                     project_piece)

        gc = jnp.dot(xn_cur[pl.ds(r0, CHUNK), :], wgc_ref[...], preferred_element_type=F32)
        gcol_w[pl.ds(r0, CHUNK), :] = gc
        grow_w[j] = jnp.concatenate(
            [gc[:, :LANES].T[0:SUBLANES, :], gc[:, LANES:].T[0:SUBLANES, :]], axis=0)

        xn_nxt[pl.ds(r0, CHUNK), :] = _rms(xnext_ref[...], g_ref[...]).astype(BF16)

    @pl.when(i % 2 == 0)
    def _():
        step(xn_a, xn_b, proj_a, gcol_a, grow_a, proj_b, gcol_b, grow_b)

    @pl.when(i % 2 == 1)
    def _():
        step(xn_b, xn_a, proj_b, gcol_b, grow_b, proj_a, gcol_a, grow_a)


def _proj_mlstm(x2, g, w_in16, w_u16, w_gcol, b_col, b_row, wqk, gn, seq):
    m, d = x2.shape
    nt = m // ROW_TILE
    last_chunk = m // CHUNK - 1
    const = lambda shape: pl.BlockSpec(shape, lambda i, j: (0,) * len(shape),
                                       pipeline_mode=pl.Buffered(1))
    slab = lambda s: pl.BlockSpec((d, PAIR), lambda i, j: (0, s * COL_TILES + j))
    return pl.pallas_call(
        functools.partial(_proj_mlstm_kernel, tiles_per_seq=seq // ROW_TILE),
        grid=(nt + 1, COL_TILES),
        in_specs=[
            const((ROW_TILE, d)),
            pl.BlockSpec((CHUNK, d),
                         lambda i, j: (jnp.minimum((i + 1) * COL_TILES + j, last_chunk), 0)),
            const((1, d)),
            slab(0), slab(1), slab(2), slab(3),
            pl.BlockSpec((d, POOL_GROUP_WIDTH), lambda i, j: (0, j)),
            const((d, 2 * LANES)),
            const((1, 2 * LANES)),
            const((2 * SUBLANES, 1)),
            const((QK_CONV, 2 * MLSTM_WIDTH)),
            const((1, MLSTM_WIDTH)),
        ],
        out_specs=[
            pl.BlockSpec((ROW_TILE, POOL_GROUP_WIDTH), lambda i, j: (i, j)),
            pl.BlockSpec((CHUNK, MLSTM_WIDTH),
                         lambda i, j: (jnp.maximum((i - 1) * COL_TILES + j, 0), 0)),
        ],
        out_shape=[
            jax.ShapeDtypeStruct((m + ROW_TILE, POOL_WIDTH), F32),
            jax.ShapeDtypeStruct((m, MLSTM_WIDTH), BF16),
        ],
        scratch_shapes=[
            pltpu.VMEM((ROW_TILE, d), BF16),
            pltpu.VMEM((ROW_TILE, d), BF16),
            pltpu.VMEM((COL_TILES, ROW_TILE, TILE_COLS), F32),
            pltpu.VMEM((COL_TILES, ROW_TILE, TILE_COLS), F32),
            pltpu.VMEM((ROW_TILE, 2 * LANES), F32),
            pltpu.VMEM((ROW_TILE, 2 * LANES), F32),
            pltpu.VMEM((COL_TILES, 2 * SUBLANES, CHUNK), F32),
            pltpu.VMEM((COL_TILES, 2 * SUBLANES, CHUNK), F32),
            pltpu.VMEM((HEADS, HEAD_DIM, 2 * HEAD_DIM), F32),
            pltpu.VMEM((SUBLANES, LANES), F32),
            pltpu.VMEM((SUBLANES + CHUNK, 2 * MLSTM_WIDTH), F32),
        ],
        compiler_params=pltpu.CompilerParams(
            dimension_semantics=("arbitrary", "arbitrary"), vmem_limit_bytes=VMEM_LIMIT),
        name="proj_mlstm",
    )(x2, x2, g, w_in16, w_in16, w_in16, w_in16, w_u16, w_gcol, b_col, b_row, wqk, gn)


def _mix_kernel(hml_ref, u_ref, x_ref, wpool_ref, pscale_ref, wo_ml_ref, wo_pool_ref, g_ref,
                gnext_ref, wnext_ref, out_ref, hn_ref, wnext16_ref, ue_ref, part_ref, *,
                tiles_per_seq):
    tm = u_ref.shape[0]
    t = pl.program_id(0) % tiles_per_seq
    wnext16_ref[...] = wnext_ref[...].astype(BF16)

    @pl.when(t == 0)
    def _():
        ue_ref[0:POOL_HALO, :] = jnp.zeros((POOL_HALO, POOL_WIDTH), F32)

    groups = _row_groups(tm)
    d = out_ref.shape[1]
    col_pieces = [slice(n * MXU_COLS, (n + 1) * MXU_COLS) for n in range(d // MXU_COLS)]

    def ml_piece(rs, cs):
        part_ref[rs, cs] = jnp.dot(hml_ref[rs, :], wo_ml_ref[:, cs], preferred_element_type=F32)

    def pool_piece(rs, cs, hpool):
        part_ref[rs, cs] += jnp.dot(hpool[rs, :], wo_pool_ref[:, cs], preferred_element_type=F32)

    def finish(rs):
        h = x_ref[rs, :] + _rms(part_ref[rs, :], g_ref[...])
        out_ref[rs, :] = h
        hn_ref[rs, :] = _rms(h, gnext_ref[...]).astype(BF16)

    ml_pieces = iter([(rs, cs) for rs in groups for cs in col_pieces])
    per_pool = len(groups) * len(col_pieces) // len(POOL_WINDOWS)

    u = u_ref[...]
    ue_ref[POOL_HALO:POOL_HALO + tm, :] = u
    pos = (lax.broadcasted_iota(jnp.int32, (tm, 1), 0) + (t * tm + 1)).astype(F32)
    mixed = []
    for g, window in enumerate(POOL_WINDOWS):
        for _ in range(per_pool):
            ml_piece(*next(ml_pieces))
        sl = slice(g * POOL_GROUP_WIDTH, (g + 1) * POOL_GROUP_WIDTH)
        win = ue_ref[:, sl]
        span = 1
        while span < window:
            win = win + pltpu.roll(win, span, axis=0)
            span *= 2
        mean = win[POOL_HALO:, :] / jnp.minimum(pos, float(window))
        pooled = (mean - u[:, sl]).astype(BF16)
        mixed.append((jnp.dot(pooled, wpool_ref[g], preferred_element_type=F32)
                      * pscale_ref[:, sl]).astype(BF16))
    ue_ref[0:POOL_HALO, :] = u[tm - POOL_HALO:tm, :]
    hpool = jnp.concatenate(mixed, axis=1)

    for cs in col_pieces:
        pool_piece(groups[0], cs, hpool)
    for prev, rs in zip(groups[:-1], groups[1:]):
        half = len(col_pieces) // 2
        for cs in col_pieces[:half]:
            pool_piece(rs, cs, hpool)
        finish(prev)
        for cs in col_pieces[half:]:
            pool_piece(rs, cs, hpool)
    finish(groups[-1])


def _mix(hml, u, x2, w_pool, pool_scale, wo_ml, wo_pool, g_post, g_next, w_next, seq):
    m, d = x2.shape
    tm = MIX_TM
    wr = w_next.shape[0] // (m // tm)
    const = lambda shape: pl.BlockSpec(shape, lambda i: (0,) * len(shape),
                                       pipeline_mode=pl.Buffered(1))
    return pl.pallas_call(
        functools.partial(_mix_kernel, tiles_per_seq=seq // tm),
        grid=(m // tm,),
        in_specs=[
            pl.BlockSpec((tm, MLSTM_WIDTH), lambda i: (i, 0)),
            pl.BlockSpec((tm, POOL_WIDTH), lambda i: (i, 0)),
            pl.BlockSpec((tm, d), lambda i: (i, 0)),
            const(w_pool.shape),
            const((1, POOL_WIDTH)),
            const(wo_ml.shape),
            const(wo_pool.shape),
            const((1, d)),
            const((1, d)),
            pl.BlockSpec((wr, w_next.shape[1]), lambda i: (i, 0)),
        ],
        out_specs=[pl.BlockSpec((tm, d), lambda i: (i, 0)), pl.BlockSpec((tm, d), lambda i: (i, 0)),
                   pl.BlockSpec((wr, w_next.shape[1]), lambda i: (i, 0))],
        out_shape=[jax.ShapeDtypeStruct((m, d), F32), jax.ShapeDtypeStruct((m, d), BF16),
                   jax.ShapeDtypeStruct(w_next.shape, BF16)],
        scratch_shapes=[pltpu.VMEM((POOL_HALO + tm, POOL_WIDTH), F32), pltpu.VMEM((tm, d), F32)],
        compiler_params=pltpu.CompilerParams(
            dimension_semantics=("arbitrary",), vmem_limit_bytes=VMEM_LIMIT),
        name="mix",
    )(hml, u, x2, w_pool, pool_scale, wo_ml, wo_pool, g_post, g_next, w_next)


def _ffn_kernel(hn_ref, w1_ref, w2_ref, acc_ref):
    j = pl.program_id(1)
    groups = _row_groups(hn_ref.shape[0])
    w1 = w1_ref[...].astype(BF16)
    acts = []
    for g in groups:
        a = jnp.dot(hn_ref[g, :], w1, preferred_element_type=F32)
        acts.append(jnp.square(jnp.maximum(a, 0.0)).astype(BF16))
    for g, a in zip(groups, acts):
        prev = jnp.where(j == 0, 0.0, acc_ref[g, :])
        acc_ref[g, :] = prev + jnp.dot(a, w2_ref[...], preferred_element_type=F32)


def _ffn(hn, w1, w2):
    m, d = hn.shape
    f = w1.shape[1]
    return pl.pallas_call(
        _ffn_kernel,
        grid=(m // FFN_TM, f // FFN_TF),
        in_specs=[
            pl.BlockSpec((FFN_TM, d), lambda i, j: (i, 0)),
            pl.BlockSpec((d, FFN_TF), lambda i, j: (0, j)),
            pl.BlockSpec((FFN_TF, d), lambda i, j: (j, 0)),
        ],
        out_specs=pl.BlockSpec((FFN_TM, d), lambda i, j: (i, 0)),
        out_shape=jax.ShapeDtypeStruct((m, d), F32),
        compiler_params=pltpu.CompilerParams(
            dimension_semantics=("arbitrary", "arbitrary"), vmem_limit_bytes=VMEM_LIMIT),
        name="ffn",
    )(hn, w1, w2)


def _ple_kernel(h_ref, ff_ref, p_ref, gff_ref, ggate_ref, wgate_ref, wproj_ref, gpost_ref, out_ref,
                hn_ref, z_ref):
    groups = _row_groups(h_ref.shape[0])
    d = out_ref.shape[1]
    col_pieces = [slice(n * MXU_COLS, (n + 1) * MXU_COLS) for n in range(d // MXU_COLS)]
    half = len(col_pieces) // 2

    def prologue(rs):
        h = h_ref[rs, :] + _rms(ff_ref[rs, :], gff_ref[...])
        out_ref[rs, :] = h
        hn_ref[rs, :] = _rms(h, ggate_ref[...]).astype(BF16)

    def gate_piece(rs, cs):
        z_ref[rs, cs] = jnp.dot(hn_ref[rs, :], wgate_ref[:, cs], preferred_element_type=F32)

    def finish(rs):
        e = jnp.dot(p_ref[rs, :].astype(BF16), wproj_ref[...], preferred_element_type=F32)
        out_ref[rs, :] += _rms(e * _sigmoid(z_ref[rs, :]), gpost_ref[...])

    prologue(groups[0])
    for prev, rs in zip(groups[:-1], groups[1:]):
        for cs in col_pieces[:half]:
            gate_piece(prev, cs)
        prologue(rs)
        for cs in col_pieces[half:]:
            gate_piece(prev, cs)
        for cs in col_pieces[:half]:
            gate_piece(rs, cs)
        finish(prev)
        for cs in col_pieces[half:]:
            gate_piece(rs, cs)
    finish(groups[-1])


def _ple(h, ff, p2, g_ff_post, g_gate, w_gate, w_proj, g_post):
    m, d = h.shape
    pd = p2.shape[1]
    tm = PLE_TM
    const = lambda shape: pl.BlockSpec(shape, lambda i: (0,) * len(shape),
                                       pipeline_mode=pl.Buffered(1))
    return pl.pallas_call(
        _ple_kernel,
        grid=(m // tm,),
        in_specs=[
            pl.BlockSpec((tm, d), lambda i: (i, 0)),
            pl.BlockSpec((tm, d), lambda i: (i, 0)),
            pl.BlockSpec((tm, pd), lambda i: (i, 0)),
            const((1, d)),
            const((1, d)),
            const(w_gate.shape),
            const(w_proj.shape),
            const((1, d)),
        ],
        out_specs=pl.BlockSpec((tm, d), lambda i: (i, 0)),
        out_shape=jax.ShapeDtypeStruct((m, d), F32),
        scratch_shapes=[pltpu.VMEM((tm, d), BF16), pltpu.VMEM((tm, d), F32)],
        compiler_params=pltpu.CompilerParams(
            dimension_semantics=("arbitrary",), vmem_limit_bytes=VMEM_LIMIT),
        name="ple",
    )(h, ff, p2, g_ff_post, g_gate, w_gate, w_proj, g_post)


def _tile_qk_columns(w_q, w_k):
    r = w_q.shape[0]
    a = jnp.stack([w_q.reshape(r, COL_TILES, PAIR), w_k.reshape(r, COL_TILES, PAIR)], axis=2)
    return a.reshape(r, 2 * MLSTM_WIDTH)


def _layer(h, p2, seq, w_in, b_gates, w_qk_conv, g_mlstm, w_pool, pool_scale, w_out,
           g_mix_pre, g_mix_post, w_ff1, w_ff2, g_ff_pre, g_ff_post, w_ple_proj, w_ple_gate,
           g_ple_gate, g_ple_post):
    m, d = h.shape
    assert seq % ROW_TILE == 0 and seq % MIX_TM == 0 and m % FFN_TM == 0 and m % PLE_TM == 0
    assert w_in.shape == (d, 4 * MLSTM_WIDTH + 2 * HEADS + POOL_WIDTH) and w_ff1.shape[1] % FFN_TF == 0
    qkvo = 4 * MLSTM_WIDTH
    row = lambda v: v.reshape(1, -1).astype(F32)
    w_in16 = w_in.astype(BF16)
    w_u16 = w_in16[:, qkvo + 2 * HEADS:]
    w_gi = w_in[:, qkvo:qkvo + HEADS]
    w_gf = w_in[:, qkvo + HEADS:qkvo + 2 * HEADS]
    lane_pad = lambda w: jnp.pad(w, ((0, 0), (0, LANES - HEADS)))
    w_gcol = jnp.concatenate([lane_pad(w_gi), lane_pad(w_gf)], axis=1).astype(BF16)
    b_col = jnp.concatenate([lane_pad(b_gates[None, :HEADS]), lane_pad(b_gates[None, HEADS:])],
                            axis=1).astype(F32)
    b_row = b_gates.reshape(2 * HEADS, 1).astype(F32)
    wqk = _tile_qk_columns(w_qk_conv[:, :MLSTM_WIDTH], w_qk_conv[:, MLSTM_WIDTH:]).astype(F32)

    u, hml = _proj_mlstm(h, row(g_mix_pre), w_in16, w_u16, w_gcol, b_col, b_row, wqk,
                         row(g_mlstm), seq)
    w_out16 = w_out.astype(BF16)
    h, hn, w_ff2_16 = _mix(hml, u, h, w_pool.astype(BF16), row(pool_scale), w_out16[:MLSTM_WIDTH],
                           w_out16[MLSTM_WIDTH:], row(g_mix_post), row(g_ff_pre), w_ff2, seq)
    ff = _ffn(hn, w_ff1, w_ff2_16)
    return _ple(h, ff, p2, row(g_ff_post), row(g_ple_gate), w_ple_gate.astype(BF16),
                w_ple_proj.astype(BF16), row(g_ple_post))


def kernel(x, p, w_in, b_gates, w_qk_conv, g_mlstm, w_pool, pool_scale, w_out, g_mix_pre,
           g_mix_post, w_ff1, w_ff2, g_ff_pre, g_ff_post, w_ple_proj, w_ple_gate, g_ple_gate,
           g_ple_post):
    batch, seq, d = x.shape
    h = x.reshape(batch * seq, d)
    for i in range(p.shape[0]):
        h = _layer(h, p[i].reshape(batch * seq, -1), seq, w_in[i], b_gates[i],
                   w_qk_conv[i], g_mlstm[i], w_pool[i], pool_scale[i], w_out[i], g_mix_pre[i],
                   g_mix_post[i], w_ff1[i], w_ff2[i], g_ff_pre[i], g_ff_post[i], w_ple_proj[i],
                   w_ple_gate[i], g_ple_gate[i], g_ple_post[i])
    return h.reshape(batch, seq, d)
```

```python
import functools

import jax
import jax.numpy as jnp
from jax import lax
from jax.experimental import pallas as pl
from jax.experimental.pallas import tpu as pltpu

EPS = 1e-6
HEADS = 8
HEAD_DIM = 128
MLSTM_WIDTH = HEADS * HEAD_DIM
QK_CONV = 4
POOL_WINDOWS = (2, 4, 8, 16)
POOL_GROUP_WIDTH = 256
POOL_WIDTH = POOL_GROUP_WIDTH * len(POOL_WINDOWS)

LANES = 128
SUBLANES = 8
POOL_HALO = 16
CHUNK = 128
VMEM_LIMIT = 56 * 1024 * 1024

COL_TILES = 4
HEADS_PER_TILE = HEADS // COL_TILES
PAIR = HEADS_PER_TILE * HEAD_DIM
TILE_COLS = 4 * PAIR
ROW_TILE = COL_TILES * CHUNK

MIX_TM = 512
PLE_TM = 512
FFN_TM = 1024
FFN_TF = 1024
MXU_COLS = 256
ROW_SPLIT = 2

F32 = jnp.float32
BF16 = jnp.bfloat16


def _rms(x, g):
    return x * lax.rsqrt(jnp.mean(x * x, axis=-1, keepdims=True) + EPS) * g


def _log_sigmoid(x):
    return jnp.minimum(x, 0.0) - jnp.log1p(jnp.exp(-jnp.abs(x)))


def _sigmoid(x):
    return 1.0 / (1.0 + jnp.exp(-x))


def _row_groups(n):
    rows = n // ROW_SPLIT
    return [slice(r * rows, (r + 1) * rows) for r in range(ROW_SPLIT)]


def _mlstm_chunk(first, proj_ref, gc, gr, bc_ref, br_ref, wqk_ref, gn_ref, out_ref, ct_ref, m_ref,
                 xe_ref, between):
    L = CHUNK
    reset = lambda v: jnp.where(first, 0.0, v)
    between(0)

    xe_ref[0:SUBLANES, :] = reset(xe_ref[0:SUBLANES, :])
    qk = []
    for t in range(COL_TILES):
        cs = slice(t * 2 * PAIR, (t + 1) * 2 * PAIR)
        x = proj_ref[t, :, 0:2 * PAIR]
        xe_ref[SUBLANES:SUBLANES + L, cs] = x
        acc = x * wqk_ref[QK_CONV - 1:QK_CONV, cs]
        for back in range(1, QK_CONV):
            tap = QK_CONV - 1 - back
            acc = acc + xe_ref[SUBLANES - back:SUBLANES - back + L, cs] * wqk_ref[tap:tap + 1, cs]
        xe_ref[0:SUBLANES, cs] = x[L - SUBLANES:L, :]
        qk.append(acc * _sigmoid(acc))

    gc = gc + bc_ref[...]
    li_c = gc[:, :LANES]
    lf_c = _log_sigmoid(gc[:, LANES:])
    gr = gr + br_ref[...]
    li_r = gr[0:SUBLANES, :]
    lf_r = _log_sigmoid(gr[SUBLANES:, :])
    row = lax.broadcasted_iota(jnp.int32, (L, L), 0)
    col = lax.broadcasted_iota(jnp.int32, (L, L), 1)
    causal = row >= col
    tril = jnp.where(causal, 1.0, 0.0).astype(F32)
    triu = jnp.where(row <= col, 1.0, 0.0).astype(F32)
    b_c = jnp.dot(tril, lf_c, preferred_element_type=F32, precision=lax.Precision.HIGHEST)
    b_r = jnp.dot(lf_r, triu, preferred_element_type=F32, precision=lax.Precision.HIGHEST)
    r_r = li_r - b_r
    btot = b_c[L - 1:L, :]
    a_c = btot - b_c + li_c
    m_prev = reset(m_ref[0:1, :])
    m_new = jnp.maximum(btot + m_prev, jnp.max(a_c, axis=0, keepdims=True))
    decay = jnp.exp(btot + m_prev - m_new)
    w_c = jnp.exp(a_c - m_new)
    inter_log = b_c + m_prev
    m_ref[0:1, :] = m_new

    ones_col = jnp.where(lax.broadcasted_iota(jnp.int32, (L, LANES), 1) == 0, 1.0, 0.0).astype(F32)

    for h in range(HEADS):
        t, e = divmod(h, HEADS_PER_TILE)
        if e == 0:
            between(t + 1)
        es = slice(e * HEAD_DIM, (e + 1) * HEAD_DIM)
        sl = slice(h * HEAD_DIM, (h + 1) * HEAD_DIM)
        qf = qk[t][:, es]
        qh = qf.astype(BF16)
        kh = (qk[t][:, PAIR + e * HEAD_DIM:PAIR + (e + 1) * HEAD_DIM] * (HEAD_DIM ** -0.5)).astype(BF16)
        vh = proj_ref[t, :, 2 * PAIR + e * HEAD_DIM:2 * PAIR + (e + 1) * HEAD_DIM]
        oh = proj_ref[t, :, 3 * PAIR + e * HEAD_DIM:3 * PAIR + (e + 1) * HEAD_DIM]
        vaug = jnp.concatenate([vh, ones_col], axis=1)
        dlog = jnp.where(causal, b_c[:, h:h + 1] + r_r[h:h + 1, :], -jnp.inf)
        inter = inter_log[:, h:h + 1]
        m_t = jnp.maximum(inter, jnp.max(dlog, axis=1, keepdims=True))
        dw = jnp.exp(dlog - m_t)
        inter_w = jnp.exp(inter - m_t)
        s = lax.dot_general(qh, kh, (((1,), (1,)), ((), ())), preferred_element_type=F32) * dw
        ct = reset(ct_ref[h])
        lhs = jnp.concatenate([s.astype(BF16), (inter_w * qf).astype(BF16)], axis=1)
        rhs = jnp.concatenate([vaug.astype(BF16), ct.astype(BF16)], axis=0)
        tot = jnp.dot(lhs, rhs, preferred_element_type=F32)
        num = tot[:, :HEAD_DIM]
        den = tot[:, HEAD_DIM:HEAD_DIM + 1]
        cell = num / jnp.maximum(jnp.abs(den), jnp.exp(-m_t))
        cell = _rms(cell, gn_ref[:, sl])
        out_ref[:, sl] = (_sigmoid(oh) * cell).astype(out_ref.dtype)
        wv = (w_c[:, h:h + 1] * vaug).astype(BF16)
        upd = lax.dot_general(kh, wv, (((0,), (0,)), ((), ())), preferred_element_type=F32)
        ct_ref[h] = decay[:, h:h + 1] * ct + upd


def _proj_mlstm_kernel(x0_ref, xnext_ref, g_ref, wq_ref, wk_ref, wv_ref, wo_ref, wu_ref, wgc_ref,
                       bc_ref, br_ref, wqk_ref, gn_ref,
                       u_ref, hml_ref, xn_a, xn_b, proj_a, proj_b, gcol_a, gcol_b, grow_a, grow_b,
                       ct_ref, m_ref, xe_ref, *, tiles_per_seq):
    w_refs = (wq_ref, wk_ref, wv_ref, wo_ref, wu_ref)
    i = pl.program_id(0)
    j = pl.program_id(1)

    @pl.when((i == 0) & (j == 0))
    def _():
        xn_a[...] = _rms(x0_ref[...], g_ref[...]).astype(BF16)
        proj_b[...] = jnp.zeros_like(proj_b)
        gcol_b[...] = jnp.zeros_like(gcol_b)
        grow_b[...] = jnp.zeros_like(grow_b)
        ct_ref[...] = jnp.zeros_like(ct_ref)
        m_ref[...] = jnp.zeros_like(m_ref)
        xe_ref[...] = jnp.zeros_like(xe_ref)

    def step(xn_cur, xn_nxt, proj_w, gcol_w, grow_w, proj_r, gcol_r, grow_r):
        r0 = pl.multiple_of(j * CHUNK, CHUNK)

        def project_piece(k):
            y = jnp.dot(xn_cur[...], w_refs[k][...], preferred_element_type=F32)
            if k < 4:
                proj_w[j, :, k * PAIR:(k + 1) * PAIR] = y
            else:
                u_ref[...] = y

        first = ((i + tiles_per_seq - 1) % tiles_per_seq == 0) & (j == 0)
        _mlstm_chunk(first, proj_r.at[:, pl.ds(r0, CHUNK), :], gcol_r[pl.ds(r0, CHUNK), :],
                     grow_r[j], bc_ref, br_ref, wqk_ref, gn_ref, hml_ref, ct_ref, m_ref, xe_ref,
                     project_piece)

        gc = jnp.dot(xn_cur[pl.ds(r0, CHUNK), :], wgc_ref[...], preferred_element_type=F32)
        gcol_w[pl.ds(r0, CHUNK), :] = gc
        grow_w[j] = jnp.concatenate(
            [gc[:, :LANES].T[0:SUBLANES, :], gc[:, LANES:].T[0:SUBLANES, :]], axis=0)

        xn_nxt[pl.ds(r0, CHUNK), :] = _rms(xnext_ref[...], g_ref[...]).astype(BF16)

    @pl.when(i % 2 == 0)
    def _():
        step(xn_a, xn_b, proj_a, gcol_a, grow_a, proj_b, gcol_b, grow_b)

    @pl.when(i % 2 == 1)
    def _():
        step(xn_b, xn_a, proj_b, gcol_b, grow_b, proj_a, gcol_a, grow_a)


def _proj_mlstm(x2, g, w_in16, w_u16, w_gcol, b_col, b_row, wqk, gn, seq):
    m, d = x2.shape
    nt = m // ROW_TILE
    last_chunk = m // CHUNK - 1
    const = lambda shape: pl.BlockSpec(shape, lambda i, j: (0,) * len(shape),
                                       pipeline_mode=pl.Buffered(1))
    slab = lambda s: pl.BlockSpec((d, PAIR), lambda i, j: (0, s * COL_TILES + j))
    return pl.pallas_call(
        functools.partial(_proj_mlstm_kernel, tiles_per_seq=seq // ROW_TILE),
        grid=(nt + 1, COL_TILES),
        in_specs=[
            const((ROW_TILE, d)),
            pl.BlockSpec((CHUNK, d),
                         lambda i, j: (jnp.minimum((i + 1) * COL_TILES + j, last_chunk), 0)),
            const((1, d)),
            slab(0), slab(1), slab(2), slab(3),
            pl.BlockSpec((d, POOL_GROUP_WIDTH), lambda i, j: (0, j)),
            const((d, 2 * LANES)),
            const((1, 2 * LANES)),
            const((2 * SUBLANES, 1)),
            const((QK_CONV, 2 * MLSTM_WIDTH)),
            const((1, MLSTM_WIDTH)),
        ],
        out_specs=[
            pl.BlockSpec((ROW_TILE, POOL_GROUP_WIDTH), lambda i, j: (i, j)),
            pl.BlockSpec((CHUNK, MLSTM_WIDTH),
                         lambda i, j: (jnp.maximum((i - 1) * COL_TILES + j, 0), 0)),
        ],
        out_shape=[
            jax.ShapeDtypeStruct((m + ROW_TILE, POOL_WIDTH), F32),
            jax.ShapeDtypeStruct((m, MLSTM_WIDTH), BF16),
        ],
        scratch_shapes=[
            pltpu.VMEM((ROW_TILE, d), BF16),
            pltpu.VMEM((ROW_TILE, d), BF16),
            pltpu.VMEM((COL_TILES, ROW_TILE, TILE_COLS), F32),
            pltpu.VMEM((COL_TILES, ROW_TILE, TILE_COLS), F32),
            pltpu.VMEM((ROW_TILE, 2 * LANES), F32),
            pltpu.VMEM((ROW_TILE, 2 * LANES), F32),
            pltpu.VMEM((COL_TILES, 2 * SUBLANES, CHUNK), F32),
            pltpu.VMEM((COL_TILES, 2 * SUBLANES, CHUNK), F32),
            pltpu.VMEM((HEADS, HEAD_DIM, 2 * HEAD_DIM), F32),
            pltpu.VMEM((SUBLANES, LANES), F32),
            pltpu.VMEM((SUBLANES + CHUNK, 2 * MLSTM_WIDTH), F32),
        ],
        compiler_params=pltpu.CompilerParams(
            dimension_semantics=("arbitrary", "arbitrary"), vmem_limit_bytes=VMEM_LIMIT),
        name="proj_mlstm",
    )(x2, x2, g, w_in16, w_in16, w_in16, w_in16, w_u16, w_gcol, b_col, b_row, wqk, gn)


def _mix_kernel(hml_ref, u_ref, x_ref, wpool_ref, pscale_ref, wo_ml_ref, wo_pool_ref, g_ref,
                gnext_ref, wnext_ref, out_ref, hn_ref, wnext16_ref, ue_ref, part_ref, *,
                tiles_per_seq):
    tm = u_ref.shape[0]
    t = pl.program_id(0) % tiles_per_seq
    wnext16_ref[...] = wnext_ref[...].astype(BF16)

    @pl.when(t == 0)
    def _():
        ue_ref[0:POOL_HALO, :] = jnp.zeros((POOL_HALO, POOL_WIDTH), F32)

    groups = _row_groups(tm)
    d = out_ref.shape[1]
    col_pieces = [slice(n * MXU_COLS, (n + 1) * MXU_COLS) for n in range(d // MXU_COLS)]

    def ml_piece(rs, cs):
        part_ref[rs, cs] = jnp.dot(hml_ref[rs, :], wo_ml_ref[:, cs], preferred_element_type=F32)

    def pool_piece(rs, cs, hpool):
        part_ref[rs, cs] += jnp.dot(hpool[rs, :], wo_pool_ref[:, cs], preferred_element_type=F32)

    def finish(rs):
        h = x_ref[rs, :] + _rms(part_ref[rs, :], g_ref[...])
        out_ref[rs, :] = h
        hn_ref[rs, :] = _rms(h, gnext_ref[...]).astype(BF16)

    ml_pieces = iter([(rs, cs) for rs in groups for cs in col_pieces])
    per_pool = len(groups) * len(col_pieces) // len(POOL_WINDOWS)

    u = u_ref[...]
    ue_ref[POOL_HALO:POOL_HALO + tm, :] = u
    pos = (lax.broadcasted_iota(jnp.int32, (tm, 1), 0) + (t * tm + 1)).astype(F32)
    mixed = []
    for g, window in enumerate(POOL_WINDOWS):
        for _ in range(per_pool):
            ml_piece(*next(ml_pieces))
        sl = slice(g * POOL_GROUP_WIDTH, (g + 1) * POOL_GROUP_WIDTH)
        win = ue_ref[:, sl]
        span = 1
        while span < window:
            win = win + pltpu.roll(win, span, axis=0)
            span *= 2
        mean = win[POOL_HALO:, :] / jnp.minimum(pos, float(window))
        pooled = (mean - u[:, sl]).astype(BF16)
        mixed.append((jnp.dot(pooled, wpool_ref[g], preferred_element_type=F32)
                      * pscale_ref[:, sl]).astype(BF16))
    ue_ref[0:POOL_HALO, :] = u[tm - POOL_HALO:tm, :]
    hpool = jnp.concatenate(mixed, axis=1)

    for cs in col_pieces:
        pool_piece(groups[0], cs, hpool)
    for prev, rs in zip(groups[:-1], groups[1:]):
        half = len(col_pieces) // 2
        for cs in col_pieces[:half]:
            pool_piece(rs, cs, hpool)
        finish(prev)
        for cs in col_pieces[half:]:
            pool_piece(rs, cs, hpool)
    finish(groups[-1])


def _mix(hml, u, x2, w_pool, pool_scale, wo_ml, wo_pool, g_post, g_next, w_next, seq):
    m, d = x2.shape
    tm = MIX_TM
    wr = w_next.shape[0] // (m // tm)
    const = lambda shape: pl.BlockSpec(shape, lambda i: (0,) * len(shape),
                                       pipeline_mode=pl.Buffered(1))
    return pl.pallas_call(
        functools.partial(_mix_kernel, tiles_per_seq=seq // tm),
        grid=(m // tm,),
        in_specs=[
            pl.BlockSpec((tm, MLSTM_WIDTH), lambda i: (i, 0)),
            pl.BlockSpec((tm, POOL_WIDTH), lambda i: (i, 0)),
            pl.BlockSpec((tm, d), lambda i: (i, 0)),
            const(w_pool.shape),
            const((1, POOL_WIDTH)),
            const(wo_ml.shape),
            const(wo_pool.shape),
            const((1, d)),
            const((1, d)),
            pl.BlockSpec((wr, w_next.shape[1]), lambda i: (i, 0)),
        ],
        out_specs=[pl.BlockSpec((tm, d), lambda i: (i, 0)), pl.BlockSpec((tm, d), lambda i: (i, 0)),
                   pl.BlockSpec((wr, w_next.shape[1]), lambda i: (i, 0))],
        out_shape=[jax.ShapeDtypeStruct((m, d), F32), jax.ShapeDtypeStruct((m, d), BF16),
                   jax.ShapeDtypeStruct(w_next.shape, BF16)],
        scratch_shapes=[pltpu.VMEM((POOL_HALO + tm, POOL_WIDTH), F32), pltpu.VMEM((tm, d), F32)],
        compiler_params=pltpu.CompilerParams(
            dimension_semantics=("arbitrary",), vmem_limit_bytes=VMEM_LIMIT),
        name="mix",
    )(hml, u, x2, w_pool, pool_scale, wo_ml, wo_pool, g_post, g_next, w_next)


def _ffn_kernel(hn_hbm, w1_ref, w2_ref, acc_ref, hn_buf, hn_sem):
    i = pl.program_id(0)
    j = pl.program_id(1)
    slot = i % 2

    def hn_copy(tile, s):
        rows = pl.ds(pl.multiple_of(tile * FFN_TM, FFN_TM), FFN_TM)
        return pltpu.make_async_copy(hn_hbm.at[rows, :], hn_buf.at[s], hn_sem.at[s])

    @pl.when((i == 0) & (j == 0))
    def _():
        hn_copy(0, 0).start()

    @pl.when(j == 0)
    def _():
        hn_copy(i, slot).wait()

    @pl.when((j == pl.num_programs(1) // 2) & (i + 1 < pl.num_programs(0)))
    def _():
        hn_copy(i + 1, 1 - slot).start()

    hn_ref = hn_buf.at[slot]
    groups = _row_groups(FFN_TM)
    w1 = w1_ref[...].astype(BF16)
    acts = []
    for g in groups:
        a = jnp.dot(hn_ref[g, :], w1, preferred_element_type=F32)
        acts.append(jnp.square(jnp.maximum(a, 0.0)).astype(BF16))
    for g, a in zip(groups, acts):
        prev = jnp.where(j == 0, 0.0, acc_ref[g, :])
        acc_ref[g, :] = prev + jnp.dot(a, w2_ref[...], preferred_element_type=F32)


def _ffn(hn, w1, w2):
    m, d = hn.shape
    f = w1.shape[1]
    return pl.pallas_call(
        _ffn_kernel,
        grid=(m // FFN_TM, f // FFN_TF),
        in_specs=[
            pl.BlockSpec(memory_space=pl.ANY),
            pl.BlockSpec((d, FFN_TF), lambda i, j: (0, j)),
            pl.BlockSpec((FFN_TF, d), lambda i, j: (j, 0)),
        ],
        out_specs=pl.BlockSpec((FFN_TM, d), lambda i, j: (i, 0)),
        out_shape=jax.ShapeDtypeStruct((m, d), F32),
        scratch_shapes=[pltpu.VMEM((2, FFN_TM, d), BF16), pltpu.SemaphoreType.DMA((2,))],
        compiler_params=pltpu.CompilerParams(
            dimension_semantics=("arbitrary", "arbitrary"), vmem_limit_bytes=VMEM_LIMIT),
        name="ffn",
    )(hn, w1, w2)


def _ple_kernel(h_ref, ff_ref, p_ref, gff_ref, ggate_ref, wgate_ref, wproj_ref, gpost_ref, out_ref,
                hn_ref, z_ref):
    groups = _row_groups(h_ref.shape[0])
    d = out_ref.shape[1]
    col_pieces = [slice(n * MXU_COLS, (n + 1) * MXU_COLS) for n in range(d // MXU_COLS)]
    half = len(col_pieces) // 2

    def prologue(rs):
        h = h_ref[rs, :] + _rms(ff_ref[rs, :], gff_ref[...])
        out_ref[rs, :] = h
        hn_ref[rs, :] = _rms(h, ggate_ref[...]).astype(BF16)

    def gate_piece(rs, cs):
        z_ref[rs, cs] = jnp.dot(hn_ref[rs, :], wgate_ref[:, cs], preferred_element_type=F32)

    def finish(rs):
        e = jnp.dot(p_ref[rs, :].astype(BF16), wproj_ref[...], preferred_element_type=F32)
        out_ref[rs, :] += _rms(e * _sigmoid(z_ref[rs, :]), gpost_ref[...])

    prologue(groups[0])
    for prev, rs in zip(groups[:-1], groups[1:]):
        for cs in col_pieces[:half]:
            gate_piece(prev, cs)
        prologue(rs)
        for cs in col_pieces[half:]:
            gate_piece(prev, cs)
        for cs in col_pieces[:half]:
            gate_piece(rs, cs)
        finish(prev)
        for cs in col_pieces[half:]:
            gate_piece(rs, cs)
    finish(groups[-1])


def _ple(h, ff, p2, g_ff_post, g_gate, w_gate, w_proj, g_post):
    m, d = h.shape
    pd = p2.shape[1]
    tm = PLE_TM
    const = lambda shape: pl.BlockSpec(shape, lambda i: (0,) * len(shape),
                                       pipeline_mode=pl.Buffered(1))
    return pl.pallas_call(
        _ple_kernel,
        grid=(m // tm,),
        in_specs=[
            pl.BlockSpec((tm, d), lambda i: (i, 0)),
            pl.BlockSpec((tm, d), lambda i: (i, 0)),
            pl.BlockSpec((tm, pd), lambda i: (i, 0)),
            const((1, d)),
            const((1, d)),
            const(w_gate.shape),
            const(w_proj.shape),
            const((1, d)),
        ],
        out_specs=pl.BlockSpec((tm, d), lambda i: (i, 0)),
        out_shape=jax.ShapeDtypeStruct((m, d), F32),
        scratch_shapes=[pltpu.VMEM((tm, d), BF16), pltpu.VMEM((tm, d), F32)],
        compiler_params=pltpu.CompilerParams(
            dimension_semantics=("arbitrary",), vmem_limit_bytes=VMEM_LIMIT),
        name="ple",
    )(h, ff, p2, g_ff_post, g_gate, w_gate, w_proj, g_post)


def _tile_qk_columns(w_q, w_k):
    r = w_q.shape[0]
    a = jnp.stack([w_q.reshape(r, COL_TILES, PAIR), w_k.reshape(r, COL_TILES, PAIR)], axis=2)
    return a.reshape(r, 2 * MLSTM_WIDTH)


def _layer(h, p2, seq, w_in, b_gates, w_qk_conv, g_mlstm, w_pool, pool_scale, w_out,
           g_mix_pre, g_mix_post, w_ff1, w_ff2, g_ff_pre, g_ff_post, w_ple_proj, w_ple_gate,
           g_ple_gate, g_ple_post):
    m, d = h.shape
    assert seq % ROW_TILE == 0 and seq % MIX_TM == 0 and m % FFN_TM == 0 and m % PLE_TM == 0
    assert w_in.shape == (d, 4 * MLSTM_WIDTH + 2 * HEADS + POOL_WIDTH) and w_ff1.shape[1] % FFN_TF == 0
    qkvo = 4 * MLSTM_WIDTH
    row = lambda v: v.reshape(1, -1).astype(F32)
    w_in16 = w_in.astype(BF16)
    w_u16 = w_in16[:, qkvo + 2 * HEADS:]
    w_gi = w_in[:, qkvo:qkvo + HEADS]
    w_gf = w_in[:, qkvo + HEADS:qkvo + 2 * HEADS]
    lane_pad = lambda w: jnp.pad(w, ((0, 0), (0, LANES - HEADS)))
    w_gcol = jnp.concatenate([lane_pad(w_gi), lane_pad(w_gf)], axis=1).astype(BF16)
    b_col = jnp.concatenate([lane_pad(b_gates[None, :HEADS]), lane_pad(b_gates[None, HEADS:])],
                            axis=1).astype(F32)
    b_row = b_gates.reshape(2 * HEADS, 1).astype(F32)
    wqk = _tile_qk_columns(w_qk_conv[:, :MLSTM_WIDTH], w_qk_conv[:, MLSTM_WIDTH:]).astype(F32)

    u, hml = _proj_mlstm(h, row(g_mix_pre), w_in16, w_u16, w_gcol, b_col, b_row, wqk,
                         row(g_mlstm), seq)
    w_out16 = w_out.astype(BF16)
    h, hn, w_ff2_16 = _mix(hml, u, h, w_pool.astype(BF16), row(pool_scale), w_out16[:MLSTM_WIDTH],
                           w_out16[MLSTM_WIDTH:], row(g_mix_post), row(g_ff_pre), w_ff2, seq)
    ff = _ffn(hn, w_ff1, w_ff2_16)
    return _ple(h, ff, p2, row(g_ff_post), row(g_ple_gate), w_ple_gate.astype(BF16),
                w_ple_proj.astype(BF16), row(g_ple_post))


def kernel(x, p, w_in, b_gates, w_qk_conv, g_mlstm, w_pool, pool_scale, w_out, g_mix_pre,
           g_mix_post, w_ff1, w_ff2, g_ff_pre, g_ff_post, w_ple_proj, w_ple_gate, g_ple_gate,
           g_ple_post):
    batch, seq, d = x.shape
    h = x.reshape(batch * seq, d)
    for i in range(p.shape[0]):
        h = _layer(h, p[i].reshape(batch * seq, -1), seq, w_in[i], b_gates[i],
                   w_qk_conv[i], g_mlstm[i], w_pool[i], pool_scale[i], w_out[i], g_mix_pre[i],
                   g_mix_post[i], w_ff1[i], w_ff2[i], g_ff_pre[i], g_ff_post[i], w_ple_proj[i],
                   w_ple_gate[i], g_ple_gate[i], g_ple_post[i])
    return h.reshape(batch, seq, d)
```
